```python
import math
import jax, jax.numpy as jnp
from jax import lax
import numpy as np

D_MODEL = 2048
BATCH = 8
SEQ = 4096
DEPTH = 1
DEC_BATCH = 1
DEC_SEQ = 16384
PAST_LEN = 128

RET_HEADS = 8
RET_QK_DIM = 128
RET_V_DIM = 256
RET_QK_WIDTH = RET_HEADS * RET_QK_DIM
RET_V_WIDTH = RET_HEADS * RET_V_DIM
RET_CHUNK = 128
ROPE_BASE = 10000.0
LRU_WIDTH = D_MODEL
LRU_BLOCKS = 16
LRU_BLOCK_DIM = LRU_WIDTH // LRU_BLOCKS
CONV_WIDTH = 4
CONV_LEFT = 2
LRU_C = 8.0
N_EXPERTS = 32
TOP_K = 4
D_FF = D_MODEL
SWIGLU_LIMIT = 7.0
SWIGLU_ALPHA = 1.702
MOE_BLOCK = 128
EPS = 1e-6
SPLIT_Q = RET_QK_WIDTH
SPLIT_K = SPLIT_Q + RET_QK_WIDTH
SPLIT_V = SPLIT_K + RET_V_WIDTH
SPLIT_GR = SPLIT_V + RET_V_WIDTH
SPLIT_XL = SPLIT_GR + LRU_WIDTH
SPLIT_GL = SPLIT_XL + LRU_WIDTH
SPLIT_MR = SPLIT_GL + D_MODEL
IN_WIDTH = SPLIT_MR + D_MODEL

kernel_name = "hybrid_retention_rglru_moe_encoder"


def rms_norm(x, g):
    xf = x.astype(jnp.float32)
    y = xf * lax.rsqrt(jnp.mean(xf * xf, axis=-1, keepdims=True) + EPS)
    return (y * g.astype(jnp.float32)).astype(x.dtype)


def rotary(x):
    s, d = x.shape[1], x.shape[-1]
    half = d // 2
    inv = ROPE_BASE ** (-jnp.arange(half, dtype=jnp.float32) / half)
    ang = jnp.arange(s, dtype=jnp.float32)[:, None] * inv[None, :]
    cos = jnp.cos(ang)[None, :, None, :]
    sin = jnp.sin(ang)[None, :, None, :]
    xf = x.astype(jnp.float32)
    x1, x2 = xf[..., :half], xf[..., half:]
    return jnp.concatenate([x1 * cos - x2 * sin, x2 * cos + x1 * sin], axis=-1).astype(x.dtype)


def retention(q, k, v):
    b, s, h, dk = q.shape
    dv = v.shape[-1]
    c = RET_CHUNK
    n = s // c
    log_gamma = jnp.log1p(-jnp.exp2(-5.0 - jnp.arange(h, dtype=jnp.float32)))

    def to_chunks(t):
        return t.astype(jnp.float32).reshape(b, n, c, h, t.shape[-1]).transpose(1, 0, 3, 2, 4)

    qc = to_chunks(q)
    kc = to_chunks(k) * (dk ** -0.5)
    vc = to_chunks(v)
    pos = jnp.arange(c, dtype=jnp.float32)
    lg = log_gamma[:, None, None]
    decay_intra = jnp.exp(lg * jnp.abs(pos[:, None] - pos[None, :]))
    scores = jnp.einsum('nbhid,nbhjd->nbhij', qc, kc) * decay_intra
    intra = jnp.einsum('nbhij,nbhjv->nbhiv', scores, vc)

    p = pos[None, :, None]
    q_fwd = qc * jnp.exp(lg * (p + 1.0))
    k_fwd = kc * jnp.exp(lg * (c - 1.0 - p))
    q_bwd = qc * jnp.exp(lg * (c - p))
    k_bwd = kc * jnp.exp(lg * p)
    chunk_decay = jnp.exp(lg * c)

    def step(state, xs):
        qd, kd, vch = xs
        out = jnp.einsum('bhid,bhdv->bhiv', qd, state)
        state = chunk_decay * state + jnp.einsum('bhjd,bhjv->bhdv', kd, vch)
        return state, out

    s0 = jnp.zeros((b, h, dk, dv), jnp.float32)
    _, fwd = lax.scan(step, s0, (q_fwd, k_fwd, vc))
    _, bwd = lax.scan(step, s0, (q_bwd, k_bwd, vc), reverse=True)
    out = intra + fwd + bwd
    return out.transpose(1, 0, 3, 2, 4).reshape(b, s, h, dv)


def head_group_norm(y, g):
    mu = jnp.mean(y, axis=-1, keepdims=True)
    var = jnp.mean(jnp.square(y - mu), axis=-1, keepdims=True)
    yn = (y - mu) * lax.rsqrt(var + EPS)
    return yn.reshape(y.shape[0], y.shape[1], -1) * g.astype(jnp.float32)


def centred_depthwise_conv(x, w, b):
    s = x.shape[1]
    xp = jnp.pad(x, ((0, 0), (CONV_LEFT, CONV_WIDTH - 1 - CONV_LEFT), (0, 0)))
    return sum(xp[:, t:t + s] * w[t] for t in range(CONV_WIDTH)) + b


def block_diag_linear(x, w, b):
    xb = x.reshape(x.shape[0], x.shape[1], LRU_BLOCKS, LRU_BLOCK_DIM)
    return jnp.einsum('bsgi,gij->bsgj', xb, w).reshape(x.shape) + b


def rg_lru(x, w_a, b_a, w_x, b_x, lam, reverse):
    r = jax.nn.sigmoid(block_diag_linear(x, w_a, b_a)).astype(jnp.float32)
    i = jax.nn.sigmoid(block_diag_linear(x, w_x, b_x)).astype(jnp.float32)
    log_a = -LRU_C * r * jax.nn.softplus(-lam.astype(jnp.float32))
    a = jnp.exp(log_a)
    u = jnp.sqrt(-jnp.expm1(2.0 * log_a)) * (i * x.astype(jnp.float32))

    def combine(c1, c2):
        a1, b1 = c1
        a2, b2 = c2
        return a1 * a2, a2 * b1 + b2

    _, h = lax.associative_scan(combine, (a, u), axis=1, reverse=reverse)
    return h


def token_mixer(xn, w_in, ret_norm_g, w_ret_o, conv_w, conv_b, lru_w_a, lru_b_a, lru_w_x, lru_b_x,
                lru_lambda, w_lru_o, w_out):
    b, s, _ = xn.shape
    z = xn @ w_in
    q, k, v, g_ret, x_lru, g_lru, m_ret, m_lru = jnp.split(
        z, [SPLIT_Q, SPLIT_K, SPLIT_V, SPLIT_GR, SPLIT_XL, SPLIT_GL, SPLIT_MR], axis=-1)
    q = rotary(q.reshape(b, s, RET_HEADS, RET_QK_DIM))
    k = rotary(k.reshape(b, s, RET_HEADS, RET_QK_DIM))
    y_ret = retention(q, k, v.reshape(b, s, RET_HEADS, RET_V_DIM))
    y_ret = (head_group_norm(y_ret, ret_norm_g) * jax.nn.silu(g_ret.astype(jnp.float32))).astype(xn.dtype)
    ret_branch = y_ret @ w_ret_o
    xc = centred_depthwise_conv(x_lru, conv_w, conv_b)
    h = (rg_lru(xc, lru_w_a[0], lru_b_a[0], lru_w_x[0], lru_b_x[0], lru_lambda[0], False)
         + rg_lru(xc, lru_w_a[1], lru_b_a[1], lru_w_x[1], lru_b_x[1], lru_lambda[1], True))
    y_lru = (h * jax.nn.gelu(g_lru.astype(jnp.float32), approximate=True)).astype(xn.dtype)
    lru_branch = y_lru @ w_lru_o
    merged = jax.nn.sigmoid(m_ret) * ret_branch + jax.nn.sigmoid(m_lru) * lru_branch
    return merged @ w_out


def moe(xn, w_router, b_router, w_e_gate, b_e_gate, w_e_up, b_e_up, w_e_down, b_e_down):
    b, s, d = xn.shape
    xt = xn.reshape(-1, d)
    t = xt.shape[0]
    logits = (xt @ w_router + b_router).astype(jnp.float32)
    top_val, top_idx = lax.top_k(logits, TOP_K)
    gates = jax.nn.softmax(top_val, axis=-1)
    a = t * TOP_K
    e_flat = top_idx.reshape(a)
    tok_flat = jnp.arange(a, dtype=jnp.int32) // TOP_K
    g_flat = gates.reshape(a)
    order = jnp.argsort(e_flat, stable=True)
    e_sorted = e_flat[order]
    counts = jnp.bincount(e_flat, length=N_EXPERTS)
    starts = jnp.cumsum(counts) - counts
    padded = ((counts + MOE_BLOCK - 1) // MOE_BLOCK) * MOE_BLOCK
    pad_ends = jnp.cumsum(padded)
    pad_starts = pad_ends - padded
    dest = pad_starts[e_sorted] + (jnp.arange(a, dtype=jnp.int32) - starts[e_sorted])
    n_blocks = (a + MOE_BLOCK - 1) // MOE_BLOCK + N_EXPERTS
    rows = n_blocks * MOE_BLOCK
    row_token = jnp.zeros((rows,), jnp.int32).at[dest].set(tok_flat[order])
    row_gate = jnp.zeros((rows,), jnp.float32).at[dest].set(g_flat[order])
    block_expert = jnp.minimum(
        jnp.searchsorted(pad_ends, jnp.arange(n_blocks, dtype=pad_ends.dtype) * MOE_BLOCK, side='right'),
        N_EXPERTS - 1)

    def expert_block(acc, blk):
        e, toks, gw = blk
        xb = xt[toks]
        gate = jnp.minimum(xb @ w_e_gate[e] + b_e_gate[e], SWIGLU_LIMIT)
        up = jnp.clip(xb @ w_e_up[e] + b_e_up[e], -SWIGLU_LIMIT, SWIGLU_LIMIT)
        hid = (up + 1.0) * gate * jax.nn.sigmoid(SWIGLU_ALPHA * gate)
        out = hid @ w_e_down[e] + b_e_down[e]
        return acc.at[toks].add(out * gw[:, None].astype(out.dtype)), None

    y, _ = lax.scan(expert_block, jnp.zeros_like(xt),
                    (block_expert, row_token.reshape(n_blocks, MOE_BLOCK), row_gate.reshape(n_blocks, MOE_BLOCK)))
    return y.reshape(b, s, d)


def setup_inputs(seed: int = 0) -> dict:
    key = jax.random.key(seed)
    ks = jax.random.split(key, 32)
    f32 = jnp.float32

    def nrm(k, shape, scale):
        return jax.random.normal(k, shape, f32) * scale

    def gain(k, shape):
        return 1.0 + 0.01 * jax.random.normal(k, shape, f32)

    a0 = jax.random.uniform(ks[13], (DEPTH, 2, LRU_WIDTH), f32, 0.9, 0.999)
    root = a0 ** (1.0 / LRU_C)
    lru_lambda = jnp.log(root) - jnp.log1p(-root)
    return {
        "x_prompt": nrm(ks[0], (BATCH, SEQ, D_MODEL), 1.0),
        "x_sample": nrm(ks[1], (DEC_BATCH, DEC_SEQ, D_MODEL), 1.0),
        "mix_norm_g": gain(ks[2], (DEPTH, D_MODEL)),
        "w_in": nrm(ks[3], (DEPTH, D_MODEL, IN_WIDTH), D_MODEL ** -0.5),
        "ret_norm_g": gain(ks[4], (DEPTH, RET_V_WIDTH)),
        "w_ret_o": nrm(ks[5], (DEPTH, RET_V_WIDTH, D_MODEL), RET_V_WIDTH ** -0.5),
        "conv_w": nrm(ks[6], (DEPTH, CONV_WIDTH, LRU_WIDTH), CONV_WIDTH ** -0.5),
        "conv_b": nrm(ks[7], (DEPTH, LRU_WIDTH), 0.01),
        "lru_w_a": nrm(ks[8], (DEPTH, 2, LRU_BLOCKS, LRU_BLOCK_DIM, LRU_BLOCK_DIM), LRU_BLOCK_DIM ** -0.5),
        "lru_b_a": nrm(ks[9], (DEPTH, 2, LRU_WIDTH), 0.01),
        "lru_w_x": nrm(ks[10], (DEPTH, 2, LRU_BLOCKS, LRU_BLOCK_DIM, LRU_BLOCK_DIM), LRU_BLOCK_DIM ** -0.5),
        "lru_b_x": nrm(ks[11], (DEPTH, 2, LRU_WIDTH), 0.01),
        "lru_lambda": lru_lambda,
        "w_lru_o": nrm(ks[12], (DEPTH, LRU_WIDTH, D_MODEL), LRU_WIDTH ** -0.5),
        "w_out": nrm(ks[14], (DEPTH, D_MODEL, D_MODEL), D_MODEL ** -0.5),
        "moe_norm_g": gain(ks[15], (DEPTH, D_MODEL)),
        "w_router": nrm(ks[16], (DEPTH, D_MODEL, N_EXPERTS), D_MODEL ** -0.5),
        "b_router": nrm(ks[17], (DEPTH, N_EXPERTS), 0.01),
        "w_e_gate": nrm(ks[18], (DEPTH, N_EXPERTS, D_MODEL, D_FF), D_MODEL ** -0.5),
        "b_e_gate": nrm(ks[19], (DEPTH, N_EXPERTS, D_FF), 0.01),
        "w_e_up": nrm(ks[20], (DEPTH, N_EXPERTS, D_MODEL, D_FF), D_MODEL ** -0.5),
        "b_e_up": nrm(ks[21], (DEPTH, N_EXPERTS, D_FF), 0.01),
        "w_e_down": nrm(ks[22], (DEPTH, N_EXPERTS, D_FF, D_MODEL), D_FF ** -0.5),
        "b_e_down": nrm(ks[23], (DEPTH, N_EXPERTS, D_MODEL), 0.01),
        "final_norm_g": gain(ks[24], (D_MODEL,)),
    }


def reference(x_prompt, x_sample, mix_norm_g, w_in, ret_norm_g, w_ret_o, conv_w, conv_b, lru_w_a, lru_b_a,
              lru_w_x, lru_b_x, lru_lambda, w_lru_o, w_out, moe_norm_g, w_router, b_router, w_e_gate, b_e_gate,
              w_e_up, b_e_up, w_e_down, b_e_down, final_norm_g):
    def run(x):
        for l in range(DEPTH):
            x = x + token_mixer(rms_norm(x, mix_norm_g[l]), w_in[l], ret_norm_g[l], w_ret_o[l], conv_w[l],
                                conv_b[l], lru_w_a[l], lru_b_a[l], lru_w_x[l], lru_b_x[l], lru_lambda[l],
                                w_lru_o[l], w_out[l])
            x = x + moe(rms_norm(x, moe_norm_g[l]), w_router[l], b_router[l], w_e_gate[l], b_e_gate[l],
                        w_e_up[l], b_e_up[l], w_e_down[l], b_e_down[l])
        return rms_norm(x, final_norm_g)

    y_prompt = run(x_prompt)
    y_sample = run(x_sample)
    return (y_prompt, y_sample)
```

```python
import functools
import math

import jax
import jax.numpy as jnp
from jax import lax
from jax.experimental import pallas as pl
from jax.experimental.pallas import tpu as pltpu

F32 = jnp.float32
BF16 = jnp.bfloat16
U32 = jnp.uint32
I32 = jnp.int32

D = 2048
HEADS = 8
DK = 128
DV = 256
QK_W = HEADS * DK
V_W = HEADS * DV
LRU_BLOCK = 128
N_EXP = 32
TOP_K = 4
D_FF = 2048
SWIGLU_LIMIT = 7.0
SWIGLU_ALPHA = 1.702
ROPE_BASE = 10000.0
LRU_C = 8.0
EPS = 1e-6
IN_W = 2 * QK_W + 2 * V_W + 2 * D + 2 * D

HALF = D // 2
LANE = 128
HALO = 16
HI_MASK = 0xFFFF0000

SEQ_CHUNK = 256
LRU_CB = 512
MOE_ROWS = 512
MOE_TF = 512
TOK_TILE = 512
CMB_TILE = 256


def _cp(sem, vmem_mb):
    return pltpu.CompilerParams(dimension_semantics=sem, vmem_limit_bytes=vmem_mb << 20)


def _bits(x):
    return lax.bitcast_convert_type(x, U32)


def _pack_pair(lo, hi):
    ulo = _bits(lo.astype(BF16).astype(F32))
    uhi = _bits(hi.astype(BF16).astype(F32))
    return (ulo >> 16) | uhi


def _unpack_pair(u):
    lo = lax.bitcast_convert_type(u << 16, F32)
    hi = lax.bitcast_convert_type(u & jnp.uint32(HI_MASK), F32)
    return lo, hi


class _Seqs:
    def __init__(self, n_prompt, prompt_len, sample_len, chunk):
        assert prompt_len % chunk == 0 and sample_len % chunk == 0
        self.cps = prompt_len // chunk
        self.npc = n_prompt * self.cps
        self.nch = self.npc + sample_len // chunk

    def is_first(self, ci):
        return ((ci < self.npc) & (ci % self.cps == 0)) | (ci == self.npc)

    def is_last(self, ci):
        return ((ci < self.npc) & (ci % self.cps == self.cps - 1)) | (ci == self.nch - 1)


def _inproj_kernel(x_ref, g_ref, w_ref, o_ref, xn_ref):
    @pl.when(pl.program_id(1) == 0)
    def _():
        x = x_ref[...]
        y = x * lax.rsqrt(jnp.mean(x * x, axis=-1, keepdims=True) + EPS)
        xn_ref[...] = (y * g_ref[...]).astype(BF16)

    o_ref[...] = jnp.dot(xn_ref[...], w_ref[...], preferred_element_type=F32).astype(o_ref.dtype)


def _inproj(x, g, w, tm=1024, tn=1024):
    t, n = x.shape[0], w.shape[1]
    return pl.pallas_call(
        _inproj_kernel,
        grid=(t // tm, n // tn),
        in_specs=[pl.BlockSpec((tm, D), lambda i, j: (i, 0)),
                  pl.BlockSpec((1, D), lambda i, j: (0, 0)),
                  pl.BlockSpec((D, tn), lambda i, j: (0, j))],
        out_specs=pl.BlockSpec((tm, tn), lambda i, j: (i, j)),
        out_shape=jax.ShapeDtypeStruct((t, n), BF16),
        scratch_shapes=[pltpu.VMEM((tm, D), BF16)],
        compiler_params=_cp(("parallel", "arbitrary"), 48),
        name="inproj",
    )(x, g, w)


def _rot(x, cosf, sinf):
    return x * cosf + pltpu.roll(x, DK // 2, 1) * sinf


def _log_gamma(h):
    return math.log1p(-(2.0 ** (-5 - h)))


def _ret_bwd_kernel(q_ref, k_ref, v_ref, cos_ref, sin_ref, o_ref, s_ref, *, seqs, c):
    ci = seqs.nch - 1 - pl.program_id(0)

    @pl.when(seqs.is_last(ci))
    def _():
        s_ref[...] = jnp.zeros_like(s_ref)

    cosf = cos_ref[...]
    sinf = sin_ref[...]
    p = lax.broadcasted_iota(I32, (c, 1), 0).astype(F32)
    for h in range(HEADS):
        lg = _log_gamma(h)
        q = _rot(q_ref[:, h * DK:(h + 1) * DK].astype(F32), cosf, sinf)
        k = _rot(k_ref[:, h * DK:(h + 1) * DK].astype(F32), cosf, sinf) * (DK ** -0.5)
        v = v_ref[:, h * DV:(h + 1) * DV]
        qd = (q * jnp.exp(lg * (c - p))).astype(BF16)
        kd = (k * jnp.exp(lg * p)).astype(BF16)
        s = s_ref[h]
        o_ref[:, h * DV:(h + 1) * DV] = jnp.dot(qd, s.astype(BF16), preferred_element_type=F32)
        s_ref[h] = math.exp(lg * c) * s + lax.dot_general(
            kd, v, (((0,), (0,)), ((), ())), preferred_element_type=F32)


def _ret_fwd_kernel(q_ref, k_ref, v_ref, cos_ref, sin_ref, bwd_ref, gate_ref, gn_ref, o_ref, s_ref, dm_ref,
                    *, seqs, c):
    ci = pl.program_id(0)

    @pl.when(ci == 0)
    def _():
        r = lax.broadcasted_iota(I32, (c, c), 0)
        col = lax.broadcasted_iota(I32, (c, c), 1)
        dist = jnp.abs(r - col).astype(F32)
        for h in range(HEADS):
            dm_ref[h] = jnp.exp(_log_gamma(h) * dist)

    @pl.when(seqs.is_first(ci))
    def _():
        s_ref[...] = jnp.zeros_like(s_ref)

    cosf = cos_ref[...]
    sinf = sin_ref[...]
    p = lax.broadcasted_iota(I32, (c, 1), 0).astype(F32)
    for h in range(HEADS):
        lg = _log_gamma(h)
        q = _rot(q_ref[:, h * DK:(h + 1) * DK].astype(F32), cosf, sinf)
        k = _rot(k_ref[:, h * DK:(h + 1) * DK].astype(F32), cosf, sinf) * (DK ** -0.5)
        v = v_ref[:, h * DV:(h + 1) * DV]
        scores = lax.dot_general(q.astype(BF16), k.astype(BF16), (((1,), (1,)), ((), ())),
                                 preferred_element_type=F32) * dm_ref[h]
        tot = jnp.dot(scores.astype(BF16), v, preferred_element_type=F32)
        qd = (q * jnp.exp(lg * (p + 1.0))).astype(BF16)
        kd = (k * jnp.exp(lg * (c - 1.0 - p))).astype(BF16)
        s = s_ref[h]
        tot = tot + jnp.dot(qd, s.astype(BF16), preferred_element_type=F32)
        s_ref[h] = math.exp(lg * c) * s + lax.dot_general(
            kd, v, (((0,), (0,)), ((), ())), preferred_element_type=F32)
        tot = tot + bwd_ref[:, h * DV:(h + 1) * DV]
        mu = jnp.mean(tot, axis=-1, keepdims=True)
        cen = tot - mu
        var = jnp.mean(cen * cen, axis=-1, keepdims=True)
        yn = cen * lax.rsqrt(var + EPS) * gn_ref[:, h * DV:(h + 1) * DV]
        g = gate_ref[:, h * DV:(h + 1) * DV].astype(F32)
        o_ref[:, h * DV:(h + 1) * DV] = (yn * (g * jax.nn.sigmoid(g))).astype(o_ref.dtype)


def _retention(z, cosf, sinf, gn, seqs, c):
    t = z.shape[0]
    nch = seqs.nch
    rev = lambda i: nch - 1 - i
    qkv_specs = lambda f: [pl.BlockSpec((c, QK_W), lambda i: (f(i), 0)),
                           pl.BlockSpec((c, QK_W), lambda i: (f(i), 1)),
                           pl.BlockSpec((c, V_W), lambda i: (f(i), 1)),
                           pl.BlockSpec((c, DK), lambda i: (f(i), 0)),
                           pl.BlockSpec((c, DK), lambda i: (f(i), 0))]
    bwd = pl.pallas_call(
        functools.partial(_ret_bwd_kernel, seqs=seqs, c=c),
        grid=(nch,),
        in_specs=qkv_specs(rev),
        out_specs=pl.BlockSpec((c, V_W), lambda i: (rev(i), 0)),
        out_shape=jax.ShapeDtypeStruct((t, V_W), F32),
        scratch_shapes=[pltpu.VMEM((HEADS, DK, DV), F32)],
        compiler_params=_cp(("arbitrary",), 32),
        name="ret_bwd",
    )(z, z, z, cosf, sinf)
    fwd_id = lambda i: i
    return pl.pallas_call(
        functools.partial(_ret_fwd_kernel, seqs=seqs, c=c),
        grid=(nch,),
        in_specs=qkv_specs(fwd_id) + [pl.BlockSpec((c, V_W), lambda i: (i, 0)),
                                      pl.BlockSpec((c, V_W), lambda i: (i, 2)),
                                      pl.BlockSpec((1, V_W), lambda i: (0, 0))],
        out_specs=pl.BlockSpec((c, V_W), lambda i: (i, 0)),
        out_shape=jax.ShapeDtypeStruct((t, V_W), BF16),
        scratch_shapes=[pltpu.VMEM((HEADS, DK, DV), F32), pltpu.VMEM((HEADS, c, c), F32)],
        compiler_params=_cp(("arbitrary",), 40),
        name="ret_fwd",
    )(z, z, z, cosf, sinf, bwd, z, gn)


def _lru_kernel(*refs, reverse, seqs, c, cb):
    if reverse:
        (x_ref, xp_ref, xn_ref, cw_ref, cbias_ref, wa_ref, ba_ref, wx_ref, bx_ref, c8_ref, hf_ref, g_ref,
         o_ref, a_s, u_s, h_s, carry_s) = refs
    else:
        (x_ref, xp_ref, xn_ref, cw_ref, cbias_ref, wa_ref, ba_ref, wx_ref, bx_ref, c8_ref,
         o_ref, a_s, u_s, h_s, carry_s) = refs
    t = pl.program_id(1)
    ci = seqs.nch - 1 - t if reverse else t
    first = seqs.is_first(ci)
    last = seqs.is_last(ci)

    x = x_ref[...].astype(F32)
    prev = jnp.where(first, 0.0, xp_ref[...].astype(F32))
    nxt = jnp.where(last, 0.0, xn_ref[...].astype(F32))
    row = lax.broadcasted_iota(I32, (c, 1), 0)
    xm1 = jnp.where(row == 0, prev[HALO - 1:HALO], pltpu.roll(x, 1, 0))
    xm2 = jnp.where(row == 0, prev[HALO - 2:HALO - 1],
                    jnp.where(row == 1, prev[HALO - 1:HALO], pltpu.roll(x, 2, 0)))
    xp1 = jnp.where(row == c - 1, nxt[0:1], pltpu.roll(x, c - 1, 0))
    cw = cw_ref[...]
    xc = cw[0:1] * xm2 + cw[1:2] * xm1 + cw[2:3] * x + cw[3:4] * xp1 + cbias_ref[...]

    for gi in range(cb // LRU_BLOCK):
        sl = slice(gi * LRU_BLOCK, (gi + 1) * LRU_BLOCK)
        xs = xc[:, sl]
        xb = xs.astype(BF16)
        r = jax.nn.sigmoid(jnp.dot(xb, wa_ref[gi], preferred_element_type=F32) + ba_ref[:, sl])
        ig = jax.nn.sigmoid(jnp.dot(xb, wx_ref[gi], preferred_element_type=F32) + bx_ref[:, sl])
        a = jnp.exp(c8_ref[:, sl] * r)
        a_s[:, sl] = a
        u_s[:, sl] = jnp.sqrt(1.0 - a * a) * (ig * xs)

    @pl.when(last if reverse else first)
    def _():
        carry_s[...] = jnp.zeros_like(carry_s)

    ng = c // 8
    row8 = lax.broadcasted_iota(I32, (8, cb), 0)

    def body(j, carry):
        gidx = ng - 1 - j if reverse else j
        r0 = pl.multiple_of(gidx * 8, 8)
        a = a_s[pl.ds(r0, 8), :]
        u = u_s[pl.ds(r0, 8), :]
        for s in (1, 2, 4):
            shift = 8 - s if reverse else s
            m = (row8 < 8 - s) if reverse else (row8 >= s)
            a_sh = pltpu.roll(a, shift, 0)
            u_sh = pltpu.roll(u, shift, 0)
            u = u + a * jnp.where(m, u_sh, 0.0)
            a = a * jnp.where(m, a_sh, 1.0)
        h = u + a * carry
        h_s[pl.ds(r0, 8), :] = h
        edge = h[0:1, :] if reverse else h[7:8, :]
        return jnp.broadcast_to(edge, (8, cb))

    carry_s[...] = lax.fori_loop(0, ng, body, carry_s[...])

    if reverse:
        g = g_ref[...].astype(F32)
        o_ref[...] = ((hf_ref[...] + h_s[...]) * jax.nn.gelu(g, approximate=True)).astype(o_ref.dtype)
    else:
        o_ref[...] = h_s[...]


def _lru_call(z, conv_w, conv_b, wa, ba, wx, bx, c8, hf, direction, seqs, c, cb):
    t = z.shape[0]
    nch = seqs.nch
    reverse = direction == 1
    tmap = (lambda ti: nch - 1 - ti) if reverse else (lambda ti: ti)
    xcol = (2 * QK_W + 2 * V_W) // cb
    gcol = (2 * QK_W + 2 * V_W + D) // cb
    hb = c // HALO
    nhb = t // HALO
    nb = cb // LRU_BLOCK
    in_specs = [
        pl.BlockSpec((c, cb), lambda ch, ti: (tmap(ti), xcol + ch)),
        pl.BlockSpec((HALO, cb), lambda ch, ti: (jnp.maximum(tmap(ti) * hb - 1, 0), xcol + ch)),
        pl.BlockSpec((HALO, cb), lambda ch, ti: (jnp.minimum((tmap(ti) + 1) * hb, nhb - 1), xcol + ch)),
        pl.BlockSpec((4, cb), lambda ch, ti: (0, ch)),
        pl.BlockSpec((1, cb), lambda ch, ti: (0, ch)),
        pl.BlockSpec((None, nb, LRU_BLOCK, LRU_BLOCK), lambda ch, ti: (direction, ch, 0, 0)),
        pl.BlockSpec((None, 1, cb), lambda ch, ti: (direction, 0, ch)),
        pl.BlockSpec((None, nb, LRU_BLOCK, LRU_BLOCK), lambda ch, ti: (direction, ch, 0, 0)),
        pl.BlockSpec((None, 1, cb), lambda ch, ti: (direction, 0, ch)),
        pl.BlockSpec((None, 1, cb), lambda ch, ti: (direction, 0, ch)),
    ]
    args = [z, z, z, conv_w, conv_b, wa, ba, wx, bx, c8]
    if reverse:
        in_specs += [pl.BlockSpec((c, cb), lambda ch, ti: (tmap(ti), ch)),
                     pl.BlockSpec((c, cb), lambda ch, ti: (tmap(ti), gcol + ch))]
        args += [hf, z]
    return pl.pallas_call(
        functools.partial(_lru_kernel, reverse=reverse, seqs=seqs, c=c, cb=cb),
        grid=(D // cb, nch),
        in_specs=in_specs,
        out_specs=pl.BlockSpec((c, cb), lambda ch, ti: (tmap(ti), ch)),
        out_shape=jax.ShapeDtypeStruct((t, D), BF16 if reverse else F32),
        scratch_shapes=[pltpu.VMEM((c, cb), F32), pltpu.VMEM((c, cb), F32), pltpu.VMEM((c, cb), F32),
                        pltpu.VMEM((8, cb), F32)],
        compiler_params=_cp(("parallel", "arbitrary"), 32),
        name="lru_rev" if reverse else "lru_fwd",
    )(*args)


def _proj_merge_kernel(yr_ref, yl_ref, wr_ref, wl_ref, mr_ref, ml_ref, o_ref):
    r = jnp.dot(yr_ref[...], wr_ref[...], preferred_element_type=F32)
    l = jnp.dot(yl_ref[...], wl_ref[...], preferred_element_type=F32)
    o_ref[...] = (jax.nn.sigmoid(mr_ref[...].astype(F32)) * r
                  + jax.nn.sigmoid(ml_ref[...].astype(F32)) * l).astype(o_ref.dtype)


def _proj_merge(y_ret, y_lru, w_ret_o, w_lru_o, z, tm=512, tn=1024):
    t = y_ret.shape[0]
    mcol = (2 * QK_W + 2 * V_W + 2 * D) // tn
    return pl.pallas_call(
        _proj_merge_kernel,
        grid=(t // tm, D // tn),
        in_specs=[pl.BlockSpec((tm, V_W), lambda i, j: (i, 0)),
                  pl.BlockSpec((tm, D), lambda i, j: (i, 0)),
                  pl.BlockSpec((V_W, tn), lambda i, j: (0, j)),
                  pl.BlockSpec((D, tn), lambda i, j: (0, j)),
                  pl.BlockSpec((tm, tn), lambda i, j: (i, mcol + j)),
                  pl.BlockSpec((tm, tn), lambda i, j: (i, mcol + D // tn + j))],
        out_specs=pl.BlockSpec((tm, tn), lambda i, j: (i, j)),
        out_shape=jax.ShapeDtypeStruct((t, D), BF16),
        compiler_params=_cp(("parallel", "arbitrary"), 48),
        name="proj_merge",
    )(y_ret, y_lru, w_ret_o, w_lru_o, z, z)


def _proj_out_kernel(m_ref, w_ref, x_ref, o_ref):
    o_ref[...] = x_ref[...] + jnp.dot(m_ref[...], w_ref[...], preferred_element_type=F32)


def _proj_out(merged, w_out, x, tm=512, tn=1024):
    t = x.shape[0]
    return pl.pallas_call(
        _proj_out_kernel,
        grid=(t // tm, D // tn),
        in_specs=[pl.BlockSpec((tm, D), lambda i, j: (i, 0)),
                  pl.BlockSpec((D, tn), lambda i, j: (0, j)),
                  pl.BlockSpec((tm, tn), lambda i, j: (i, j))],
        out_specs=pl.BlockSpec((tm, tn), lambda i, j: (i, j)),
        out_shape=jax.ShapeDtypeStruct((t, D), F32),
        compiler_params=_cp(("parallel", "arbitrary"), 48),
        name="proj_out",
    )(merged, w_out, x)


def _router_kernel(x_ref, g_ref, w_ref, b_ref, xp_ref, idx_ref, gate_ref, rank_ref, cnt_ref, tri_ref, run_ref,
                   *, tm):
    i = pl.program_id(0)

    @pl.when(i == 0)
    def _():
        r = lax.broadcasted_iota(I32, (tm, tm), 0)
        col = lax.broadcasted_iota(I32, (tm, tm), 1)
        tri_ref[...] = (col < r).astype(BF16)
        run_ref[...] = jnp.zeros_like(run_ref)

    x = x_ref[...]
    xn = x * lax.rsqrt(jnp.mean(x * x, axis=-1, keepdims=True) + EPS) * g_ref[...]
    xp_ref[...] = _pack_pair(xn[:, :HALF], xn[:, HALF:])

    logits = jnp.dot(xn, w_ref[...], preferred_element_type=F32, precision=lax.Precision.HIGHEST) + b_ref[...]
    lane = lax.broadcasted_iota(I32, (tm, LANE), 1)
    lane_f = lane.astype(F32)
    vals, idxs = [], []
    cur = logits
    for _ in range(TOP_K):
        m = jnp.max(cur, axis=-1, keepdims=True)
        idx = jnp.min(jnp.where(cur == m, lane_f, float(LANE)), axis=-1, keepdims=True).astype(I32)
        vals.append(m)
        idxs.append(idx)
        cur = jnp.where(lane == idx, -jnp.inf, cur)
    exps = [jnp.exp(v - vals[0]) for v in vals]
    denom = exps[0] + exps[1] + exps[2] + exps[3]

    onehot = jnp.zeros((tm, LANE), F32)
    for idx in idxs:
        onehot = onehot + (lane == idx).astype(F32)
    before = jnp.dot(tri_ref[...], onehot.astype(BF16), preferred_element_type=F32) + run_ref[...]

    idx_out = jnp.zeros((tm, LANE), I32)
    gate_out = jnp.zeros((tm, LANE), F32)
    rank_out = jnp.zeros((tm, LANE), I32)
    for k in range(TOP_K):
        rank = jnp.sum(jnp.where(lane == idxs[k], before, 0.0), axis=-1, keepdims=True).astype(I32)
        idx_out = jnp.where(lane == k, idxs[k], idx_out)
        gate_out = jnp.where(lane == k, exps[k] / denom, gate_out)
        rank_out = jnp.where(lane == k, rank, rank_out)
    idx_ref[...] = idx_out
    gate_ref[...] = gate_out
    rank_ref[...] = rank_out

    run = run_ref[...] + jnp.sum(onehot, axis=0, keepdims=True)
    run_ref[...] = run
    cnt_ref[...] = jnp.broadcast_to(run, cnt_ref.shape)


def _router(x1, g, w_pad, b_pad, tm):
    t = x1.shape[0]
    tok = lambda i: (i, 0)
    fixed = lambda i: (0, 0)
    return pl.pallas_call(
        functools.partial(_router_kernel, tm=tm),
        grid=(t // tm,),
        in_specs=[pl.BlockSpec((tm, D), tok), pl.BlockSpec((1, D), fixed),
                  pl.BlockSpec((D, LANE), fixed), pl.BlockSpec((1, LANE), fixed)],
        out_specs=[pl.BlockSpec((tm, HALF), tok), pl.BlockSpec((tm, LANE), tok), pl.BlockSpec((tm, LANE), tok),
                   pl.BlockSpec((tm, LANE), tok), pl.BlockSpec((8, LANE), fixed)],
        out_shape=[jax.ShapeDtypeStruct((t, HALF), U32), jax.ShapeDtypeStruct((t, LANE), I32),
                   jax.ShapeDtypeStruct((t, LANE), F32), jax.ShapeDtypeStruct((t, LANE), I32),
                   jax.ShapeDtypeStruct((8, LANE), F32)],
        scratch_shapes=[pltpu.VMEM((tm, tm), BF16), pltpu.VMEM((1, LANE), F32)],
        compiler_params=_cp(("arbitrary",), 32),
        name="router",
    )(x1, g, w_pad, b_pad)


def _dispatch_kernel(dest_ref, xp_ref, xg_in_ref, xg_ref, sem, *, tm):
    del xg_in_ref

    def row_copy(r, k):
        f = r * TOP_K + k
        d = dest_ref[f // LANE, f % LANE]
        return pltpu.make_async_copy(xp_ref.at[pl.ds(r, 1), :], xg_ref.at[pl.ds(d, 1), :], sem)

    def start(r, carry):
        for k in range(TOP_K):
            row_copy(r, k).start()
        return carry

    def wait(r, carry):
        for k in range(TOP_K):
            row_copy(r, k).wait()
        return carry

    lax.fori_loop(0, tm, start, 0)
    lax.fori_loop(0, tm, wait, 0)


def _dispatch(dest2d, xp, rows, tm):
    t = xp.shape[0]
    sb = tm * TOP_K // LANE
    return pl.pallas_call(
        functools.partial(_dispatch_kernel, tm=tm),
        grid=(t // tm,),
        in_specs=[pl.BlockSpec((sb, LANE), lambda i: (i, 0), memory_space=pltpu.SMEM),
                  pl.BlockSpec((tm, HALF), lambda i: (i, 0)),
                  pl.BlockSpec(memory_space=pl.ANY)],
        out_specs=pl.BlockSpec(memory_space=pl.ANY),
        out_shape=jax.ShapeDtypeStruct((rows, HALF), U32),
        scratch_shapes=[pltpu.SemaphoreType.DMA(())],
        input_output_aliases={2: 0},
        compiler_params=_cp(("arbitrary",), 32),
        name="dispatch",
    )(dest2d, xp, jnp.zeros((rows, HALF), U32))


def _expert_kernel(be_ref, nu_ref, xg_ref, wg_ref, bg_ref, wu_ref, bu_ref, wd_ref, bd_ref, o_ref, xb_ref, acc_ref,
                   *, nj):
    del be_ref
    i = pl.program_id(0)
    j = pl.program_id(1)

    @pl.when(i < nu_ref[0])
    def _():
        @pl.when(j == 0)
        def _():
            lo, hi = _unpack_pair(xg_ref[...])
            xb_ref[:, :HALF] = lo.astype(BF16)
            xb_ref[:, HALF:] = hi.astype(BF16)
            acc_ref[...] = jnp.broadcast_to(bd_ref[...], acc_ref.shape)

        xb = xb_ref[...]
        gate = jnp.minimum(jnp.dot(xb, wg_ref[...], preferred_element_type=F32) + bg_ref[...], SWIGLU_LIMIT)
        up = jnp.clip(jnp.dot(xb, wu_ref[...], preferred_element_type=F32) + bu_ref[...],
                      -SWIGLU_LIMIT, SWIGLU_LIMIT)
        hid = (up + 1.0) * gate * jax.nn.sigmoid(SWIGLU_ALPHA * gate)
        acc_ref[...] += jnp.dot(hid.astype(BF16), wd_ref[...], preferred_element_type=F32)

        @pl.when(j == nj - 1)
        def _():
            acc = acc_ref[...]
            o_ref[...] = _pack_pair(acc[:, :HALF], acc[:, HALF:])


def _experts(block_expert, n_used, xg, wg, bg, wu, bu, wd, bd, rows_blk, tf):
    rows = xg.shape[0]
    nb = rows // rows_blk
    nj = D_FF // tf

    def blk(i, j, be, nu):
        return (jnp.minimum(i, nu[0] - 1), 0)

    def jj(i, j, nu):
        return jnp.where(i < nu[0], j, nj - 1)

    grid_spec = pltpu.PrefetchScalarGridSpec(
        num_scalar_prefetch=2,
        grid=(nb, nj),
        in_specs=[pl.BlockSpec((rows_blk, HALF), blk),
                  pl.BlockSpec((None, D, tf), lambda i, j, be, nu: (be[i], 0, jj(i, j, nu))),
                  pl.BlockSpec((None, 1, tf), lambda i, j, be, nu: (be[i], 0, jj(i, j, nu))),
                  pl.BlockSpec((None, D, tf), lambda i, j, be, nu: (be[i], 0, jj(i, j, nu))),
                  pl.BlockSpec((None, 1, tf), lambda i, j, be, nu: (be[i], 0, jj(i, j, nu))),
                  pl.BlockSpec((None, tf, D), lambda i, j, be, nu: (be[i], jj(i, j, nu), 0)),
                  pl.BlockSpec((None, 1, D), lambda i, j, be, nu: (be[i], 0, 0))],
        out_specs=pl.BlockSpec((rows_blk, HALF), blk),
        scratch_shapes=[pltpu.VMEM((rows_blk, D), BF16), pltpu.VMEM((rows_blk, D), F32)],
    )
    return pl.pallas_call(
        functools.partial(_expert_kernel, nj=nj),
        grid_spec=grid_spec,
        out_shape=jax.ShapeDtypeStruct((rows, HALF), U32),
        compiler_params=_cp(("arbitrary", "arbitrary"), 48),
        name="experts",
    )(block_expert, n_used, xg, wg, bg, wu, bu, wd, bd)


def _combine_kernel(dest_ref, os_ref, gate_ref, x1_ref, fg_ref, o_ref, buf, sem, *, tm):
    def row_copy(r, k):
        f = r * TOP_K + k
        d = dest_ref[f // LANE, f % LANE]
        return pltpu.make_async_copy(os_ref.at[pl.ds(d, 1), :], buf.at[k, pl.ds(r, 1), :], sem)

    def start(r, carry):
        for k in range(TOP_K):
            row_copy(r, k).start()
        return carry

    def wait(r, carry):
        for k in range(TOP_K):
            row_copy(r, k).wait()
        return carry

    lax.fori_loop(0, tm, start, 0)
    lax.fori_loop(0, tm, wait, 0)

    gates = gate_ref[...]
    y_lo = x1_ref[:, :HALF]
    y_hi = x1_ref[:, HALF:]
    for k in range(TOP_K):
        lo, hi = _unpack_pair(buf[k])
        gk = gates[:, k:k + 1]
        y_lo = y_lo + gk * lo
        y_hi = y_hi + gk * hi
    ms = (jnp.sum(y_lo * y_lo, axis=-1, keepdims=True) + jnp.sum(y_hi * y_hi, axis=-1, keepdims=True)) / D
    inv = lax.rsqrt(ms + EPS)
    o_ref[:, :HALF] = y_lo * inv * fg_ref[:, :HALF]
    o_ref[:, HALF:] = y_hi * inv * fg_ref[:, HALF:]


def _combine(dest2d, out_sorted, gates, x1, fg, tm):
    t = x1.shape[0]
    sb = tm * TOP_K // LANE
    return pl.pallas_call(
        functools.partial(_combine_kernel, tm=tm),
        grid=(t // tm,),
        in_specs=[pl.BlockSpec((sb, LANE), lambda i: (i, 0), memory_space=pltpu.SMEM),
                  pl.BlockSpec(memory_space=pl.ANY),
                  pl.BlockSpec((tm, LANE), lambda i: (i, 0)),
                  pl.BlockSpec((tm, D), lambda i: (i, 0)),
                  pl.BlockSpec((1, D), lambda i: (0, 0))],
        out_specs=pl.BlockSpec((tm, D), lambda i: (i, 0)),
        out_shape=jax.ShapeDtypeStruct((t, D), F32),
        scratch_shapes=[pltpu.VMEM((TOP_K, tm, HALF), U32), pltpu.SemaphoreType.DMA(())],
        compiler_params=_cp(("arbitrary",), 32),
        name="combine",
    )(dest2d, out_sorted, gates, x1, fg)


def _rope_tables(n_prompt, prompt_len, sample_len):
    half = DK // 2
    inv = ROPE_BASE ** (-jnp.arange(half, dtype=F32) / half)
    pos = jnp.concatenate([jnp.tile(jnp.arange(prompt_len, dtype=F32), n_prompt),
                           jnp.arange(sample_len, dtype=F32)])
    ang = pos[:, None] * inv[None, :]
    cos, sin = jnp.cos(ang), jnp.sin(ang)
    return jnp.concatenate([cos, cos], axis=-1), jnp.concatenate([-sin, sin], axis=-1)


def _layer(x, n_prompt, prompt_len, sample_len, mix_norm_g, w_in, ret_norm_g, w_ret_o, conv_w, conv_b, lru_w_a,
           lru_b_a, lru_w_x, lru_b_x, lru_lambda, w_lru_o, w_out, moe_norm_g, w_router, b_router, w_e_gate,
           b_e_gate, w_e_up, b_e_up, w_e_down, b_e_down, final_norm_g):
    t = x.shape[0]
    seqs = _Seqs(n_prompt, prompt_len, sample_len, SEQ_CHUNK)
    row = lambda v: v.reshape(1, -1)

    z = _inproj(x, row(mix_norm_g), w_in.astype(BF16))

    cosf, sinf = _rope_tables(n_prompt, prompt_len, sample_len)
    y_ret = _retention(z, cosf, sinf, row(ret_norm_g), seqs, SEQ_CHUNK)

    wa, wx = lru_w_a.astype(BF16), lru_w_x.astype(BF16)
    ba, bx = lru_b_a[:, None, :], lru_b_x[:, None, :]
    c8 = (-LRU_C * jax.nn.softplus(-lru_lambda))[:, None, :]
    lru_args = (z, conv_w, row(conv_b), wa, ba, wx, bx, c8)
    h_fwd = _lru_call(*lru_args, None, 0, seqs, SEQ_CHUNK, LRU_CB)
    y_lru = _lru_call(*lru_args, h_fwd, 1, seqs, SEQ_CHUNK, LRU_CB)

    merged = _proj_merge(y_ret, y_lru, w_ret_o.astype(BF16), w_lru_o.astype(BF16), z)
    x1 = _proj_out(merged, w_out.astype(BF16), x)

    w_pad = jnp.zeros((D, LANE), F32).at[:, :N_EXP].set(w_router)
    b_pad = jnp.full((1, LANE), -1e30, F32).at[0, :N_EXP].set(b_router)
    xp, idx, gates, rank, cnt = _router(x1, row(moe_norm_g), w_pad, b_pad, TOK_TILE)

    counts = cnt[0, :N_EXP].astype(I32)
    padded = (counts + MOE_ROWS - 1) // MOE_ROWS * MOE_ROWS
    pad_ends = jnp.cumsum(padded)
    pad_starts = pad_ends - padded
    n_blocks = (t * TOP_K) // MOE_ROWS + N_EXP
    rows = n_blocks * MOE_ROWS
    top_idx = idx[:, :TOP_K]
    start_of = jnp.sum(jnp.where(top_idx[:, :, None] == jnp.arange(N_EXP, dtype=I32), pad_starts, 0), axis=-1)
    dest2d = (start_of + rank[:, :TOP_K]).reshape(t * TOP_K // LANE, LANE)
    n_used = (pad_ends[-1] // MOE_ROWS).astype(I32)
    blk_ids = jnp.minimum(jnp.arange(n_blocks, dtype=I32), n_used - 1)
    block_expert = jnp.minimum(
        jnp.sum((pad_ends[None, :] <= (blk_ids * MOE_ROWS)[:, None]).astype(I32), axis=-1), N_EXP - 1)

    xg = _dispatch(dest2d, xp, rows, TOK_TILE)
    out_sorted = _experts(block_expert, n_used.reshape(1), xg,
                          w_e_gate.astype(BF16), b_e_gate[:, None, :], w_e_up.astype(BF16), b_e_up[:, None, :],
                          w_e_down.astype(BF16), b_e_down[:, None, :], MOE_ROWS, MOE_TF)
    return _combine(dest2d, out_sorted, gates, x1, row(final_norm_g), CMB_TILE)


def kernel(x_prompt, x_sample, mix_norm_g, w_in, ret_norm_g, w_ret_o, conv_w, conv_b, lru_w_a, lru_b_a, lru_w_x,
           lru_b_x, lru_lambda, w_lru_o, w_out, moe_norm_g, w_router, b_router, w_e_gate, b_e_gate, w_e_up, b_e_up,
           w_e_down, b_e_down, final_norm_g):
    assert mix_norm_g.shape[0] == 1, "one layer"
    n_prompt, prompt_len, _ = x_prompt.shape
    n_sample, sample_len, _ = x_sample.shape
    assert n_sample == 1
    tp = n_prompt * prompt_len
    x = jnp.concatenate([x_prompt.reshape(tp, D), x_sample.reshape(sample_len, D)], axis=0)
    y = _layer(x, n_prompt, prompt_len, sample_len, mix_norm_g[0], w_in[0], ret_norm_g[0], w_ret_o[0], conv_w[0],
               conv_b[0], lru_w_a[0], lru_b_a[0], lru_w_x[0], lru_b_x[0], lru_lambda[0], w_lru_o[0], w_out[0],
               moe_norm_g[0], w_router[0], b_router[0], w_e_gate[0], b_e_gate[0], w_e_up[0], b_e_up[0],
               w_e_down[0], b_e_down[0], final_norm_g)
    return y[:tp].reshape(x_prompt.shape), y[tp:].reshape(x_sample.shape)
```

```python
import functools
import math

import jax
import jax.numpy as jnp
from jax import lax
from jax.experimental import pallas as pl
from jax.experimental.pallas import tpu as pltpu

F32 = jnp.float32
BF16 = jnp.bfloat16
U32 = jnp.uint32
I32 = jnp.int32

D = 2048
HEADS = 8
DK = 128
DV = 256
QK_W = HEADS * DK
V_W = HEADS * DV
LRU_BLOCK = 128
N_EXP = 32
TOP_K = 4
D_FF = 2048
SWIGLU_LIMIT = 7.0
SWIGLU_ALPHA = 1.702
ROPE_BASE = 10000.0
LRU_C = 8.0
EPS = 1e-6
IN_W = 2 * QK_W + 2 * V_W + 2 * D + 2 * D

HALF = D // 2
LANE = 128
HALO = 16
HI_MASK = 0xFFFF0000

SEQ_CHUNK = 256
LRU_CB = 512
MOE_ROWS = 512
MOE_TF = 512
TOK_TILE = 512
SEG_ALIGN = 8
PERM_CHUNK = 256
SORT_ROWS = TOK_TILE * 4 + 32 * SEG_ALIGN
SEG_MAXBIT = (TOK_TILE // SEG_ALIGN).bit_length() - 1


def _cp(sem, vmem_mb):
    return pltpu.CompilerParams(dimension_semantics=sem, vmem_limit_bytes=vmem_mb << 20)


def _bits(x):
    return lax.bitcast_convert_type(x, U32)


def _pack_pair(lo, hi, rounded=False):
    if not rounded:
        lo = lo.astype(BF16).astype(F32)
        hi = hi.astype(BF16).astype(F32)
    return (_bits(lo) >> 16) | (_bits(hi) & jnp.uint32(HI_MASK))


def _unpack_pair(u):
    lo = lax.bitcast_convert_type(u << 16, F32)
    hi = lax.bitcast_convert_type(u & jnp.uint32(HI_MASK), F32)
    return lo, hi


class _Seqs:
    def __init__(self, n_prompt, prompt_len, sample_len, chunk):
        assert prompt_len % chunk == 0 and sample_len % chunk == 0
        self.cps = prompt_len // chunk
        self.npc = n_prompt * self.cps
        self.nch = self.npc + sample_len // chunk

    def is_first(self, ci):
        return ((ci < self.npc) & (ci % self.cps == 0)) | (ci == self.npc)

    def is_last(self, ci):
        return ((ci < self.npc) & (ci % self.cps == self.cps - 1)) | (ci == self.nch - 1)


def _two_group_specs(tm, npt, width=D):
    return [pl.BlockSpec((tm, width), lambda i, j: (jnp.minimum(i, npt - 1), 0)),
            pl.BlockSpec((tm, width), lambda i, j: (jnp.maximum(i - npt, 0), 0))]


def _inproj_kernel(xp_ref, xs_ref, g_ref, w_ref, o_ref, xn_ref, *, npt):
    i = pl.program_id(0)

    def norm(x_ref):
        x = x_ref[...]
        y = x * lax.rsqrt(jnp.mean(x * x, axis=-1, keepdims=True) + EPS)
        xn_ref[...] = (y * g_ref[...]).astype(BF16)

    @pl.when(pl.program_id(1) == 0)
    def _():
        pl.when(i < npt)(lambda: norm(xp_ref))
        pl.when(i >= npt)(lambda: norm(xs_ref))

    o_ref[...] = jnp.dot(xn_ref[...], w_ref[...], preferred_element_type=F32).astype(o_ref.dtype)


def _inproj(xp, xs, g, w, tm=1024, tn=1024):
    t, n = xp.shape[0] + xs.shape[0], w.shape[1]
    npt = xp.shape[0] // tm
    return pl.pallas_call(
        functools.partial(_inproj_kernel, npt=npt),
        grid=(t // tm, n // tn),
        in_specs=_two_group_specs(tm, npt) + [pl.BlockSpec((1, D), lambda i, j: (0, 0)),
                                              pl.BlockSpec((D, tn), lambda i, j: (0, j))],
        out_specs=pl.BlockSpec((tm, tn), lambda i, j: (i, j)),
        out_shape=jax.ShapeDtypeStruct((t, n), BF16),
        scratch_shapes=[pltpu.VMEM((tm, D), BF16)],
        compiler_params=_cp(("arbitrary", "arbitrary"), 56),
        name="inproj",
    )(xp, xs, g, w)


def _rot(x, cosf, sinf):
    return x * cosf + pltpu.roll(x, DK // 2, 1) * sinf


def _log_gamma(h):
    return math.log1p(-(2.0 ** (-5 - h)))


def _ret_bwd_kernel(q_ref, k_ref, v_ref, cos_ref, sin_ref, o_ref, s_ref, *, seqs, c):
    ci = seqs.nch - 1 - pl.program_id(0)

    @pl.when(seqs.is_last(ci))
    def _():
        s_ref[...] = jnp.zeros_like(s_ref)

    cosf = cos_ref[...]
    sinf = sin_ref[...]
    p = lax.broadcasted_iota(I32, (c, 1), 0).astype(F32)
    for h in range(HEADS):
        lg = _log_gamma(h)
        q = _rot(q_ref[:, h * DK:(h + 1) * DK].astype(F32), cosf, sinf)
        k = _rot(k_ref[:, h * DK:(h + 1) * DK].astype(F32), cosf, sinf) * (DK ** -0.5)
        v = v_ref[:, h * DV:(h + 1) * DV]
        qd = (q * jnp.exp(lg * (c - p))).astype(BF16)
        kd = (k * jnp.exp(lg * p)).astype(BF16)
        s = s_ref[h]
        o_ref[:, h * DV:(h + 1) * DV] = jnp.dot(qd, s.astype(BF16), preferred_element_type=F32)
        s_ref[h] = math.exp(lg * c) * s + lax.dot_general(
            kd, v, (((0,), (0,)), ((), ())), preferred_element_type=F32)


def _ret_fwd_kernel(q_ref, k_ref, v_ref, cos_ref, sin_ref, bwd_ref, gate_ref, gn_ref, o_ref, s_ref, dm_ref,
                    *, seqs, c):
    ci = pl.program_id(0)

    @pl.when(ci == 0)
    def _():
        r = lax.broadcasted_iota(I32, (c, c), 0)
        col = lax.broadcasted_iota(I32, (c, c), 1)
        dist = jnp.abs(r - col).astype(F32)
        for h in range(HEADS):
            dm_ref[h] = jnp.exp(_log_gamma(h) * dist)

    @pl.when(seqs.is_first(ci))
    def _():
        s_ref[...] = jnp.zeros_like(s_ref)

    cosf = cos_ref[...]
    sinf = sin_ref[...]
    p = lax.broadcasted_iota(I32, (c, 1), 0).astype(F32)
    for h in range(HEADS):
        lg = _log_gamma(h)
        q = _rot(q_ref[:, h * DK:(h + 1) * DK].astype(F32), cosf, sinf)
        k = _rot(k_ref[:, h * DK:(h + 1) * DK].astype(F32), cosf, sinf) * (DK ** -0.5)
        v = v_ref[:, h * DV:(h + 1) * DV]
        scores = lax.dot_general(q.astype(BF16), k.astype(BF16), (((1,), (1,)), ((), ())),
                                 preferred_element_type=F32) * dm_ref[h]
        tot = jnp.dot(scores.astype(BF16), v, preferred_element_type=F32)
        qd = (q * jnp.exp(lg * (p + 1.0))).astype(BF16)
        kd = (k * jnp.exp(lg * (c - 1.0 - p))).astype(BF16)
        s = s_ref[h]
        tot = tot + jnp.dot(qd, s.astype(BF16), preferred_element_type=F32)
        s_ref[h] = math.exp(lg * c) * s + lax.dot_general(
            kd, v, (((0,), (0,)), ((), ())), preferred_element_type=F32)
        tot = tot + bwd_ref[:, h * DV:(h + 1) * DV]
        mu = jnp.mean(tot, axis=-1, keepdims=True)
        cen = tot - mu
        var = jnp.mean(cen * cen, axis=-1, keepdims=True)
        yn = cen * lax.rsqrt(var + EPS) * gn_ref[:, h * DV:(h + 1) * DV]
        g = gate_ref[:, h * DV:(h + 1) * DV].astype(F32)
        o_ref[:, h * DV:(h + 1) * DV] = (yn * (g * jax.nn.sigmoid(g))).astype(o_ref.dtype)


def _retention(z, cosf, sinf, gn, seqs, c):
    t = z.shape[0]
    nch = seqs.nch
    rev = lambda i: nch - 1 - i
    qkv_specs = lambda f: [pl.BlockSpec((c, QK_W), lambda i: (f(i), 0)),
                           pl.BlockSpec((c, QK_W), lambda i: (f(i), 1)),
                           pl.BlockSpec((c, V_W), lambda i: (f(i), 1)),
                           pl.BlockSpec((c, DK), lambda i: (f(i), 0)),
                           pl.BlockSpec((c, DK), lambda i: (f(i), 0))]
    bwd = pl.pallas_call(
        functools.partial(_ret_bwd_kernel, seqs=seqs, c=c),
        grid=(nch,),
        in_specs=qkv_specs(rev),
        out_specs=pl.BlockSpec((c, V_W), lambda i: (rev(i), 0)),
        out_shape=jax.ShapeDtypeStruct((t, V_W), F32),
        scratch_shapes=[pltpu.VMEM((HEADS, DK, DV), F32)],
        compiler_params=_cp(("arbitrary",), 32),
        name="ret_bwd",
    )(z, z, z, cosf, sinf)
    fwd_id = lambda i: i
    return pl.pallas_call(
        functools.partial(_ret_fwd_kernel, seqs=seqs, c=c),
        grid=(nch,),
        in_specs=qkv_specs(fwd_id) + [pl.BlockSpec((c, V_W), lambda i: (i, 0)),
                                      pl.BlockSpec((c, V_W), lambda i: (i, 2)),
                                      pl.BlockSpec((1, V_W), lambda i: (0, 0))],
        out_specs=pl.BlockSpec((c, V_W), lambda i: (i, 0)),
        out_shape=jax.ShapeDtypeStruct((t, V_W), BF16),
        scratch_shapes=[pltpu.VMEM((HEADS, DK, DV), F32), pltpu.VMEM((HEADS, c, c), F32)],
        compiler_params=_cp(("arbitrary",), 40),
        name="ret_fwd",
    )(z, z, z, cosf, sinf, bwd, z, gn)


def _lru_kernel(*refs, reverse, seqs, c, cb):
    if reverse:
        (x_ref, xp_ref, xn_ref, cw_ref, cbias_ref, wa_ref, ba_ref, wx_ref, bx_ref, c8_ref, hf_ref, g_ref,
         o_ref, a_s, u_s, h_s, carry_s) = refs
    else:
        (x_ref, xp_ref, xn_ref, cw_ref, cbias_ref, wa_ref, ba_ref, wx_ref, bx_ref, c8_ref,
         o_ref, a_s, u_s, h_s, carry_s) = refs
    t = pl.program_id(1)
    ci = seqs.nch - 1 - t if reverse else t
    first = seqs.is_first(ci)
    last = seqs.is_last(ci)

    x = x_ref[...].astype(F32)
    prev = jnp.where(first, 0.0, xp_ref[...].astype(F32))
    nxt = jnp.where(last, 0.0, xn_ref[...].astype(F32))
    row = lax.broadcasted_iota(I32, (c, 1), 0)
    xm1 = jnp.where(row == 0, prev[HALO - 1:HALO], pltpu.roll(x, 1, 0))
    xm2 = jnp.where(row == 0, prev[HALO - 2:HALO - 1],
                    jnp.where(row == 1, prev[HALO - 1:HALO], pltpu.roll(x, 2, 0)))
    xp1 = jnp.where(row == c - 1, nxt[0:1], pltpu.roll(x, c - 1, 0))
    cw = cw_ref[...]
    xc = cw[0:1] * xm2 + cw[1:2] * xm1 + cw[2:3] * x + cw[3:4] * xp1 + cbias_ref[...]

    for gi in range(cb // LRU_BLOCK):
        sl = slice(gi * LRU_BLOCK, (gi + 1) * LRU_BLOCK)
        xs = xc[:, sl]
        xb = xs.astype(BF16)
        r = jax.nn.sigmoid(jnp.dot(xb, wa_ref[gi], preferred_element_type=F32) + ba_ref[:, sl])
        ig = jax.nn.sigmoid(jnp.dot(xb, wx_ref[gi], preferred_element_type=F32) + bx_ref[:, sl])
        a = jnp.exp(c8_ref[:, sl] * r)
        a_s[:, sl] = a
        u_s[:, sl] = jnp.sqrt(1.0 - a * a) * (ig * xs)

    @pl.when(last if reverse else first)
    def _():
        carry_s[...] = jnp.zeros_like(carry_s)

    ng = c // 8
    row8 = lax.broadcasted_iota(I32, (8, cb), 0)

    def body(j, carry):
        gidx = ng - 1 - j if reverse else j
        r0 = pl.multiple_of(gidx * 8, 8)
        a = a_s[pl.ds(r0, 8), :]
        u = u_s[pl.ds(r0, 8), :]
        for s in (1, 2, 4):
            shift = 8 - s if reverse else s
            m = (row8 < 8 - s) if reverse else (row8 >= s)
            a_sh = pltpu.roll(a, shift, 0)
            u_sh = pltpu.roll(u, shift, 0)
            u = u + a * jnp.where(m, u_sh, 0.0)
            a = a * jnp.where(m, a_sh, 1.0)
        h = u + a * carry
        h_s[pl.ds(r0, 8), :] = h
        edge = h[0:1, :] if reverse else h[7:8, :]
        return jnp.broadcast_to(edge, (8, cb))

    carry_s[...] = lax.fori_loop(0, ng, body, carry_s[...])

    if reverse:
        g = g_ref[...].astype(F32)
        o_ref[...] = ((hf_ref[...] + h_s[...]) * jax.nn.gelu(g, approximate=True)).astype(o_ref.dtype)
    else:
        o_ref[...] = h_s[...]


def _lru_call(z, conv_w, conv_b, wa, ba, wx, bx, c8, hf, direction, seqs, c, cb):
    t = z.shape[0]
    nch = seqs.nch
    reverse = direction == 1
    tmap = (lambda ti: nch - 1 - ti) if reverse else (lambda ti: ti)
    xcol = (2 * QK_W + 2 * V_W) // cb
    gcol = (2 * QK_W + 2 * V_W + D) // cb
    hb = c // HALO
    nhb = t // HALO
    nb = cb // LRU_BLOCK
    in_specs = [
        pl.BlockSpec((c, cb), lambda ch, ti: (tmap(ti), xcol + ch)),
        pl.BlockSpec((HALO, cb), lambda ch, ti: (jnp.maximum(tmap(ti) * hb - 1, 0), xcol + ch)),
        pl.BlockSpec((HALO, cb), lambda ch, ti: (jnp.minimum((tmap(ti) + 1) * hb, nhb - 1), xcol + ch)),
        pl.BlockSpec((4, cb), lambda ch, ti: (0, ch)),
        pl.BlockSpec((1, cb), lambda ch, ti: (0, ch)),
        pl.BlockSpec((None, nb, LRU_BLOCK, LRU_BLOCK), lambda ch, ti: (direction, ch, 0, 0)),
        pl.BlockSpec((None, 1, cb), lambda ch, ti: (direction, 0, ch)),
        pl.BlockSpec((None, nb, LRU_BLOCK, LRU_BLOCK), lambda ch, ti: (direction, ch, 0, 0)),
        pl.BlockSpec((None, 1, cb), lambda ch, ti: (direction, 0, ch)),
        pl.BlockSpec((None, 1, cb), lambda ch, ti: (direction, 0, ch)),
    ]
    args = [z, z, z, conv_w, conv_b, wa, ba, wx, bx, c8]
    if reverse:
        in_specs += [pl.BlockSpec((c, cb), lambda ch, ti: (tmap(ti), ch)),
                     pl.BlockSpec((c, cb), lambda ch, ti: (tmap(ti), gcol + ch))]
        args += [hf, z]
    return pl.pallas_call(
        functools.partial(_lru_kernel, reverse=reverse, seqs=seqs, c=c, cb=cb),
        grid=(D // cb, nch),
        in_specs=in_specs,
        out_specs=pl.BlockSpec((c, cb), lambda ch, ti: (tmap(ti), ch)),
        out_shape=jax.ShapeDtypeStruct((t, D), BF16 if reverse else F32),
        scratch_shapes=[pltpu.VMEM((c, cb), F32), pltpu.VMEM((c, cb), F32), pltpu.VMEM((c, cb), F32),
                        pltpu.VMEM((8, cb), F32)],
        compiler_params=_cp(("arbitrary", "arbitrary"),32),
        name="lru_rev" if reverse else "lru_fwd",
    )(*args)


def _proj_merge_kernel(yr_ref, yl_ref, wr_ref, wl_ref, mr_ref, ml_ref, o_ref):
    r = jnp.dot(yr_ref[...], wr_ref[...], preferred_element_type=F32)
    l = jnp.dot(yl_ref[...], wl_ref[...], preferred_element_type=F32)
    o_ref[...] = (jax.nn.sigmoid(mr_ref[...].astype(F32)) * r
                  + jax.nn.sigmoid(ml_ref[...].astype(F32)) * l).astype(o_ref.dtype)


def _proj_merge(y_ret, y_lru, w_ret_o, w_lru_o, z, tm=512, tn=1024):
    t = y_ret.shape[0]
    mcol = (2 * QK_W + 2 * V_W + 2 * D) // tn
    return pl.pallas_call(
        _proj_merge_kernel,
        grid=(t // tm, D // tn),
        in_specs=[pl.BlockSpec((tm, V_W), lambda i, j: (i, 0)),
                  pl.BlockSpec((tm, D), lambda i, j: (i, 0)),
                  pl.BlockSpec((V_W, tn), lambda i, j: (0, j)),
                  pl.BlockSpec((D, tn), lambda i, j: (0, j)),
                  pl.BlockSpec((tm, tn), lambda i, j: (i, mcol + j)),
                  pl.BlockSpec((tm, tn), lambda i, j: (i, mcol + D // tn + j))],
        out_specs=pl.BlockSpec((tm, tn), lambda i, j: (i, j)),
        out_shape=jax.ShapeDtypeStruct((t, D), BF16),
        compiler_params=_cp(("arbitrary", "arbitrary"),48),
        name="proj_merge",
    )(y_ret, y_lru, w_ret_o, w_lru_o, z, z)


def _proj_out_kernel(m_ref, w_ref, xp_ref, xs_ref, o_ref, *, npt):
    i = pl.program_id(0)
    y = jnp.dot(m_ref[...], w_ref[...], preferred_element_type=F32)

    @pl.when(i < npt)
    def _():
        o_ref[...] = xp_ref[...] + y

    @pl.when(i >= npt)
    def _():
        o_ref[...] = xs_ref[...] + y


def _proj_out(merged, w_out, xp, xs, tm=512, tn=1024):
    t = merged.shape[0]
    npt = xp.shape[0] // tm
    nj = D // tn
    return pl.pallas_call(
        functools.partial(_proj_out_kernel, npt=npt),
        grid=(t // tm, D // tn),
        in_specs=[pl.BlockSpec((tm, D), lambda i, j: (i, 0)),
                  pl.BlockSpec((D, tn), lambda i, j: (0, j)),
                  pl.BlockSpec((tm, tn), lambda i, j: (jnp.minimum(i, npt - 1), jnp.where(i < npt, j, nj - 1))),
                  pl.BlockSpec((tm, tn), lambda i, j: (jnp.maximum(i - npt, 0), jnp.where(i >= npt, j, 0)))],
        out_specs=pl.BlockSpec((tm, tn), lambda i, j: (i, j)),
        out_shape=jax.ShapeDtypeStruct((t, D), F32),
        compiler_params=_cp(("arbitrary", "arbitrary"), 48),
        name="proj_out",
    )(merged, w_out, xp, xs)


def _router_kernel(x_ref, g_ref, w_ref, b_ref, xn_ref, tpos_ref, tpos_t_ref, gate_ref, meta_ref, tri_ref, run_ref,
                   *, tm):
    i = pl.program_id(0)

    @pl.when(i == 0)
    def _():
        r = lax.broadcasted_iota(I32, (tm, tm), 0)
        col = lax.broadcasted_iota(I32, (tm, tm), 1)
        tri_ref[...] = (col < r).astype(BF16)
        run_ref[...] = jnp.zeros_like(run_ref)

    x = x_ref[...]
    xn = x * lax.rsqrt(jnp.mean(x * x, axis=-1, keepdims=True) + EPS) * g_ref[...]
    xn_ref[...] = xn.astype(BF16)

    logits = jnp.dot(xn, w_ref[...], preferred_element_type=F32, precision=lax.Precision.HIGHEST) + b_ref[...]
    lane = lax.broadcasted_iota(I32, (tm, LANE), 1)
    lane_f = lane.astype(F32)
    vals, idxs = [], []
    cur = logits
    for _ in range(TOP_K):
        m = jnp.max(cur, axis=-1, keepdims=True)
        idx = jnp.min(jnp.where(cur == m, lane_f, float(LANE)), axis=-1, keepdims=True).astype(I32)
        vals.append(m)
        idxs.append(idx)
        cur = jnp.where(lane == idx, -jnp.inf, cur)
    exps = [jnp.exp(v - vals[0]) for v in vals]
    denom = exps[0] + exps[1] + exps[2] + exps[3]

    onehot = jnp.zeros((tm, LANE), F32)
    for idx in idxs:
        onehot = onehot + (lane == idx).astype(F32)
    before = jnp.dot(tri_ref[...], onehot.astype(BF16), preferred_element_type=F32)

    cnt = jnp.broadcast_to(jnp.sum(onehot, axis=0, keepdims=True), (8, LANE))
    cnt8 = jnp.floor((cnt + (SEG_ALIGN - 1)) * (1.0 / SEG_ALIGN)) * SEG_ALIGN
    lane8 = lax.broadcasted_iota(I32, (8, LANE), 1)
    incl = cnt8
    s = 1
    while s < LANE:
        incl = incl + jnp.where(lane8 >= s, pltpu.roll(incl, s, 1), 0.0)
        s *= 2
    toff = incl - cnt8

    pos = before + toff[0:1]
    tpos_out = jnp.zeros((tm, LANE), F32)
    gate_out = jnp.zeros((tm, LANE), F32)
    for k in range(TOP_K):
        tpos_k = jnp.sum(jnp.where(lane == idxs[k], pos, 0.0), axis=-1, keepdims=True)
        tpos_out = jnp.where(lane == k, tpos_k, tpos_out)
        gate_out = jnp.where(lane == k, exps[k] / denom, gate_out)
    tpos_ref[...] = tpos_out.astype(I32)
    tpos_t_ref[...] = tpos_out.T[0:8].astype(I32)
    gate_ref[...] = gate_out

    sub8 = lax.broadcasted_iota(I32, (8, LANE), 0)
    run = jnp.broadcast_to(run_ref[...], (8, LANE))
    meta_ref[...] = jnp.where(sub8 == 0, cnt8, jnp.where(sub8 == 1, toff, jnp.where(sub8 == 2, run, 0.0)))
    run_ref[...] = run_ref[...] + cnt8[0:1]


def _router(x1, g, w_pad, b_pad, tm):
    t = x1.shape[0]
    nt = t // tm
    tok = lambda i: (i, 0)
    fixed = lambda i: (0, 0)
    per_tile = lambda i: (i, 0, 0)
    return pl.pallas_call(
        functools.partial(_router_kernel, tm=tm),
        grid=(nt,),
        in_specs=[pl.BlockSpec((tm, D), tok), pl.BlockSpec((1, D), fixed),
                  pl.BlockSpec((D, LANE), fixed), pl.BlockSpec((1, LANE), fixed)],
        out_specs=[pl.BlockSpec((tm, D), tok), pl.BlockSpec((tm, LANE), tok),
                   pl.BlockSpec((None, 8, tm), per_tile), pl.BlockSpec((tm, LANE), tok),
                   pl.BlockSpec((None, 8, LANE), per_tile)],
        out_shape=[jax.ShapeDtypeStruct((t, D), BF16), jax.ShapeDtypeStruct((t, LANE), I32),
                   jax.ShapeDtypeStruct((nt, 8, tm), I32), jax.ShapeDtypeStruct((t, LANE), F32),
                   jax.ShapeDtypeStruct((nt, 8, LANE), F32)],
        scratch_shapes=[pltpu.VMEM((tm, tm), BF16), pltpu.VMEM((1, LANE), F32)],
        compiler_params=_cp(("arbitrary",), 32),
        name="router",
    )(x1, g, w_pad, b_pad)


def _segment_copies(i, toff_ref, c8_ref, make_copy, act):
    def per_expert(e, carry):
        base = i * N_EXP + e
        n = c8_ref[base] // SEG_ALIGN
        t0 = toff_ref[base]
        off = 0
        for b in range(SEG_MAXBIT, -1, -1):
            rows = SEG_ALIGN << b
            bit = (n >> b) & 1

            @pl.when(bit == 1)
            def _(off=off, rows=rows):
                act(make_copy(pl.multiple_of(t0 + off, SEG_ALIGN), off, e, rows))

            off = off + bit * rows
        return carry

    lax.fori_loop(0, N_EXP, per_expert, 0)


def _dispatch_kernel(toff_ref, c8_ref, dst_ref, zblk_ref, tpos_t_ref, x_ref, xg_ref, sorted_ref, zero_ref, sem):
    i = pl.program_id(0)

    def zero_copy(e):
        start = pl.multiple_of(jnp.maximum(zblk_ref[e], 0), MOE_ROWS)
        return pltpu.make_async_copy(zero_ref, xg_ref.at[pl.ds(start, MOE_ROWS), :], sem)

    def for_nonempty(act):
        def body(e, carry):
            pl.when(zblk_ref[e] >= 0)(lambda: act(zero_copy(e)))
            return carry
        lax.fori_loop(0, N_EXP, body, 0)

    @pl.when(i == 0)
    def _():
        zero_ref[...] = jnp.zeros_like(zero_ref)
        for_nonempty(lambda cp: cp.start())
        for_nonempty(lambda cp: cp.wait())

    x = x_ref[...]
    for c in range(SORT_ROWS // PERM_CHUNK):
        p = c * PERM_CHUNK + lax.broadcasted_iota(I32, (PERM_CHUNK, 1), 0)
        perm = jnp.zeros((PERM_CHUNK, x.shape[0]), F32)
        for k in range(TOP_K):
            perm = perm + jnp.where(p == tpos_t_ref[k:k + 1, :], 1.0, 0.0)
        rows = jnp.dot(perm.astype(BF16), x, preferred_element_type=F32)
        sorted_ref[c * PERM_CHUNK:(c + 1) * PERM_CHUNK, :] = _pack_pair(rows[:, :HALF], rows[:, HALF:], rounded=True)

    def make_copy(t_off, s_off, e, rows):
        d = pl.multiple_of(dst_ref[i * N_EXP + e] + s_off, SEG_ALIGN)
        return pltpu.make_async_copy(sorted_ref.at[pl.ds(t_off, rows), :], xg_ref.at[pl.ds(d, rows), :], sem)

    _segment_copies(i, toff_ref, c8_ref, make_copy, lambda cp: cp.start())
    _segment_copies(i, toff_ref, c8_ref, make_copy, lambda cp: cp.wait())


def _dispatch(toff, c8, dst, zblk, tpos_t, xn, rows):
    nt, _, tm = tpos_t.shape
    grid_spec = pltpu.PrefetchScalarGridSpec(
        num_scalar_prefetch=4,
        grid=(nt,),
        in_specs=[pl.BlockSpec((None, 8, tm), lambda i, *_: (i, 0, 0)),
                  pl.BlockSpec((tm, D), lambda i, *_: (i, 0))],
        out_specs=pl.BlockSpec(memory_space=pl.ANY),
        scratch_shapes=[pltpu.VMEM((SORT_ROWS, HALF), U32), pltpu.VMEM((MOE_ROWS, HALF), U32),
                        pltpu.SemaphoreType.DMA(())],
    )
    return pl.pallas_call(
        _dispatch_kernel,
        grid_spec=grid_spec,
        out_shape=jax.ShapeDtypeStruct((rows, HALF), U32),
        compiler_params=_cp(("arbitrary",), 40),
        name="dispatch",
    )(toff, c8, dst, zblk, tpos_t, xn)


def _expert_kernel(be_ref, nu_ref, xg_ref, wg_ref, bg_ref, wu_ref, bu_ref, wd_ref, bd_ref, o_ref, xb_ref, acc_ref,
                   *, nj):
    del be_ref
    i = pl.program_id(0)
    j = pl.program_id(1)

    @pl.when(i < nu_ref[0])
    def _():
        @pl.when(j == 0)
        def _():
            lo, hi = _unpack_pair(xg_ref[...])
            xb_ref[:, :HALF] = lo.astype(BF16)
            xb_ref[:, HALF:] = hi.astype(BF16)
            acc_ref[...] = jnp.broadcast_to(bd_ref[...], acc_ref.shape)

        xb = xb_ref[...]
        gate = jnp.minimum(jnp.dot(xb, wg_ref[...], preferred_element_type=F32) + bg_ref[...], SWIGLU_LIMIT)
        up = jnp.clip(jnp.dot(xb, wu_ref[...], preferred_element_type=F32) + bu_ref[...],
                      -SWIGLU_LIMIT, SWIGLU_LIMIT)
        hid = (up + 1.0) * gate * jax.nn.sigmoid(SWIGLU_ALPHA * gate)
        acc_ref[...] += jnp.dot(hid.astype(BF16), wd_ref[...], preferred_element_type=F32)

        @pl.when(j == nj - 1)
        def _():
            acc = acc_ref[...]
            o_ref[...] = _pack_pair(acc[:, :HALF], acc[:, HALF:])


def _experts(block_expert, n_used, xg, wg, bg, wu, bu, wd, bd, rows_blk, tf):
    rows = xg.shape[0]
    nb = rows // rows_blk
    nj = D_FF // tf

    def blk(i, j, be, nu):
        return (jnp.minimum(i, nu[0] - 1), 0)

    def jj(i, j, nu):
        return jnp.where(i < nu[0], j, nj - 1)

    grid_spec = pltpu.PrefetchScalarGridSpec(
        num_scalar_prefetch=2,
        grid=(nb, nj),
        in_specs=[pl.BlockSpec((rows_blk, HALF), blk),
                  pl.BlockSpec((None, D, tf), lambda i, j, be, nu: (be[i], 0, jj(i, j, nu))),
                  pl.BlockSpec((None, 1, tf), lambda i, j, be, nu: (be[i], 0, jj(i, j, nu))),
                  pl.BlockSpec((None, D, tf), lambda i, j, be, nu: (be[i], 0, jj(i, j, nu))),
                  pl.BlockSpec((None, 1, tf), lambda i, j, be, nu: (be[i], 0, jj(i, j, nu))),
                  pl.BlockSpec((None, tf, D), lambda i, j, be, nu: (be[i], jj(i, j, nu), 0)),
                  pl.BlockSpec((None, 1, D), lambda i, j, be, nu: (be[i], 0, 0))],
        out_specs=pl.BlockSpec((rows_blk, HALF), blk),
        scratch_shapes=[pltpu.VMEM((rows_blk, D), BF16), pltpu.VMEM((rows_blk, D), F32)],
    )
    return pl.pallas_call(
        functools.partial(_expert_kernel, nj=nj),
        grid_spec=grid_spec,
        out_shape=jax.ShapeDtypeStruct((rows, HALF), U32),
        compiler_params=_cp(("arbitrary", "arbitrary"), 48),
        name="experts",
    )(block_expert, n_used, xg, wg, bg, wu, bu, wd, bd)


def _combine_kernel(toff_ref, c8_ref, dst_ref, os_ref, tpos_ref, gate_ref, x1_ref, fg_ref, op_ref, osm_ref,
                    sorted_ref, sem, *, npt):
    i = pl.program_id(0)

    def make_copy(t_off, s_off, e, rows):
        d = pl.multiple_of(dst_ref[i * N_EXP + e] + s_off, SEG_ALIGN)
        return pltpu.make_async_copy(os_ref.at[pl.ds(d, rows), :], sorted_ref.at[pl.ds(t_off, rows), :], sem)

    _segment_copies(i, toff_ref, c8_ref, make_copy, lambda cp: cp.start())
    _segment_copies(i, toff_ref, c8_ref, make_copy, lambda cp: cp.wait())

    last = i * N_EXP + N_EXP - 1
    used = toff_ref[last] + c8_ref[last]
    tpos = tpos_ref[...]
    gates = gate_ref[...]
    y_lo = x1_ref[:, :HALF]
    y_hi = x1_ref[:, HALF:]
    for c in range(SORT_ROWS // PERM_CHUNK):
        p_lane = c * PERM_CHUNK + lax.broadcasted_iota(I32, (1, PERM_CHUNK), 1)
        place = jnp.zeros((tpos.shape[0], PERM_CHUNK), F32)
        for k in range(TOP_K):
            place = place + jnp.where(tpos[:, k:k + 1] == p_lane, gates[:, k:k + 1], 0.0)
        p_row = c * PERM_CHUNK + lax.broadcasted_iota(I32, (PERM_CHUNK, 1), 0)
        u = jnp.where(p_row < used, sorted_ref[c * PERM_CHUNK:(c + 1) * PERM_CHUNK, :], jnp.uint32(0))
        lo, hi = _unpack_pair(u)
        place = place.astype(BF16)
        y_lo = y_lo + jnp.dot(place, lo.astype(BF16), preferred_element_type=F32)
        y_hi = y_hi + jnp.dot(place, hi.astype(BF16), preferred_element_type=F32)
    ms = (jnp.sum(y_lo * y_lo, axis=-1, keepdims=True) + jnp.sum(y_hi * y_hi, axis=-1, keepdims=True)) / D
    inv = lax.rsqrt(ms + EPS)
    out = jnp.concatenate([y_lo * inv * fg_ref[:, :HALF], y_hi * inv * fg_ref[:, HALF:]], axis=-1)

    @pl.when(i < npt)
    def _():
        op_ref[...] = out

    @pl.when(i >= npt)
    def _():
        osm_ref[...] = out


def _combine(toff, c8, dst, out_sorted, tpos, gates, x1, fg, tp, tm):
    t = x1.shape[0]
    npt = tp // tm
    tok = lambda i, *_: (i, 0)
    grid_spec = pltpu.PrefetchScalarGridSpec(
        num_scalar_prefetch=3,
        grid=(t // tm,),
        in_specs=[pl.BlockSpec(memory_space=pl.ANY),
                  pl.BlockSpec((tm, LANE), tok), pl.BlockSpec((tm, LANE), tok), pl.BlockSpec((tm, D), tok),
                  pl.BlockSpec((1, D), lambda i, *_: (0, 0))],
        out_specs=[pl.BlockSpec((tm, D), lambda i, *_: (jnp.minimum(i, npt - 1), 0)),
                   pl.BlockSpec((tm, D), lambda i, *_: (jnp.maximum(i - npt, 0), 0))],
        scratch_shapes=[pltpu.VMEM((SORT_ROWS, HALF), U32), pltpu.SemaphoreType.DMA(())],
    )
    return pl.pallas_call(
        functools.partial(_combine_kernel, npt=npt),
        grid_spec=grid_spec,
        out_shape=[jax.ShapeDtypeStruct((tp, D), F32), jax.ShapeDtypeStruct((t - tp, D), F32)],
        compiler_params=_cp(("arbitrary",), 56),
        name="combine",
    )(toff, c8, dst, out_sorted, tpos, gates, x1, fg)


def _rope_tables(n_prompt, prompt_len, sample_len):
    half = DK // 2
    inv = ROPE_BASE ** (-jnp.arange(half, dtype=F32) / half)
    pos = jnp.concatenate([jnp.tile(jnp.arange(prompt_len, dtype=F32), n_prompt),
                           jnp.arange(sample_len, dtype=F32)])
    ang = pos[:, None] * inv[None, :]
    cos, sin = jnp.cos(ang), jnp.sin(ang)
    return jnp.concatenate([cos, cos], axis=-1), jnp.concatenate([-sin, sin], axis=-1)


def _layer(xp, xs, n_prompt, prompt_len, sample_len, mix_norm_g, w_in, ret_norm_g, w_ret_o, conv_w, conv_b, lru_w_a,
           lru_b_a, lru_w_x, lru_b_x, lru_lambda, w_lru_o, w_out, moe_norm_g, w_router, b_router, w_e_gate,
           b_e_gate, w_e_up, b_e_up, w_e_down, b_e_down, final_norm_g):
    tp = xp.shape[0]
    t = tp + xs.shape[0]
    seqs = _Seqs(n_prompt, prompt_len, sample_len, SEQ_CHUNK)
    row = lambda v: v.reshape(1, -1)

    z = _inproj(xp, xs, row(mix_norm_g), w_in.astype(BF16))

    cosf, sinf = _rope_tables(n_prompt, prompt_len, sample_len)
    y_ret = _retention(z, cosf, sinf, row(ret_norm_g), seqs, SEQ_CHUNK)

    wa, wx = lru_w_a.astype(BF16), lru_w_x.astype(BF16)
    ba, bx = lru_b_a[:, None, :], lru_b_x[:, None, :]
    c8 = (-LRU_C * jax.nn.softplus(-lru_lambda))[:, None, :]
    lru_args = (z, conv_w, row(conv_b), wa, ba, wx, bx, c8)
    h_fwd = _lru_call(*lru_args, None, 0, seqs, SEQ_CHUNK, LRU_CB)
    y_lru = _lru_call(*lru_args, h_fwd, 1, seqs, SEQ_CHUNK, LRU_CB)

    merged = _proj_merge(y_ret, y_lru, w_ret_o.astype(BF16), w_lru_o.astype(BF16), z)
    x1 = _proj_out(merged, w_out.astype(BF16), xp, xs)

    w_pad = jnp.zeros((D, LANE), F32).at[:, :N_EXP].set(w_router)
    b_pad = jnp.full((1, LANE), -1e30, F32).at[0, :N_EXP].set(b_router)
    xn2, tpos, tpos_t, gates, meta = _router(x1, row(moe_norm_g), w_pad, b_pad, TOK_TILE)

    nt = t // TOK_TILE
    c8 = meta[:, 0, :N_EXP].astype(I32)
    toff = meta[:, 1, :N_EXP].astype(I32)
    run = meta[:, 2, :N_EXP].astype(I32)
    total = run[-1] + c8[-1]
    padded = (total + MOE_ROWS - 1) // MOE_ROWS * MOE_ROWS
    pad_ends = jnp.cumsum(padded)
    pad_starts = pad_ends - padded
    dst = (pad_starts[None, :] + run).reshape(-1)
    zblk = jnp.where(padded > 0, pad_ends - MOE_ROWS, -1)
    n_blocks = (t * TOP_K + (SEG_ALIGN - 1) * nt * N_EXP) // MOE_ROWS + 1 + N_EXP
    n_used = (pad_ends[-1] // MOE_ROWS).astype(I32)
    blk_ids = jnp.minimum(jnp.arange(n_blocks, dtype=I32), n_used - 1)
    block_expert = jnp.minimum(
        jnp.sum((pad_ends[None, :] <= (blk_ids * MOE_ROWS)[:, None]).astype(I32), axis=-1), N_EXP - 1)
    toff, c8 = toff.reshape(-1), c8.reshape(-1)

    xg = _dispatch(toff, c8, dst, zblk, tpos_t, xn2, n_blocks * MOE_ROWS)
    out_sorted = _experts(block_expert, n_used.reshape(1), xg,
                          w_e_gate.astype(BF16), b_e_gate[:, None, :], w_e_up.astype(BF16), b_e_up[:, None, :],
                          w_e_down.astype(BF16), b_e_down[:, None, :], MOE_ROWS, MOE_TF)
    return _combine(toff, c8, dst, out_sorted, tpos, gates, x1, row(final_norm_g), tp, TOK_TILE)


def kernel(x_prompt, x_sample, mix_norm_g, w_in, ret_norm_g, w_ret_o, conv_w, conv_b, lru_w_a, lru_b_a, lru_w_x,
           lru_b_x, lru_lambda, w_lru_o, w_out, moe_norm_g, w_router, b_router, w_e_gate, b_e_gate, w_e_up, b_e_up,
           w_e_down, b_e_down, final_norm_g):
    assert mix_norm_g.shape[0] == 1, "one layer"
    n_prompt, prompt_len, _ = x_prompt.shape
    n_sample, sample_len, _ = x_sample.shape
    assert n_sample == 1
    tp = n_prompt * prompt_len
    yp, ys = _layer(x_prompt.reshape(tp, D), x_sample.reshape(sample_len, D), n_prompt, prompt_len, sample_len,
                    mix_norm_g[0], w_in[0], ret_norm_g[0], w_ret_o[0], conv_w[0],
                    conv_b[0], lru_w_a[0], lru_b_a[0], lru_w_x[0], lru_b_x[0], lru_lambda[0], w_lru_o[0], w_out[0],
                    moe_norm_g[0], w_router[0], b_router[0], w_e_gate[0], b_e_gate[0], w_e_up[0], b_e_up[0],
                    w_e_down[0], b_e_down[0], final_norm_g)
    return yp.reshape(x_prompt.shape), ys.reshape(x_sample.shape)
```

```python
import functools
import math

import jax
import jax.numpy as jnp
from jax import lax
from jax.experimental import pallas as pl
from jax.experimental.pallas import tpu as pltpu

F32 = jnp.float32
BF16 = jnp.bfloat16
U32 = jnp.uint32
I32 = jnp.int32

D = 2048
HEADS = 8
DK = 128
DV = 256
QK_W = HEADS * DK
V_W = HEADS * DV
LRU_BLOCK = 128
N_EXP = 32
TOP_K = 4
D_FF = 2048
SWIGLU_LIMIT = 7.0
SWIGLU_ALPHA = 1.702
ROPE_BASE = 10000.0
LRU_C = 8.0
EPS = 1e-6
IN_W = 2 * QK_W + 2 * V_W + 2 * D + 2 * D

HALF = D // 2
LANE = 128
HALO = 16
HI_MASK = 0xFFFF0000

SEQ_CHUNK = 256
LRU_CB = 512
LRU_SEG = 4
LRU_SUPER = 8 * LRU_SEG
MOE_ROWS = 512
MOE_TF = 1024
TOK_TILE = 512
SEG_ALIGN = 8
PERM_CHUNK = 256
SORT_ROWS = TOK_TILE * 4 + 32 * SEG_ALIGN
SEG_MAXBIT = (TOK_TILE // SEG_ALIGN).bit_length() - 1


def _cp(sem, vmem_mb):
    return pltpu.CompilerParams(dimension_semantics=sem, vmem_limit_bytes=vmem_mb << 20)


def _bits(x):
    return lax.bitcast_convert_type(x, U32)


def _pack_pair(lo, hi, rounded=False):
    if not rounded:
        lo = lo.astype(BF16).astype(F32)
        hi = hi.astype(BF16).astype(F32)
    return (_bits(lo) >> 16) | (_bits(hi) & jnp.uint32(HI_MASK))


def _unpack_pair(u):
    lo = lax.bitcast_convert_type(u << 16, F32)
    hi = lax.bitcast_convert_type(u & jnp.uint32(HI_MASK), F32)
    return lo, hi


class _Seqs:
    def __init__(self, n_prompt, prompt_len, sample_len, chunk):
        assert prompt_len % chunk == 0 and sample_len % chunk == 0
        self.cps = prompt_len // chunk
        self.npc = n_prompt * self.cps
        self.nch = self.npc + sample_len // chunk

    def is_first(self, ci):
        return ((ci < self.npc) & (ci % self.cps == 0)) | (ci == self.npc)

    def is_last(self, ci):
        return ((ci < self.npc) & (ci % self.cps == self.cps - 1)) | (ci == self.nch - 1)


def _two_group_specs(tm, npt, width=D):
    return [pl.BlockSpec((tm, width), lambda i, j: (jnp.minimum(i, npt - 1), 0)),
            pl.BlockSpec((tm, width), lambda i, j: (jnp.maximum(i - npt, 0), 0))]


def _inproj_kernel(xp_ref, xs_ref, g_ref, w_ref, o_ref, xn_ref, *, npt):
    i = pl.program_id(0)

    def norm(x_ref):
        x = x_ref[...]
        y = x * lax.rsqrt(jnp.mean(x * x, axis=-1, keepdims=True) + EPS)
        xn_ref[...] = (y * g_ref[...]).astype(BF16)

    @pl.when(pl.program_id(1) == 0)
    def _():
        pl.when(i < npt)(lambda: norm(xp_ref))
        pl.when(i >= npt)(lambda: norm(xs_ref))

    o_ref[...] = jnp.dot(xn_ref[...], w_ref[...], preferred_element_type=F32).astype(o_ref.dtype)


def _inproj(xp, xs, g, w, tm=1024, tn=1024):
    t, n = xp.shape[0] + xs.shape[0], w.shape[1]
    npt = xp.shape[0] // tm
    return pl.pallas_call(
        functools.partial(_inproj_kernel, npt=npt),
        grid=(t // tm, n // tn),
        in_specs=_two_group_specs(tm, npt) + [pl.BlockSpec((1, D), lambda i, j: (0, 0)),
                                              pl.BlockSpec((D, tn), lambda i, j: (0, j))],
        out_specs=pl.BlockSpec((tm, tn), lambda i, j: (i, j)),
        out_shape=jax.ShapeDtypeStruct((t, n), BF16),
        scratch_shapes=[pltpu.VMEM((tm, D), BF16)],
        compiler_params=_cp(("arbitrary", "arbitrary"), 56),
        name="inproj",
    )(xp, xs, g, w)


def _rot(x, cosf, sinf):
    return x * cosf + pltpu.roll(x, DK // 2, 1) * sinf


def _log_gamma(h):
    return math.log1p(-(2.0 ** (-5 - h)))


def _ret_bwd_kernel(q_ref, k_ref, v_ref, cos_ref, sin_ref, o_ref, s_ref, *, seqs, c):
    ci = seqs.nch - 1 - pl.program_id(0)

    @pl.when(seqs.is_last(ci))
    def _():
        s_ref[...] = jnp.zeros_like(s_ref)

    cosf = cos_ref[...]
    sinf = sin_ref[...]
    p = lax.broadcasted_iota(I32, (c, 1), 0).astype(F32)
    for h in range(HEADS):
        lg = _log_gamma(h)
        q = _rot(q_ref[:, h * DK:(h + 1) * DK].astype(F32), cosf, sinf)
        k = _rot(k_ref[:, h * DK:(h + 1) * DK].astype(F32), cosf, sinf) * (DK ** -0.5)
        v = v_ref[:, h * DV:(h + 1) * DV]
        qd = (q * jnp.exp(lg * (c - p))).astype(BF16)
        kd = (k * jnp.exp(lg * p)).astype(BF16)
        s = s_ref[h]
        o_ref[:, h * DV:(h + 1) * DV] = jnp.dot(qd, s.astype(BF16), preferred_element_type=F32)
        s_ref[h] = math.exp(lg * c) * s + lax.dot_general(
            kd, v, (((0,), (0,)), ((), ())), preferred_element_type=F32)


def _ret_fwd_kernel(q_ref, k_ref, v_ref, cos_ref, sin_ref, bwd_ref, gate_ref, gn_ref, o_ref, s_ref, dm_ref,
                    *, seqs, c):
    ci = pl.program_id(0)

    @pl.when(ci == 0)
    def _():
        r = lax.broadcasted_iota(I32, (c, c), 0)
        col = lax.broadcasted_iota(I32, (c, c), 1)
        dist = jnp.abs(r - col).astype(F32)
        for h in range(HEADS):
            dm_ref[h] = jnp.exp(_log_gamma(h) * dist)

    @pl.when(seqs.is_first(ci))
    def _():
        s_ref[...] = jnp.zeros_like(s_ref)

    cosf = cos_ref[...]
    sinf = sin_ref[...]
    p = lax.broadcasted_iota(I32, (c, 1), 0).astype(F32)
    for h in range(HEADS):
        lg = _log_gamma(h)
        q = _rot(q_ref[:, h * DK:(h + 1) * DK].astype(F32), cosf, sinf)
        k = _rot(k_ref[:, h * DK:(h + 1) * DK].astype(F32), cosf, sinf) * (DK ** -0.5)
        v = v_ref[:, h * DV:(h + 1) * DV]
        scores = lax.dot_general(q.astype(BF16), k.astype(BF16), (((1,), (1,)), ((), ())),
                                 preferred_element_type=F32) * dm_ref[h]
        tot = jnp.dot(scores.astype(BF16), v, preferred_element_type=F32)
        qd = (q * jnp.exp(lg * (p + 1.0))).astype(BF16)
        kd = (k * jnp.exp(lg * (c - 1.0 - p))).astype(BF16)
        s = s_ref[h]
        tot = tot + jnp.dot(qd, s.astype(BF16), preferred_element_type=F32)
        s_ref[h] = math.exp(lg * c) * s + lax.dot_general(
            kd, v, (((0,), (0,)), ((), ())), preferred_element_type=F32)
        tot = tot + bwd_ref[:, h * DV:(h + 1) * DV]
        mu = jnp.mean(tot, axis=-1, keepdims=True)
        cen = tot - mu
        var = jnp.mean(cen * cen, axis=-1, keepdims=True)
        yn = cen * lax.rsqrt(var + EPS) * gn_ref[:, h * DV:(h + 1) * DV]
        g = gate_ref[:, h * DV:(h + 1) * DV].astype(F32)
        o_ref[:, h * DV:(h + 1) * DV] = (yn * (g * jax.nn.sigmoid(g))).astype(o_ref.dtype)


def _retention(z, cosf, sinf, gn, seqs, c):
    t = z.shape[0]
    nch = seqs.nch
    rev = lambda i: nch - 1 - i
    qkv_specs = lambda f: [pl.BlockSpec((c, QK_W), lambda i: (f(i), 0)),
                           pl.BlockSpec((c, QK_W), lambda i: (f(i), 1)),
                           pl.BlockSpec((c, V_W), lambda i: (f(i), 1)),
                           pl.BlockSpec((c, DK), lambda i: (f(i), 0)),
                           pl.BlockSpec((c, DK), lambda i: (f(i), 0))]
    bwd = pl.pallas_call(
        functools.partial(_ret_bwd_kernel, seqs=seqs, c=c),
        grid=(nch,),
        in_specs=qkv_specs(rev),
        out_specs=pl.BlockSpec((c, V_W), lambda i: (rev(i), 0)),
        out_shape=jax.ShapeDtypeStruct((t, V_W), F32),
        scratch_shapes=[pltpu.VMEM((HEADS, DK, DV), F32)],
        compiler_params=_cp(("arbitrary",), 32),
        name="ret_bwd",
    )(z, z, z, cosf, sinf)
    fwd_id = lambda i: i
    return pl.pallas_call(
        functools.partial(_ret_fwd_kernel, seqs=seqs, c=c),
        grid=(nch,),
        in_specs=qkv_specs(fwd_id) + [pl.BlockSpec((c, V_W), lambda i: (i, 0)),
                                      pl.BlockSpec((c, V_W), lambda i: (i, 2)),
                                      pl.BlockSpec((1, V_W), lambda i: (0, 0))],
        out_specs=pl.BlockSpec((c, V_W), lambda i: (i, 0)),
        out_shape=jax.ShapeDtypeStruct((t, V_W), BF16),
        scratch_shapes=[pltpu.VMEM((HEADS, DK, DV), F32), pltpu.VMEM((HEADS, c, c), F32)],
        compiler_params=_cp(("arbitrary",), 40),
        name="ret_fwd",
    )(z, z, z, cosf, sinf, bwd, z, gn)


def _lru_kernel(*refs, reverse, seqs, c, cb):
    if reverse:
        (x_ref, xp_ref, xn_ref, cw_ref, cbias_ref, wa_ref, ba_ref, wx_ref, bx_ref, c8_ref, hf_ref, g_ref,
         o_ref, a_s, u_s, h_s, carry_s) = refs
    else:
        (x_ref, xp_ref, xn_ref, cw_ref, cbias_ref, wa_ref, ba_ref, wx_ref, bx_ref, c8_ref,
         o_ref, a_s, u_s, h_s, carry_s) = refs
    t = pl.program_id(1)
    ci = seqs.nch - 1 - t if reverse else t
    first = seqs.is_first(ci)
    last = seqs.is_last(ci)

    x = x_ref[...].astype(F32)
    prev = jnp.where(first, 0.0, xp_ref[...].astype(F32))
    nxt = jnp.where(last, 0.0, xn_ref[...].astype(F32))
    row8 = lax.broadcasted_iota(I32, (8, 1), 0)

    def patch(arr, at, rows8):
        parts = ([arr[:at]] if at > 0 else []) + [rows8] + ([arr[at + 8:]] if at + 8 < c else [])
        return jnp.concatenate(parts, axis=0)

    xm1 = pltpu.roll(x, 1, 0)
    xm1 = patch(xm1, 0, jnp.where(row8 == 0, prev[HALO - 1:HALO], xm1[0:8]))
    xm2 = pltpu.roll(x, 2, 0)
    xm2 = patch(xm2, 0, jnp.where(row8 == 0, prev[HALO - 2:HALO - 1],
                                  jnp.where(row8 == 1, prev[HALO - 1:HALO], xm2[0:8])))
    xp1 = pltpu.roll(x, c - 1, 0)
    xp1 = patch(xp1, c - 8, jnp.where(row8 == 7, nxt[0:1], xp1[c - 8:c]))
    cw = cw_ref[...]
    xc = cw[0:1] * xm2 + cw[1:2] * xm1 + cw[2:3] * x + cw[3:4] * xp1 + cbias_ref[...]

    nslab = cb // LRU_BLOCK
    for gi in range(nslab):
        sl = slice(gi * LRU_BLOCK, (gi + 1) * LRU_BLOCK)
        xs = xc[:, sl]
        xb = xs.astype(BF16)
        r = jax.nn.sigmoid(jnp.dot(xb, wa_ref[gi], preferred_element_type=F32) + ba_ref[:, sl])
        ig = jax.nn.sigmoid(jnp.dot(xb, wx_ref[gi], preferred_element_type=F32) + bx_ref[:, sl])
        a = jnp.exp(c8_ref[:, sl] * r)
        a_s[gi] = a
        u_s[gi] = jnp.sqrt(1.0 - a * a) * (ig * xs)

    @pl.when(last if reverse else first)
    def _():
        carry_s[...] = jnp.zeros_like(carry_s)

    sub = lax.broadcasted_iota(I32, (8, LRU_BLOCK), 0)
    nsuper = c // LRU_SUPER
    steps = list(range(LRU_SEG))
    if reverse:
        steps = steps[::-1]

    def super_group(q, carries):
        qi = nsuper - 1 - q if reverse else q
        base = qi * LRU_SUPER
        out = []
        for gi in range(nslab):
            hs, ps = {}, {}
            h = p = None
            for j in steps:
                rows = pl.ds(base + j, 8, stride=LRU_SEG)
                a = a_s[gi, rows, :]
                u = u_s[gi, rows, :]
                h = u if h is None else a * h + u
                p = a if p is None else a * p
                hs[j], ps[j] = h, p
            eh, ep = h, p
            for s in (1, 2, 4):
                shift = 8 - s if reverse else s
                m = (sub < 8 - s) if reverse else (sub >= s)
                eh_sh = pltpu.roll(eh, shift, 0)
                ep_sh = pltpu.roll(ep, shift, 0)
                eh = eh + ep * jnp.where(m, eh_sh, 0.0)
                ep = ep * jnp.where(m, ep_sh, 1.0)
            end = eh + ep * carries[gi]
            if reverse:
                enter = jnp.where(sub == 7, carries[gi], pltpu.roll(end, 7, 0))
                out.append(jnp.broadcast_to(end[0:1], (8, LRU_BLOCK)))
            else:
                enter = jnp.where(sub == 0, carries[gi], pltpu.roll(end, 1, 0))
                out.append(jnp.broadcast_to(end[7:8], (8, LRU_BLOCK)))
            for j in steps:
                h_s[gi, pl.ds(base + j, 8, stride=LRU_SEG), :] = hs[j] + ps[j] * enter
        return tuple(out)

    carries = lax.fori_loop(0, nsuper, super_group, tuple(carry_s[gi] for gi in range(nslab)))
    for gi in range(nslab):
        carry_s[gi] = carries[gi]

    for gi in range(nslab):
        sl = slice(gi * LRU_BLOCK, (gi + 1) * LRU_BLOCK)
        if reverse:
            g = g_ref[:, sl].astype(F32)
            o_ref[:, sl] = ((hf_ref[:, sl] + h_s[gi]) * jax.nn.gelu(g, approximate=True)).astype(o_ref.dtype)
        else:
            o_ref[:, sl] = h_s[gi]


def _lru_call(z, conv_w, conv_b, wa, ba, wx, bx, c8, hf, direction, seqs, c, cb):
    t = z.shape[0]
    nch = seqs.nch
    reverse = direction == 1
    tmap = (lambda ti: nch - 1 - ti) if reverse else (lambda ti: ti)
    xcol = (2 * QK_W + 2 * V_W) // cb
    gcol = (2 * QK_W + 2 * V_W + D) // cb
    hb = c // HALO
    nhb = t // HALO
    nb = cb // LRU_BLOCK
    in_specs = [
        pl.BlockSpec((c, cb), lambda ch, ti: (tmap(ti), xcol + ch)),
        pl.BlockSpec((HALO, cb), lambda ch, ti: (jnp.maximum(tmap(ti) * hb - 1, 0), xcol + ch)),
        pl.BlockSpec((HALO, cb), lambda ch, ti: (jnp.minimum((tmap(ti) + 1) * hb, nhb - 1), xcol + ch)),
        pl.BlockSpec((4, cb), lambda ch, ti: (0, ch)),
        pl.BlockSpec((1, cb), lambda ch, ti: (0, ch)),
        pl.BlockSpec((None, nb, LRU_BLOCK, LRU_BLOCK), lambda ch, ti: (direction, ch, 0, 0)),
        pl.BlockSpec((None, 1, cb), lambda ch, ti: (direction, 0, ch)),
        pl.BlockSpec((None, nb, LRU_BLOCK, LRU_BLOCK), lambda ch, ti: (direction, ch, 0, 0)),
        pl.BlockSpec((None, 1, cb), lambda ch, ti: (direction, 0, ch)),
        pl.BlockSpec((None, 1, cb), lambda ch, ti: (direction, 0, ch)),
    ]
    args = [z, z, z, conv_w, conv_b, wa, ba, wx, bx, c8]
    if reverse:
        in_specs += [pl.BlockSpec((c, cb), lambda ch, ti: (tmap(ti), ch)),
                     pl.BlockSpec((c, cb), lambda ch, ti: (tmap(ti), gcol + ch))]
        args += [hf, z]
    return pl.pallas_call(
        functools.partial(_lru_kernel, reverse=reverse, seqs=seqs, c=c, cb=cb),
        grid=(D // cb, nch),
        in_specs=in_specs,
        out_specs=pl.BlockSpec((c, cb), lambda ch, ti: (tmap(ti), ch)),
        out_shape=jax.ShapeDtypeStruct((t, D), BF16 if reverse else F32),
        scratch_shapes=[pltpu.VMEM((nb, c, LRU_BLOCK), F32), pltpu.VMEM((nb, c, LRU_BLOCK), F32),
                        pltpu.VMEM((nb, c, LRU_BLOCK), F32), pltpu.VMEM((nb, 8, LRU_BLOCK), F32)],
        compiler_params=_cp(("arbitrary", "arbitrary"),32),
        name="lru_rev" if reverse else "lru_fwd",
    )(*args)


def _proj_merge_kernel(yr_ref, yl_ref, wr_ref, wl_ref, mr_ref, ml_ref, o_ref):
    r = jnp.dot(yr_ref[...], wr_ref[...], preferred_element_type=F32)
    l = jnp.dot(yl_ref[...], wl_ref[...], preferred_element_type=F32)
    o_ref[...] = (jax.nn.sigmoid(mr_ref[...].astype(F32)) * r
                  + jax.nn.sigmoid(ml_ref[...].astype(F32)) * l).astype(o_ref.dtype)


def _proj_merge(y_ret, y_lru, w_ret_o, w_lru_o, z, tm=512, tn=1024):
    t = y_ret.shape[0]
    mcol = (2 * QK_W + 2 * V_W + 2 * D) // tn
    return pl.pallas_call(
        _proj_merge_kernel,
        grid=(D // tn, t // tm),
        in_specs=[pl.BlockSpec((tm, V_W), lambda j, i: (i, 0)),
                  pl.BlockSpec((tm, D), lambda j, i: (i, 0)),
                  pl.BlockSpec((V_W, tn), lambda j, i: (0, j)),
                  pl.BlockSpec((D, tn), lambda j, i: (0, j)),
                  pl.BlockSpec((tm, tn), lambda j, i: (i, mcol + j)),
                  pl.BlockSpec((tm, tn), lambda j, i: (i, mcol + D // tn + j))],
        out_specs=pl.BlockSpec((tm, tn), lambda j, i: (i, j)),
        out_shape=jax.ShapeDtypeStruct((t, D), BF16),
        compiler_params=_cp(("arbitrary", "arbitrary"), 48),
        name="proj_merge",
    )(y_ret, y_lru, w_ret_o, w_lru_o, z, z)


def _proj_out_kernel(m_ref, w_ref, xp_ref, xs_ref, o_ref, *, npt):
    i = pl.program_id(1)
    y = jnp.dot(m_ref[...], w_ref[...], preferred_element_type=F32)

    @pl.when(i < npt)
    def _():
        o_ref[...] = xp_ref[...] + y

    @pl.when(i >= npt)
    def _():
        o_ref[...] = xs_ref[...] + y


def _proj_out(merged, w_out, xp, xs, tm=512, tn=1024):
    t = merged.shape[0]
    npt = xp.shape[0] // tm
    return pl.pallas_call(
        functools.partial(_proj_out_kernel, npt=npt),
        grid=(D // tn, t // tm),
        in_specs=[pl.BlockSpec((tm, D), lambda j, i: (i, 0)),
                  pl.BlockSpec((D, tn), lambda j, i: (0, j)),
                  pl.BlockSpec((tm, tn), lambda j, i: (jnp.minimum(i, npt - 1), j)),
                  pl.BlockSpec((tm, tn), lambda j, i: (jnp.maximum(i - npt, 0), j))],
        out_specs=pl.BlockSpec((tm, tn), lambda j, i: (i, j)),
        out_shape=jax.ShapeDtypeStruct((t, D), F32),
        compiler_params=_cp(("arbitrary", "arbitrary"), 48),
        name="proj_out",
    )(merged, w_out, xp, xs)


def _router_kernel(x_ref, g_ref, w_ref, b_ref, xn_ref, tpos_ref, tpos_t_ref, gate_ref, meta_ref, tri_ref, run_ref,
                   *, tm):
    i = pl.program_id(0)

    @pl.when(i == 0)
    def _():
        r = lax.broadcasted_iota(I32, (tm, tm), 0)
        col = lax.broadcasted_iota(I32, (tm, tm), 1)
        tri_ref[...] = (col < r).astype(BF16)
        run_ref[...] = jnp.zeros_like(run_ref)

    x = x_ref[...]
    xn = x * lax.rsqrt(jnp.mean(x * x, axis=-1, keepdims=True) + EPS) * g_ref[...]
    xn_ref[...] = xn.astype(BF16)

    logits = jnp.dot(xn, w_ref[...], preferred_element_type=F32, precision=lax.Precision.HIGHEST) + b_ref[...]
    lane = lax.broadcasted_iota(I32, (tm, LANE), 1)
    lane_f = lane.astype(F32)
    vals, idxs = [], []
    cur = logits
    for _ in range(TOP_K):
        m = jnp.max(cur, axis=-1, keepdims=True)
        idx = jnp.min(jnp.where(cur == m, lane_f, float(LANE)), axis=-1, keepdims=True).astype(I32)
        vals.append(m)
        idxs.append(idx)
        cur = jnp.where(lane == idx, -jnp.inf, cur)
    exps = [jnp.exp(v - vals[0]) for v in vals]
    denom = exps[0] + exps[1] + exps[2] + exps[3]

    onehot = jnp.zeros((tm, LANE), F32)
    for idx in idxs:
        onehot = onehot + (lane == idx).astype(F32)
    before = jnp.dot(tri_ref[...], onehot.astype(BF16), preferred_element_type=F32)

    cnt = jnp.broadcast_to(jnp.sum(onehot, axis=0, keepdims=True), (8, LANE))
    cnt8 = jnp.floor((cnt + (SEG_ALIGN - 1)) * (1.0 / SEG_ALIGN)) * SEG_ALIGN
    lane8 = lax.broadcasted_iota(I32, (8, LANE), 1)
    incl = cnt8
    s = 1
    while s < LANE:
        incl = incl + jnp.where(lane8 >= s, pltpu.roll(incl, s, 1), 0.0)
        s *= 2
    toff = incl - cnt8

    pos = before + toff[0:1]
    tpos_out = jnp.zeros((tm, LANE), F32)
    gate_out = jnp.zeros((tm, LANE), F32)
    for k in range(TOP_K):
        tpos_k = jnp.sum(jnp.where(lane == idxs[k], pos, 0.0), axis=-1, keepdims=True)
        tpos_out = jnp.where(lane == k, tpos_k, tpos_out)
        gate_out = jnp.where(lane == k, exps[k] / denom, gate_out)
    tpos_ref[...] = tpos_out.astype(I32)
    tpos_t_ref[...] = tpos_out.T[0:8].astype(I32)
    gate_ref[...] = gate_out

    sub8 = lax.broadcasted_iota(I32, (8, LANE), 0)
    run = jnp.broadcast_to(run_ref[...], (8, LANE))
    meta_ref[...] = jnp.where(sub8 == 0, cnt8, jnp.where(sub8 == 1, toff, jnp.where(sub8 == 2, run, 0.0)))
    run_ref[...] = run_ref[...] + cnt8[0:1]


def _router(x1, g, w_pad, b_pad, tm):
    t = x1.shape[0]
    nt = t // tm
    tok = lambda i: (i, 0)
    fixed = lambda i: (0, 0)
    per_tile = lambda i: (i, 0, 0)
    return pl.pallas_call(
        functools.partial(_router_kernel, tm=tm),
        grid=(nt,),
        in_specs=[pl.BlockSpec((tm, D), tok), pl.BlockSpec((1, D), fixed),
                  pl.BlockSpec((D, LANE), fixed), pl.BlockSpec((1, LANE), fixed)],
        out_specs=[pl.BlockSpec((tm, D), tok), pl.BlockSpec((tm, LANE), tok),
                   pl.BlockSpec((None, 8, tm), per_tile), pl.BlockSpec((tm, LANE), tok),
                   pl.BlockSpec((None, 8, LANE), per_tile)],
        out_shape=[jax.ShapeDtypeStruct((t, D), BF16), jax.ShapeDtypeStruct((t, LANE), I32),
                   jax.ShapeDtypeStruct((nt, 8, tm), I32), jax.ShapeDtypeStruct((t, LANE), F32),
                   jax.ShapeDtypeStruct((nt, 8, LANE), F32)],
        scratch_shapes=[pltpu.VMEM((tm, tm), BF16), pltpu.VMEM((1, LANE), F32)],
        compiler_params=_cp(("arbitrary",), 32),
        name="router",
    )(x1, g, w_pad, b_pad)


def _segment_copies(i, toff_ref, c8_ref, make_copy, act):
    def per_expert(e, carry):
        base = i * N_EXP + e
        n = c8_ref[base] // SEG_ALIGN
        t0 = toff_ref[base]
        off = 0
        for b in range(SEG_MAXBIT, -1, -1):
            rows = SEG_ALIGN << b
            bit = (n >> b) & 1

            @pl.when(bit == 1)
            def _(off=off, rows=rows):
                act(make_copy(pl.multiple_of(t0 + off, SEG_ALIGN), off, e, rows))

            off = off + bit * rows
        return carry

    lax.fori_loop(0, N_EXP, per_expert, 0)


def _dispatch_kernel(toff_ref, c8_ref, dst_ref, zblk_ref, tpos_t_ref, x_ref, xg_ref, sorted_ref, zero_ref, sem):
    i = pl.program_id(0)

    def zero_copy(e):
        start = pl.multiple_of(jnp.maximum(zblk_ref[e], 0), MOE_ROWS)
        return pltpu.make_async_copy(zero_ref, xg_ref.at[pl.ds(start, MOE_ROWS), :], sem)

    def for_nonempty(act):
        def body(e, carry):
            pl.when(zblk_ref[e] >= 0)(lambda: act(zero_copy(e)))
            return carry
        lax.fori_loop(0, N_EXP, body, 0)

    @pl.when(i == 0)
    def _():
        zero_ref[...] = jnp.zeros_like(zero_ref)
        for_nonempty(lambda cp: cp.start())
        for_nonempty(lambda cp: cp.wait())

    x = x_ref[...]
    for c in range(SORT_ROWS // PERM_CHUNK):
        p = c * PERM_CHUNK + lax.broadcasted_iota(I32, (PERM_CHUNK, 1), 0)
        perm = jnp.zeros((PERM_CHUNK, x.shape[0]), F32)
        for k in range(TOP_K):
            perm = perm + jnp.where(p == tpos_t_ref[k:k + 1, :], 1.0, 0.0)
        rows = jnp.dot(perm.astype(BF16), x, preferred_element_type=F32)
        sorted_ref[c * PERM_CHUNK:(c + 1) * PERM_CHUNK, :] = _pack_pair(rows[:, :HALF], rows[:, HALF:], rounded=True)

    def make_copy(t_off, s_off, e, rows):
        d = pl.multiple_of(dst_ref[i * N_EXP + e] + s_off, SEG_ALIGN)
        return pltpu.make_async_copy(sorted_ref.at[pl.ds(t_off, rows), :], xg_ref.at[pl.ds(d, rows), :], sem)

    _segment_copies(i, toff_ref, c8_ref, make_copy, lambda cp: cp.start())
    _segment_copies(i, toff_ref, c8_ref, make_copy, lambda cp: cp.wait())


def _dispatch(toff, c8, dst, zblk, tpos_t, xn, rows):
    nt, _, tm = tpos_t.shape
    grid_spec = pltpu.PrefetchScalarGridSpec(
        num_scalar_prefetch=4,
        grid=(nt,),
        in_specs=[pl.BlockSpec((None, 8, tm), lambda i, *_: (i, 0, 0)),
                  pl.BlockSpec((tm, D), lambda i, *_: (i, 0))],
        out_specs=pl.BlockSpec(memory_space=pl.ANY),
        scratch_shapes=[pltpu.VMEM((SORT_ROWS, HALF), U32), pltpu.VMEM((MOE_ROWS, HALF), U32),
                        pltpu.SemaphoreType.DMA(())],
    )
    return pl.pallas_call(
        _dispatch_kernel,
        grid_spec=grid_spec,
        out_shape=jax.ShapeDtypeStruct((rows, HALF), U32),
        compiler_params=_cp(("arbitrary",), 40),
        name="dispatch",
    )(toff, c8, dst, zblk, tpos_t, xn)


def _expert_kernel(be_ref, nu_ref, xg_ref, wg_ref, bg_ref, wu_ref, bu_ref, wd_ref, bd_ref, o_ref, xb_ref, acc_ref,
                   *, nj):
    del be_ref
    i = pl.program_id(0)
    j = pl.program_id(1)

    @pl.when(i < nu_ref[0])
    def _():
        @pl.when(j == 0)
        def _():
            lo, hi = _unpack_pair(xg_ref[...])
            xb_ref[:, :HALF] = lo.astype(BF16)
            xb_ref[:, HALF:] = hi.astype(BF16)
            acc_ref[...] = jnp.broadcast_to(bd_ref[...], acc_ref.shape)

        xb = xb_ref[...]
        gate = jnp.minimum(jnp.dot(xb, wg_ref[...], preferred_element_type=F32) + bg_ref[...], SWIGLU_LIMIT)
        up = jnp.clip(jnp.dot(xb, wu_ref[...], preferred_element_type=F32) + bu_ref[...],
                      -SWIGLU_LIMIT, SWIGLU_LIMIT)
        hid = (up + 1.0) * gate * jax.nn.sigmoid(SWIGLU_ALPHA * gate)
        acc_ref[...] += jnp.dot(hid.astype(BF16), wd_ref[...], preferred_element_type=F32)

        @pl.when(j == nj - 1)
        def _():
            acc = acc_ref[...]
            o_ref[...] = _pack_pair(acc[:, :HALF], acc[:, HALF:])


def _experts(block_expert, n_used, xg, wg, bg, wu, bu, wd, bd, rows_blk, tf):
    rows = xg.shape[0]
    nb = rows // rows_blk
    nj = D_FF // tf

    def blk(i, j, be, nu):
        return (jnp.minimum(i, nu[0] - 1), 0)

    def jj(i, j, nu):
        return jnp.where(i < nu[0], j, nj - 1)

    grid_spec = pltpu.PrefetchScalarGridSpec(
        num_scalar_prefetch=2,
        grid=(nb, nj),
        in_specs=[pl.BlockSpec((rows_blk, HALF), blk),
                  pl.BlockSpec((None, D, tf), lambda i, j, be, nu: (be[i], 0, jj(i, j, nu))),
                  pl.BlockSpec((None, 1, tf), lambda i, j, be, nu: (be[i], 0, jj(i, j, nu))),
                  pl.BlockSpec((None, D, tf), lambda i, j, be, nu: (be[i], 0, jj(i, j, nu))),
                  pl.BlockSpec((None, 1, tf), lambda i, j, be, nu: (be[i], 0, jj(i, j, nu))),
                  pl.BlockSpec((None, tf, D), lambda i, j, be, nu: (be[i], jj(i, j, nu), 0)),
                  pl.BlockSpec((None, 1, D), lambda i, j, be, nu: (be[i], 0, 0))],
        out_specs=pl.BlockSpec((rows_blk, HALF), blk),
        scratch_shapes=[pltpu.VMEM((rows_blk, D), BF16), pltpu.VMEM((rows_blk, D), F32)],
    )
    return pl.pallas_call(
        functools.partial(_expert_kernel, nj=nj),
        grid_spec=grid_spec,
        out_shape=jax.ShapeDtypeStruct((rows, HALF), U32),
        compiler_params=_cp(("arbitrary", "arbitrary"), 56),
        name="experts",
    )(block_expert, n_used, xg, wg, bg, wu, bu, wd, bd)


def _combine_kernel(toff_ref, c8_ref, dst_ref, os_ref, tpos_ref, gate_ref, x1_ref, fg_ref, op_ref, osm_ref,
                    sorted_ref, sem, *, npt, nt):
    i = pl.program_id(0)
    slot = i % 2

    def copies(tile, buf, act):
        def make_copy(t_off, s_off, e, rows):
            d = pl.multiple_of(dst_ref[tile * N_EXP + e] + s_off, SEG_ALIGN)
            return pltpu.make_async_copy(os_ref.at[pl.ds(d, rows), :], sorted_ref.at[buf, pl.ds(t_off, rows), :],
                                         sem.at[buf])
        _segment_copies(tile, toff_ref, c8_ref, make_copy, act)

    pl.when(i == 0)(lambda: copies(0, 0, lambda cp: cp.start()))
    pl.when(i + 1 < nt)(lambda: copies(i + 1, 1 - slot, lambda cp: cp.start()))
    copies(i, slot, lambda cp: cp.wait())

    last = i * N_EXP + N_EXP - 1
    used = toff_ref[last] + c8_ref[last]
    tpos = tpos_ref[...]
    gates = gate_ref[...]
    y_lo = x1_ref[:, :HALF]
    y_hi = x1_ref[:, HALF:]
    for c in range(SORT_ROWS // PERM_CHUNK):
        p_lane = c * PERM_CHUNK + lax.broadcasted_iota(I32, (1, PERM_CHUNK), 1)
        place = jnp.zeros((tpos.shape[0], PERM_CHUNK), F32)
        for k in range(TOP_K):
            place = place + jnp.where(tpos[:, k:k + 1] == p_lane, gates[:, k:k + 1], 0.0)
        p_row = c * PERM_CHUNK + lax.broadcasted_iota(I32, (PERM_CHUNK, 1), 0)
        u = jnp.where(p_row < used, sorted_ref[slot, c * PERM_CHUNK:(c + 1) * PERM_CHUNK, :], jnp.uint32(0))
        lo, hi = _unpack_pair(u)
        place = place.astype(BF16)
        y_lo = y_lo + jnp.dot(place, lo.astype(BF16), preferred_element_type=F32)
        y_hi = y_hi + jnp.dot(place, hi.astype(BF16), preferred_element_type=F32)
    ms = (jnp.sum(y_lo * y_lo, axis=-1, keepdims=True) + jnp.sum(y_hi * y_hi, axis=-1, keepdims=True)) / D
    inv = lax.rsqrt(ms + EPS)
    out = jnp.concatenate([y_lo * inv * fg_ref[:, :HALF], y_hi * inv * fg_ref[:, HALF:]], axis=-1)

    @pl.when(i < npt)
    def _():
        op_ref[...] = out

    @pl.when(i >= npt)
    def _():
        osm_ref[...] = out


def _combine(toff, c8, dst, out_sorted, tpos, gates, x1, fg, tp, tm):
    t = x1.shape[0]
    npt = tp // tm
    tok = lambda i, *_: (i, 0)
    grid_spec = pltpu.PrefetchScalarGridSpec(
        num_scalar_prefetch=3,
        grid=(t // tm,),
        in_specs=[pl.BlockSpec(memory_space=pl.ANY),
                  pl.BlockSpec((tm, LANE), tok), pl.BlockSpec((tm, LANE), tok), pl.BlockSpec((tm, D), tok),
                  pl.BlockSpec((1, D), lambda i, *_: (0, 0))],
        out_specs=[pl.BlockSpec((tm, D), lambda i, *_: (jnp.minimum(i, npt - 1), 0)),
                   pl.BlockSpec((tm, D), lambda i, *_: (jnp.maximum(i - npt, 0), 0))],
        scratch_shapes=[pltpu.VMEM((2, SORT_ROWS, HALF), U32), pltpu.SemaphoreType.DMA((2,))],
    )
    return pl.pallas_call(
        functools.partial(_combine_kernel, npt=npt, nt=t // tm),
        grid_spec=grid_spec,
        out_shape=[jax.ShapeDtypeStruct((tp, D), F32), jax.ShapeDtypeStruct((t - tp, D), F32)],
        compiler_params=_cp(("arbitrary",), 56),
        name="combine",
    )(toff, c8, dst, out_sorted, tpos, gates, x1, fg)


def _rope_tables(n_prompt, prompt_len, sample_len):
    half = DK // 2
    inv = ROPE_BASE ** (-jnp.arange(half, dtype=F32) / half)
    pos = jnp.concatenate([jnp.tile(jnp.arange(prompt_len, dtype=F32), n_prompt),
                           jnp.arange(sample_len, dtype=F32)])
    ang = pos[:, None] * inv[None, :]
    cos, sin = jnp.cos(ang), jnp.sin(ang)
    return jnp.concatenate([cos, cos], axis=-1), jnp.concatenate([-sin, sin], axis=-1)


def _layer(xp, xs, n_prompt, prompt_len, sample_len, mix_norm_g, w_in, ret_norm_g, w_ret_o, conv_w, conv_b, lru_w_a,
           lru_b_a, lru_w_x, lru_b_x, lru_lambda, w_lru_o, w_out, moe_norm_g, w_router, b_router, w_e_gate,
           b_e_gate, w_e_up, b_e_up, w_e_down, b_e_down, final_norm_g):
    tp = xp.shape[0]
    t = tp + xs.shape[0]
    seqs = _Seqs(n_prompt, prompt_len, sample_len, SEQ_CHUNK)
    row = lambda v: v.reshape(1, -1)

    z = _inproj(xp, xs, row(mix_norm_g), w_in.astype(BF16))

    cosf, sinf = _rope_tables(n_prompt, prompt_len, sample_len)
    y_ret = _retention(z, cosf, sinf, row(ret_norm_g), seqs, SEQ_CHUNK)

    wa, wx = lru_w_a.astype(BF16), lru_w_x.astype(BF16)
    ba, bx = lru_b_a[:, None, :], lru_b_x[:, None, :]
    c8 = (-LRU_C * jax.nn.softplus(-lru_lambda))[:, None, :]
    lru_args = (z, conv_w, row(conv_b), wa, ba, wx, bx, c8)
    h_fwd = _lru_call(*lru_args, None, 0, seqs, SEQ_CHUNK, LRU_CB)
    y_lru = _lru_call(*lru_args, h_fwd, 1, seqs, SEQ_CHUNK, LRU_CB)

    merged = _proj_merge(y_ret, y_lru, w_ret_o.astype(BF16), w_lru_o.astype(BF16), z)
    x1 = _proj_out(merged, w_out.astype(BF16), xp, xs)

    w_pad = jnp.zeros((D, LANE), F32).at[:, :N_EXP].set(w_router)
    b_pad = jnp.full((1, LANE), -1e30, F32).at[0, :N_EXP].set(b_router)
    xn2, tpos, tpos_t, gates, meta = _router(x1, row(moe_norm_g), w_pad, b_pad, TOK_TILE)

    nt = t // TOK_TILE
    c8 = meta[:, 0, :N_EXP].astype(I32)
    toff = meta[:, 1, :N_EXP].astype(I32)
    run = meta[:, 2, :N_EXP].astype(I32)
    total = run[-1] + c8[-1]
    padded = (total + MOE_ROWS - 1) // MOE_ROWS * MOE_ROWS
    pad_ends = jnp.cumsum(padded)
    pad_starts = pad_ends - padded
    dst = (pad_starts[None, :] + run).reshape(-1)
    zblk = jnp.where(padded > 0, pad_ends - MOE_ROWS, -1)
    n_blocks = (t * TOP_K + (SEG_ALIGN - 1) * nt * N_EXP) // MOE_ROWS + 1 + N_EXP
    n_used = (pad_ends[-1] // MOE_ROWS).astype(I32)
    blk_ids = jnp.minimum(jnp.arange(n_blocks, dtype=I32), n_used - 1)
    block_expert = jnp.minimum(
        jnp.sum((pad_ends[None, :] <= (blk_ids * MOE_ROWS)[:, None]).astype(I32), axis=-1), N_EXP - 1)
    toff, c8 = toff.reshape(-1), c8.reshape(-1)

    xg = _dispatch(toff, c8, dst, zblk, tpos_t, xn2, n_blocks * MOE_ROWS)
    out_sorted = _experts(block_expert, n_used.reshape(1), xg,
                          w_e_gate.astype(BF16), b_e_gate[:, None, :], w_e_up.astype(BF16), b_e_up[:, None, :],
                          w_e_down.astype(BF16), b_e_down[:, None, :], MOE_ROWS, MOE_TF)
    return _combine(toff, c8, dst, out_sorted, tpos, gates, x1, row(final_norm_g), tp, TOK_TILE)


def kernel(x_prompt, x_sample, mix_norm_g, w_in, ret_norm_g, w_ret_o, conv_w, conv_b, lru_w_a, lru_b_a, lru_w_x,
           lru_b_x, lru_lambda, w_lru_o, w_out, moe_norm_g, w_router, b_router, w_e_gate, b_e_gate, w_e_up, b_e_up,
           w_e_down, b_e_down, final_norm_g):
    assert mix_norm_g.shape[0] == 1, "one layer"
    n_prompt, prompt_len, _ = x_prompt.shape
    n_sample, sample_len, _ = x_sample.shape
    assert n_sample == 1
    tp = n_prompt * prompt_len
    yp, ys = _layer(x_prompt.reshape(tp, D), x_sample.reshape(sample_len, D), n_prompt, prompt_len, sample_len,
                    mix_norm_g[0], w_in[0], ret_norm_g[0], w_ret_o[0], conv_w[0],
                    conv_b[0], lru_w_a[0], lru_b_a[0], lru_w_x[0], lru_b_x[0], lru_lambda[0], w_lru_o[0], w_out[0],
                    moe_norm_g[0], w_router[0], b_router[0], w_e_gate[0], b_e_gate[0], w_e_up[0], b_e_up[0],
                    w_e_down[0], b_e_down[0], final_norm_g)
    return yp.reshape(x_prompt.shape), ys.reshape(x_sample.shape)
```

```python
import functools
import math

import jax
import jax.numpy as jnp
from jax import lax
from jax.experimental import pallas as pl
from jax.experimental.pallas import tpu as pltpu

F32 = jnp.float32
BF16 = jnp.bfloat16
U32 = jnp.uint32
I32 = jnp.int32

D = 2048
HEADS = 8
DK = 128
DV = 256
QK_W = HEADS * DK
V_W = HEADS * DV
LRU_BLOCK = 128
N_EXP = 32
TOP_K = 4
D_FF = 2048
SWIGLU_LIMIT = 7.0
SWIGLU_ALPHA = 1.702
ROPE_BASE = 10000.0
LRU_C = 8.0
EPS = 1e-6
IN_W = 2 * QK_W + 2 * V_W + 2 * D + 2 * D

HALF = D // 2
LANE = 128
HALO = 16
HI_MASK = 0xFFFF0000

SEQ_CHUNK = 256
LRU_CHUNK = 512
LRU_CB = 512
LRU_SEG = 4
LRU_SUPER = 8 * LRU_SEG
MOE_ROWS = 512
MOE_TF = 1024
MOE_TN = 512
TOK_TILE = 512
SEG_ALIGN = 8
PERM_CHUNK = 256
SORT_ROWS = TOK_TILE * 4 + 32 * SEG_ALIGN
SEG_MAXBIT = (TOK_TILE // SEG_ALIGN).bit_length() - 1


def _cp(sem, vmem_mb):
    return pltpu.CompilerParams(dimension_semantics=sem, vmem_limit_bytes=vmem_mb << 20)


def _bits(x):
    return lax.bitcast_convert_type(x, U32)


def _pack_pair(lo, hi, rounded=False):
    if not rounded:
        lo = lo.astype(BF16).astype(F32)
        hi = hi.astype(BF16).astype(F32)
    return (_bits(lo) >> 16) | (_bits(hi) & jnp.uint32(HI_MASK))


def _unpack_pair(u):
    lo = lax.bitcast_convert_type(u << 16, F32)
    hi = lax.bitcast_convert_type(u & jnp.uint32(HI_MASK), F32)
    return lo, hi


class _Seqs:
    def __init__(self, n_prompt, prompt_len, sample_len, chunk):
        assert prompt_len % chunk == 0 and sample_len % chunk == 0
        self.cps = prompt_len // chunk
        self.npc = n_prompt * self.cps
        self.nch = self.npc + sample_len // chunk

    def is_first(self, ci):
        return ((ci < self.npc) & (ci % self.cps == 0)) | (ci == self.npc)

    def is_last(self, ci):
        return ((ci < self.npc) & (ci % self.cps == self.cps - 1)) | (ci == self.nch - 1)


def _two_group_specs(tm, npt, width=D):
    return [pl.BlockSpec((tm, width), lambda i, j: (jnp.minimum(i, npt - 1), 0)),
            pl.BlockSpec((tm, width), lambda i, j: (jnp.maximum(i - npt, 0), 0))]


def _inproj_kernel(xp_ref, xs_ref, g_ref, w_ref, o_ref, xn_ref, *, npt):
    i = pl.program_id(0)

    def norm(x_ref):
        x = x_ref[...]
        y = x * lax.rsqrt(jnp.mean(x * x, axis=-1, keepdims=True) + EPS)
        xn_ref[...] = (y * g_ref[...]).astype(BF16)

    @pl.when(pl.program_id(1) == 0)
    def _():
        pl.when(i < npt)(lambda: norm(xp_ref))
        pl.when(i >= npt)(lambda: norm(xs_ref))

    o_ref[...] = jnp.dot(xn_ref[...], w_ref[...], preferred_element_type=F32).astype(o_ref.dtype)


def _inproj(xp, xs, g, w, tm=1024, tn=1024):
    t, n = xp.shape[0] + xs.shape[0], w.shape[1]
    npt = xp.shape[0] // tm
    return pl.pallas_call(
        functools.partial(_inproj_kernel, npt=npt),
        grid=(t // tm, n // tn),
        in_specs=_two_group_specs(tm, npt) + [pl.BlockSpec((1, D), lambda i, j: (0, 0)),
                                              pl.BlockSpec((D, tn), lambda i, j: (0, j))],
        out_specs=pl.BlockSpec((tm, tn), lambda i, j: (i, j)),
        out_shape=jax.ShapeDtypeStruct((t, n), BF16),
        scratch_shapes=[pltpu.VMEM((tm, D), BF16)],
        compiler_params=_cp(("arbitrary", "arbitrary"), 56),
        name="inproj",
    )(xp, xs, g, w)


def _rot(x, cosf, sinf):
    return x * cosf + pltpu.roll(x, DK // 2, 1) * sinf


def _log_gamma(h):
    return math.log1p(-(2.0 ** (-5 - h)))


def _decay_tables(dec_ref, c, q_exponent, k_exponent):
    p = lax.broadcasted_iota(I32, (c, DK), 0).astype(F32)
    for h in range(HEADS):
        lg = _log_gamma(h)
        dec_ref[h, 0] = jnp.exp(lg * q_exponent(p))
        dec_ref[h, 1] = jnp.exp(lg * k_exponent(p))


def _ret_bwd_kernel(q_ref, k_ref, v_ref, cos_ref, sin_ref, o_ref, s_ref, dec_ref, *, seqs, c):
    ci = seqs.nch - 1 - pl.program_id(0)

    @pl.when(pl.program_id(0) == 0)
    def _():
        _decay_tables(dec_ref, c, lambda p: c - p, lambda p: p)

    @pl.when(seqs.is_last(ci))
    def _():
        s_ref[...] = jnp.zeros_like(s_ref)

    cosf = cos_ref[...]
    sinf = sin_ref[...]
    for h in range(HEADS):
        lg = _log_gamma(h)
        q = _rot(q_ref[:, h * DK:(h + 1) * DK].astype(F32), cosf, sinf)
        k = _rot(k_ref[:, h * DK:(h + 1) * DK].astype(F32), cosf, sinf) * (DK ** -0.5)
        v = v_ref[:, h * DV:(h + 1) * DV]
        qd = (q * dec_ref[h, 0]).astype(BF16)
        kd = (k * dec_ref[h, 1]).astype(BF16)
        s = s_ref[h]
        o_ref[:, h * DV:(h + 1) * DV] = jnp.dot(qd, s.astype(BF16), preferred_element_type=F32)
        s_ref[h] = math.exp(lg * c) * s + lax.dot_general(
            kd, v, (((0,), (0,)), ((), ())), preferred_element_type=F32)


def _ret_fwd_kernel(q_ref, k_ref, v_ref, cos_ref, sin_ref, bwd_ref, gate_ref, gn_ref, o_ref, s_ref, dm_ref,
                    dec_ref, *, seqs, c):
    ci = pl.program_id(0)

    @pl.when(ci == 0)
    def _():
        _decay_tables(dec_ref, c, lambda p: p + 1.0, lambda p: c - 1.0 - p)
        r = lax.broadcasted_iota(I32, (c, c), 0)
        col = lax.broadcasted_iota(I32, (c, c), 1)
        dist = jnp.abs(r - col).astype(F32)
        for h in range(HEADS):
            dm_ref[h] = jnp.exp(_log_gamma(h) * dist)

    @pl.when(seqs.is_first(ci))
    def _():
        s_ref[...] = jnp.zeros_like(s_ref)

    cosf = cos_ref[...]
    sinf = sin_ref[...]
    for h in range(HEADS):
        lg = _log_gamma(h)
        q = _rot(q_ref[:, h * DK:(h + 1) * DK].astype(F32), cosf, sinf)
        k = _rot(k_ref[:, h * DK:(h + 1) * DK].astype(F32), cosf, sinf) * (DK ** -0.5)
        v = v_ref[:, h * DV:(h + 1) * DV]
        scores = lax.dot_general(q.astype(BF16), k.astype(BF16), (((1,), (1,)), ((), ())),
                                 preferred_element_type=F32) * dm_ref[h]
        tot = jnp.dot(scores.astype(BF16), v, preferred_element_type=F32)
        qd = (q * dec_ref[h, 0]).astype(BF16)
        kd = (k * dec_ref[h, 1]).astype(BF16)
        s = s_ref[h]
        tot = tot + jnp.dot(qd, s.astype(BF16), preferred_element_type=F32)
        s_ref[h] = math.exp(lg * c) * s + lax.dot_general(
            kd, v, (((0,), (0,)), ((), ())), preferred_element_type=F32)
        tot = tot + bwd_ref[:, h * DV:(h + 1) * DV]
        mu = jnp.mean(tot, axis=-1, keepdims=True)
        cen = tot - mu
        var = jnp.mean(cen * cen, axis=-1, keepdims=True)
        yn = cen * lax.rsqrt(var + EPS) * gn_ref[:, h * DV:(h + 1) * DV]
        g = gate_ref[:, h * DV:(h + 1) * DV].astype(F32)
        o_ref[:, h * DV:(h + 1) * DV] = (yn * (g * jax.nn.sigmoid(g))).astype(o_ref.dtype)


def _retention(z, cosf, sinf, gn, seqs, c):
    t = z.shape[0]
    nch = seqs.nch
    rev = lambda i: nch - 1 - i
    qkv_specs = lambda f: [pl.BlockSpec((c, QK_W), lambda i: (f(i), 0)),
                           pl.BlockSpec((c, QK_W), lambda i: (f(i), 1)),
                           pl.BlockSpec((c, V_W), lambda i: (f(i), 1)),
                           pl.BlockSpec((c, DK), lambda i: (f(i), 0)),
                           pl.BlockSpec((c, DK), lambda i: (f(i), 0))]
    bwd = pl.pallas_call(
        functools.partial(_ret_bwd_kernel, seqs=seqs, c=c),
        grid=(nch,),
        in_specs=qkv_specs(rev),
        out_specs=pl.BlockSpec((c, V_W), lambda i: (rev(i), 0)),
        out_shape=jax.ShapeDtypeStruct((t, V_W), F32),
        scratch_shapes=[pltpu.VMEM((HEADS, DK, DV), F32), pltpu.VMEM((HEADS, 2, c, DK), F32)],
        compiler_params=_cp(("arbitrary",), 32),
        name="ret_bwd",
    )(z, z, z, cosf, sinf)
    fwd_id = lambda i: i
    return pl.pallas_call(
        functools.partial(_ret_fwd_kernel, seqs=seqs, c=c),
        grid=(nch,),
        in_specs=qkv_specs(fwd_id) + [pl.BlockSpec((c, V_W), lambda i: (i, 0)),
                                      pl.BlockSpec((c, V_W), lambda i: (i, 2)),
                                      pl.BlockSpec((1, V_W), lambda i: (0, 0))],
        out_specs=pl.BlockSpec((c, V_W), lambda i: (i, 0)),
        out_shape=jax.ShapeDtypeStruct((t, V_W), BF16),
        scratch_shapes=[pltpu.VMEM((HEADS, DK, DV), F32), pltpu.VMEM((HEADS, c, c), F32),
                        pltpu.VMEM((HEADS, 2, c, DK), F32)],
        compiler_params=_cp(("arbitrary",), 40),
        name="ret_fwd",
    )(z, z, z, cosf, sinf, bwd, z, gn)


def _lru_kernel(*refs, reverse, seqs, c, cb):
    if reverse:
        (x_ref, xp_ref, xn_ref, cw_ref, cbias_ref, wa_ref, ba_ref, wx_ref, bx_ref, c8_ref, hf_ref, g_ref,
         o_ref, a_s, u_s, h_s, carry_s) = refs
    else:
        (x_ref, xp_ref, xn_ref, cw_ref, cbias_ref, wa_ref, ba_ref, wx_ref, bx_ref, c8_ref,
         o_ref, a_s, u_s, h_s, carry_s) = refs
    t = pl.program_id(1)
    ci = seqs.nch - 1 - t if reverse else t
    first = seqs.is_first(ci)
    last = seqs.is_last(ci)

    x = x_ref[...].astype(F32)
    prev = jnp.where(first, 0.0, xp_ref[...].astype(F32))
    nxt = jnp.where(last, 0.0, xn_ref[...].astype(F32))
    row8 = lax.broadcasted_iota(I32, (8, 1), 0)

    def patch(arr, at, rows8):
        parts = ([arr[:at]] if at > 0 else []) + [rows8] + ([arr[at + 8:]] if at + 8 < c else [])
        return jnp.concatenate(parts, axis=0)

    xm1 = pltpu.roll(x, 1, 0)
    xm1 = patch(xm1, 0, jnp.where(row8 == 0, prev[HALO - 1:HALO], xm1[0:8]))
    xm2 = pltpu.roll(x, 2, 0)
    xm2 = patch(xm2, 0, jnp.where(row8 == 0, prev[HALO - 2:HALO - 1],
                                  jnp.where(row8 == 1, prev[HALO - 1:HALO], xm2[0:8])))
    xp1 = pltpu.roll(x, c - 1, 0)
    xp1 = patch(xp1, c - 8, jnp.where(row8 == 7, nxt[0:1], xp1[c - 8:c]))
    cw = cw_ref[...]
    xc = cw[0:1] * xm2 + cw[1:2] * xm1 + cw[2:3] * x + cw[3:4] * xp1 + cbias_ref[...]

    nslab = cb // LRU_BLOCK
    for gi in range(nslab):
        sl = slice(gi * LRU_BLOCK, (gi + 1) * LRU_BLOCK)
        xs = xc[:, sl]
        xb = xs.astype(BF16)
        r = jax.nn.sigmoid(jnp.dot(xb, wa_ref[gi], preferred_element_type=F32) + ba_ref[:, sl])
        ig = jax.nn.sigmoid(jnp.dot(xb, wx_ref[gi], preferred_element_type=F32) + bx_ref[:, sl])
        a = jnp.exp(c8_ref[:, sl] * r)
        a_s[gi] = a
        u_s[gi] = jnp.sqrt(1.0 - a * a) * (ig * xs)

    @pl.when(last if reverse else first)
    def _():
        carry_s[...] = jnp.zeros_like(carry_s)

    sub = lax.broadcasted_iota(I32, (8, LRU_BLOCK), 0)
    nsuper = c // LRU_SUPER
    steps = list(range(LRU_SEG))
    if reverse:
        steps = steps[::-1]

    def super_group(q, carries):
        qi = nsuper - 1 - q if reverse else q
        base = qi * LRU_SUPER
        out = []
        for gi in range(nslab):
            hs, ps = {}, {}
            h = p = None
            for j in steps:
                rows = pl.ds(base + j, 8, stride=LRU_SEG)
                a = a_s[gi, rows, :]
                u = u_s[gi, rows, :]
                h = u if h is None else a * h + u
                p = a if p is None else a * p
                hs[j], ps[j] = h, p
            eh, ep = h, p
            for s in (1, 2, 4):
                shift = 8 - s if reverse else s
                m = (sub < 8 - s) if reverse else (sub >= s)
                eh_sh = pltpu.roll(eh, shift, 0)
                ep_sh = pltpu.roll(ep, shift, 0)
                eh = eh + ep * jnp.where(m, eh_sh, 0.0)
                ep = ep * jnp.where(m, ep_sh, 1.0)
            end = eh + ep * carries[gi]
            if reverse:
                enter = jnp.where(sub == 7, carries[gi], pltpu.roll(end, 7, 0))
                out.append(jnp.broadcast_to(end[0:1], (8, LRU_BLOCK)))
            else:
                enter = jnp.where(sub == 0, carries[gi], pltpu.roll(end, 1, 0))
                out.append(jnp.broadcast_to(end[7:8], (8, LRU_BLOCK)))
            for j in steps:
                h_s[gi, pl.ds(base + j, 8, stride=LRU_SEG), :] = hs[j] + ps[j] * enter
        return tuple(out)

    carries = lax.fori_loop(0, nsuper, super_group, tuple(carry_s[gi] for gi in range(nslab)))
    for gi in range(nslab):
        carry_s[gi] = carries[gi]

    for gi in range(nslab):
        sl = slice(gi * LRU_BLOCK, (gi + 1) * LRU_BLOCK)
        if reverse:
            g = g_ref[:, sl].astype(F32)
            o_ref[:, sl] = ((hf_ref[:, sl] + h_s[gi]) * jax.nn.gelu(g, approximate=True)).astype(o_ref.dtype)
        else:
            o_ref[:, sl] = h_s[gi]


def _lru_call(z, conv_w, conv_b, wa, ba, wx, bx, c8, hf, direction, seqs, c, cb):
    t = z.shape[0]
    nch = seqs.nch
    reverse = direction == 1
    tmap = (lambda ti: nch - 1 - ti) if reverse else (lambda ti: ti)
    xcol = (2 * QK_W + 2 * V_W) // cb
    gcol = (2 * QK_W + 2 * V_W + D) // cb
    hb = c // HALO
    nhb = t // HALO
    nb = cb // LRU_BLOCK
    in_specs = [
        pl.BlockSpec((c, cb), lambda ch, ti: (tmap(ti), xcol + ch)),
        pl.BlockSpec((HALO, cb), lambda ch, ti: (jnp.maximum(tmap(ti) * hb - 1, 0), xcol + ch)),
        pl.BlockSpec((HALO, cb), lambda ch, ti: (jnp.minimum((tmap(ti) + 1) * hb, nhb - 1), xcol + ch)),
        pl.BlockSpec((4, cb), lambda ch, ti: (0, ch)),
        pl.BlockSpec((1, cb), lambda ch, ti: (0, ch)),
        pl.BlockSpec((None, nb, LRU_BLOCK, LRU_BLOCK), lambda ch, ti: (direction, ch, 0, 0)),
        pl.BlockSpec((None, 1, cb), lambda ch, ti: (direction, 0, ch)),
        pl.BlockSpec((None, nb, LRU_BLOCK, LRU_BLOCK), lambda ch, ti: (direction, ch, 0, 0)),
        pl.BlockSpec((None, 1, cb), lambda ch, ti: (direction, 0, ch)),
        pl.BlockSpec((None, 1, cb), lambda ch, ti: (direction, 0, ch)),
    ]
    args = [z, z, z, conv_w, conv_b, wa, ba, wx, bx, c8]
    if reverse:
        in_specs += [pl.BlockSpec((c, cb), lambda ch, ti: (tmap(ti), ch)),
                     pl.BlockSpec((c, cb), lambda ch, ti: (tmap(ti), gcol + ch))]
        args += [hf, z]
    return pl.pallas_call(
        functools.partial(_lru_kernel, reverse=reverse, seqs=seqs, c=c, cb=cb),
        grid=(D // cb, nch),
        in_specs=in_specs,
        out_specs=pl.BlockSpec((c, cb), lambda ch, ti: (tmap(ti), ch)),
        out_shape=jax.ShapeDtypeStruct((t, D), BF16 if reverse else F32),
        scratch_shapes=[pltpu.VMEM((nb, c, LRU_BLOCK), F32), pltpu.VMEM((nb, c, LRU_BLOCK), F32),
                        pltpu.VMEM((nb, c, LRU_BLOCK), F32), pltpu.VMEM((nb, 8, LRU_BLOCK), F32)],
        compiler_params=_cp(("arbitrary", "arbitrary"),32),
        name="lru_rev" if reverse else "lru_fwd",
    )(*args)


def _proj_merge_kernel(yr_ref, yl_ref, wr_ref, wl_ref, mr_ref, ml_ref, o_ref):
    r = jnp.dot(yr_ref[...], wr_ref[...], preferred_element_type=F32)
    l = jnp.dot(yl_ref[...], wl_ref[...], preferred_element_type=F32)
    o_ref[...] = (jax.nn.sigmoid(mr_ref[...].astype(F32)) * r
                  + jax.nn.sigmoid(ml_ref[...].astype(F32)) * l).astype(o_ref.dtype)


def _proj_merge(y_ret, y_lru, w_ret_o, w_lru_o, z, tm=512, tn=1024):
    t = y_ret.shape[0]
    mcol = (2 * QK_W + 2 * V_W + 2 * D) // tn
    return pl.pallas_call(
        _proj_merge_kernel,
        grid=(D // tn, t // tm),
        in_specs=[pl.BlockSpec((tm, V_W), lambda j, i: (i, 0)),
                  pl.BlockSpec((tm, D), lambda j, i: (i, 0)),
                  pl.BlockSpec((V_W, tn), lambda j, i: (0, j)),
                  pl.BlockSpec((D, tn), lambda j, i: (0, j)),
                  pl.BlockSpec((tm, tn), lambda j, i: (i, mcol + j)),
                  pl.BlockSpec((tm, tn), lambda j, i: (i, mcol + D // tn + j))],
        out_specs=pl.BlockSpec((tm, tn), lambda j, i: (i, j)),
        out_shape=jax.ShapeDtypeStruct((t, D), BF16),
        compiler_params=_cp(("arbitrary", "arbitrary"), 48),
        name="proj_merge",
    )(y_ret, y_lru, w_ret_o, w_lru_o, z, z)


def _proj_out_kernel(m_ref, w_ref, xp_ref, xs_ref, o_ref, *, npt):
    i = pl.program_id(1)
    y = jnp.dot(m_ref[...], w_ref[...], preferred_element_type=F32)

    @pl.when(i < npt)
    def _():
        o_ref[...] = xp_ref[...] + y

    @pl.when(i >= npt)
    def _():
        o_ref[...] = xs_ref[...] + y


def _proj_out(merged, w_out, xp, xs, tm=512, tn=1024):
    t = merged.shape[0]
    npt = xp.shape[0] // tm
    return pl.pallas_call(
        functools.partial(_proj_out_kernel, npt=npt),
        grid=(D // tn, t // tm),
        in_specs=[pl.BlockSpec((tm, D), lambda j, i: (i, 0)),
                  pl.BlockSpec((D, tn), lambda j, i: (0, j)),
                  pl.BlockSpec((tm, tn), lambda j, i: (jnp.minimum(i, npt - 1), j)),
                  pl.BlockSpec((tm, tn), lambda j, i: (jnp.maximum(i - npt, 0), j))],
        out_specs=pl.BlockSpec((tm, tn), lambda j, i: (i, j)),
        out_shape=jax.ShapeDtypeStruct((t, D), F32),
        compiler_params=_cp(("arbitrary", "arbitrary"), 48),
        name="proj_out",
    )(merged, w_out, xp, xs)


def _router_kernel(x_ref, g_ref, whi_ref, wlo_ref, b_ref, xn_ref, tpos_ref, tpos_t_ref, gate_ref, meta_ref, tri_ref,
                   run_ref, *, tm):
    i = pl.program_id(0)

    @pl.when(i == 0)
    def _():
        r = lax.broadcasted_iota(I32, (tm, tm), 0)
        col = lax.broadcasted_iota(I32, (tm, tm), 1)
        tri_ref[...] = (col < r).astype(BF16)
        run_ref[...] = jnp.zeros_like(run_ref)

    x = x_ref[...]
    xn = x * lax.rsqrt(jnp.mean(x * x, axis=-1, keepdims=True) + EPS) * g_ref[...]
    xh = xn.astype(BF16)
    xn_ref[...] = xh

    xl = (xn - xh.astype(F32)).astype(BF16)
    logits = (jnp.dot(xh, whi_ref[...], preferred_element_type=F32)
              + (jnp.dot(xl, whi_ref[...], preferred_element_type=F32)
                 + jnp.dot(xh, wlo_ref[...], preferred_element_type=F32))) + b_ref[...]
    lane = lax.broadcasted_iota(I32, (tm, LANE), 1)
    lane_f = lane.astype(F32)
    vals, idxs = [], []
    cur = logits
    for _ in range(TOP_K):
        m = jnp.max(cur, axis=-1, keepdims=True)
        idx = jnp.min(jnp.where(cur == m, lane_f, float(LANE)), axis=-1, keepdims=True).astype(I32)
        vals.append(m)
        idxs.append(idx)
        cur = jnp.where(lane == idx, -jnp.inf, cur)
    exps = [jnp.exp(v - vals[0]) for v in vals]
    denom = exps[0] + exps[1] + exps[2] + exps[3]

    onehot = jnp.zeros((tm, LANE), F32)
    for idx in idxs:
        onehot = onehot + (lane == idx).astype(F32)
    before = jnp.dot(tri_ref[...], onehot.astype(BF16), preferred_element_type=F32)

    cnt = jnp.broadcast_to(jnp.sum(onehot, axis=0, keepdims=True), (8, LANE))
    cnt8 = jnp.floor((cnt + (SEG_ALIGN - 1)) * (1.0 / SEG_ALIGN)) * SEG_ALIGN
    lane8 = lax.broadcasted_iota(I32, (8, LANE), 1)
    incl = cnt8
    s = 1
    while s < LANE:
        incl = incl + jnp.where(lane8 >= s, pltpu.roll(incl, s, 1), 0.0)
        s *= 2
    toff = incl - cnt8

    pos = before + toff[0:1]
    tpos_out = jnp.zeros((tm, LANE), F32)
    gate_out = jnp.zeros((tm, LANE), F32)
    for k in range(TOP_K):
        tpos_k = jnp.sum(jnp.where(lane == idxs[k], pos, 0.0), axis=-1, keepdims=True)
        tpos_out = jnp.where(lane == k, tpos_k, tpos_out)
        gate_out = jnp.where(lane == k, exps[k] / denom, gate_out)
    tpos_ref[...] = tpos_out.astype(I32)
    tpos_t_ref[...] = tpos_out.T[0:8].astype(I32)
    gate_ref[...] = gate_out

    sub8 = lax.broadcasted_iota(I32, (8, LANE), 0)
    run = jnp.broadcast_to(run_ref[...], (8, LANE))
    meta_ref[...] = jnp.where(sub8 == 0, cnt8, jnp.where(sub8 == 1, toff, jnp.where(sub8 == 2, run, 0.0)))
    run_ref[...] = run_ref[...] + cnt8[0:1]


def _router(x1, g, w_hi, w_lo, b_pad, tm):
    t = x1.shape[0]
    nt = t // tm
    tok = lambda i: (i, 0)
    fixed = lambda i: (0, 0)
    per_tile = lambda i: (i, 0, 0)
    return pl.pallas_call(
        functools.partial(_router_kernel, tm=tm),
        grid=(nt,),
        in_specs=[pl.BlockSpec((tm, D), tok), pl.BlockSpec((1, D), fixed),
                  pl.BlockSpec((D, LANE), fixed), pl.BlockSpec((D, LANE), fixed), pl.BlockSpec((1, LANE), fixed)],
        out_specs=[pl.BlockSpec((tm, D), tok), pl.BlockSpec((tm, LANE), tok),
                   pl.BlockSpec((None, 8, tm), per_tile), pl.BlockSpec((tm, LANE), tok),
                   pl.BlockSpec((None, 8, LANE), per_tile)],
        out_shape=[jax.ShapeDtypeStruct((t, D), BF16), jax.ShapeDtypeStruct((t, LANE), I32),
                   jax.ShapeDtypeStruct((nt, 8, tm), I32), jax.ShapeDtypeStruct((t, LANE), F32),
                   jax.ShapeDtypeStruct((nt, 8, LANE), F32)],
        scratch_shapes=[pltpu.VMEM((tm, tm), BF16), pltpu.VMEM((1, LANE), F32)],
        compiler_params=_cp(("arbitrary",), 32),
        name="router",
    )(x1, g, w_hi, w_lo, b_pad)


def _segment_copies(i, toff_ref, c8_ref, make_copy, act):
    def per_expert(e, carry):
        base = i * N_EXP + e
        n = c8_ref[base] // SEG_ALIGN
        t0 = toff_ref[base]
        off = 0
        for b in range(SEG_MAXBIT, -1, -1):
            rows = SEG_ALIGN << b
            bit = (n >> b) & 1

            @pl.when(bit == 1)
            def _(off=off, rows=rows):
                act(make_copy(pl.multiple_of(t0 + off, SEG_ALIGN), off, e, rows))

            off = off + bit * rows
        return carry

    lax.fori_loop(0, N_EXP, per_expert, 0)


def _dispatch_kernel(toff_ref, c8_ref, dst_ref, zblk_ref, tpos_t_ref, x_ref, xg_ref, sorted_ref, zero_ref, sem):
    i = pl.program_id(0)

    def zero_copy(e):
        start = pl.multiple_of(jnp.maximum(zblk_ref[e], 0), MOE_ROWS)
        return pltpu.make_async_copy(zero_ref, xg_ref.at[pl.ds(start, MOE_ROWS), :], sem)

    def for_nonempty(act):
        def body(e, carry):
            pl.when(zblk_ref[e] >= 0)(lambda: act(zero_copy(e)))
            return carry
        lax.fori_loop(0, N_EXP, body, 0)

    @pl.when(i == 0)
    def _():
        zero_ref[...] = jnp.zeros_like(zero_ref)
        for_nonempty(lambda cp: cp.start())
        for_nonempty(lambda cp: cp.wait())

    x = x_ref[...]
    for c in range(SORT_ROWS // PERM_CHUNK):
        p = c * PERM_CHUNK + lax.broadcasted_iota(I32, (PERM_CHUNK, 1), 0)
        perm = jnp.zeros((PERM_CHUNK, x.shape[0]), F32)
        for k in range(TOP_K):
            perm = perm + jnp.where(p == tpos_t_ref[k:k + 1, :], 1.0, 0.0)
        rows = jnp.dot(perm.astype(BF16), x, preferred_element_type=F32)
        sorted_ref[c * PERM_CHUNK:(c + 1) * PERM_CHUNK, :] = _pack_pair(rows[:, :HALF], rows[:, HALF:], rounded=True)

    def make_copy(t_off, s_off, e, rows):
        d = pl.multiple_of(dst_ref[i * N_EXP + e] + s_off, SEG_ALIGN)
        return pltpu.make_async_copy(sorted_ref.at[pl.ds(t_off, rows), :], xg_ref.at[pl.ds(d, rows), :], sem)

    _segment_copies(i, toff_ref, c8_ref, make_copy, lambda cp: cp.start())
    _segment_copies(i, toff_ref, c8_ref, make_copy, lambda cp: cp.wait())


def _dispatch(toff, c8, dst, zblk, tpos_t, xn, rows):
    nt, _, tm = tpos_t.shape
    grid_spec = pltpu.PrefetchScalarGridSpec(
        num_scalar_prefetch=4,
        grid=(nt,),
        in_specs=[pl.BlockSpec((None, 8, tm), lambda i, *_: (i, 0, 0)),
                  pl.BlockSpec((tm, D), lambda i, *_: (i, 0))],
        out_specs=pl.BlockSpec(memory_space=pl.ANY),
        scratch_shapes=[pltpu.VMEM((SORT_ROWS, HALF), U32), pltpu.VMEM((MOE_ROWS, HALF), U32),
                        pltpu.SemaphoreType.DMA(())],
    )
    return pl.pallas_call(
        _dispatch_kernel,
        grid_spec=grid_spec,
        out_shape=jax.ShapeDtypeStruct((rows, HALF), U32),
        compiler_params=_cp(("arbitrary",), 40),
        name="dispatch",
    )(toff, c8, dst, zblk, tpos_t, xn)


def _expert_changed(be_ref, i):
    return (i == 0) | (be_ref[i] != be_ref[jnp.maximum(i - 1, 0)])


def _expert_up_kernel(be_ref, nu_ref, xg_ref, wg_ref, bg_ref, wu_ref, bu_ref, h_ref, wgb_ref, wub_ref):
    i = pl.program_id(1)

    @pl.when(i < nu_ref[0])
    def _():
        @pl.when(_expert_changed(be_ref, i))
        def _():
            wgb_ref[...] = wg_ref[...].astype(BF16)
            wub_ref[...] = wu_ref[...].astype(BF16)

        lo, hi = _unpack_pair(xg_ref[...])
        xb = jnp.concatenate([lo.astype(BF16), hi.astype(BF16)], axis=-1)
        gate = jnp.minimum(jnp.dot(xb, wgb_ref[...], preferred_element_type=F32) + bg_ref[...], SWIGLU_LIMIT)
        up = jnp.clip(jnp.dot(xb, wub_ref[...], preferred_element_type=F32) + bu_ref[...],
                      -SWIGLU_LIMIT, SWIGLU_LIMIT)
        h_ref[...] = ((up + 1.0) * gate * jax.nn.sigmoid(SWIGLU_ALPHA * gate)).astype(h_ref.dtype)


def _expert_down_kernel(be_ref, nu_ref, h_ref, wlo_ref, whi_ref, blo_ref, bhi_ref, o_ref, wlob_ref, whib_ref):
    i = pl.program_id(1)

    @pl.when(i < nu_ref[0])
    def _():
        @pl.when(_expert_changed(be_ref, i))
        def _():
            wlob_ref[...] = wlo_ref[...].astype(BF16)
            whib_ref[...] = whi_ref[...].astype(BF16)

        h = h_ref[...]
        lo = jnp.dot(h, wlob_ref[...], preferred_element_type=F32) + blo_ref[...]
        hi = jnp.dot(h, whib_ref[...], preferred_element_type=F32) + bhi_ref[...]
        o_ref[...] = _pack_pair(lo, hi)


def _experts(block_expert, n_used, xg, wg, bg, wu, bu, wd, bd, rows_blk, tf, tn):
    rows = xg.shape[0]
    nb = rows // rows_blk
    nj = D_FF // tf
    nn = HALF // tn

    def blk(i, nu):
        return jnp.minimum(i, nu[0] - 1)

    up_spec = pltpu.PrefetchScalarGridSpec(
        num_scalar_prefetch=2,
        grid=(nj, nb),
        in_specs=[pl.BlockSpec((rows_blk, HALF), lambda j, i, be, nu: (blk(i, nu), 0)),
                  pl.BlockSpec((None, D, tf), lambda j, i, be, nu: (be[i], 0, j)),
                  pl.BlockSpec((None, 1, tf), lambda j, i, be, nu: (be[i], 0, j)),
                  pl.BlockSpec((None, D, tf), lambda j, i, be, nu: (be[i], 0, j)),
                  pl.BlockSpec((None, 1, tf), lambda j, i, be, nu: (be[i], 0, j))],
        out_specs=pl.BlockSpec((rows_blk, tf), lambda j, i, be, nu: (blk(i, nu), j)),
        scratch_shapes=[pltpu.VMEM((D, tf), BF16), pltpu.VMEM((D, tf), BF16)],
    )
    hid = pl.pallas_call(
        _expert_up_kernel,
        grid_spec=up_spec,
        out_shape=jax.ShapeDtypeStruct((rows, D_FF), BF16),
        compiler_params=_cp(("arbitrary", "arbitrary"), 60),
        name="experts_up",
    )(block_expert, n_used, xg, wg, bg, wu, bu)

    down_spec = pltpu.PrefetchScalarGridSpec(
        num_scalar_prefetch=2,
        grid=(nn, nb),
        in_specs=[pl.BlockSpec((rows_blk, D_FF), lambda n, i, be, nu: (blk(i, nu), 0)),
                  pl.BlockSpec((None, D_FF, tn), lambda n, i, be, nu: (be[i], 0, n)),
                  pl.BlockSpec((None, D_FF, tn), lambda n, i, be, nu: (be[i], 0, nn + n)),
                  pl.BlockSpec((None, 1, tn), lambda n, i, be, nu: (be[i], 0, n)),
                  pl.BlockSpec((None, 1, tn), lambda n, i, be, nu: (be[i], 0, nn + n))],
        out_specs=pl.BlockSpec((rows_blk, tn), lambda n, i, be, nu: (blk(i, nu), n)),
        scratch_shapes=[pltpu.VMEM((D_FF, tn), BF16), pltpu.VMEM((D_FF, tn), BF16)],
    )
    return pl.pallas_call(
        _expert_down_kernel,
        grid_spec=down_spec,
        out_shape=jax.ShapeDtypeStruct((rows, HALF), U32),
        compiler_params=_cp(("arbitrary", "arbitrary"), 48),
        name="experts_down",
    )(block_expert, n_used, hid, wd, wd, bd, bd)


def _combine_kernel(toff_ref, c8_ref, dst_ref, os_ref, tpos_ref, gate_ref, x1_ref, fg_ref, op_ref, osm_ref,
                    sorted_ref, sem, *, npt, nt):
    i = pl.program_id(0)
    slot = i % 2

    def copies(tile, buf, act):
        def make_copy(t_off, s_off, e, rows):
            d = pl.multiple_of(dst_ref[tile * N_EXP + e] + s_off, SEG_ALIGN)
            return pltpu.make_async_copy(os_ref.at[pl.ds(d, rows), :], sorted_ref.at[buf, pl.ds(t_off, rows), :],
                                         sem.at[buf])
        _segment_copies(tile, toff_ref, c8_ref, make_copy, act)

    pl.when(i == 0)(lambda: copies(0, 0, lambda cp: cp.start()))
    pl.when(i + 1 < nt)(lambda: copies(i + 1, 1 - slot, lambda cp: cp.start()))
    copies(i, slot, lambda cp: cp.wait())

    last = i * N_EXP + N_EXP - 1
    used = toff_ref[last] + c8_ref[last]
    tpos = tpos_ref[...]
    gates = gate_ref[...]
    y_lo = x1_ref[:, :HALF]
    y_hi = x1_ref[:, HALF:]
    for c in range(SORT_ROWS // PERM_CHUNK):
        p_lane = c * PERM_CHUNK + lax.broadcasted_iota(I32, (1, PERM_CHUNK), 1)
        place = jnp.zeros((tpos.shape[0], PERM_CHUNK), F32)
        for k in range(TOP_K):
            place = place + jnp.where(tpos[:, k:k + 1] == p_lane, gates[:, k:k + 1], 0.0)
        p_row = c * PERM_CHUNK + lax.broadcasted_iota(I32, (PERM_CHUNK, 1), 0)
        u = jnp.where(p_row < used, sorted_ref[slot, c * PERM_CHUNK:(c + 1) * PERM_CHUNK, :], jnp.uint32(0))
        lo, hi = _unpack_pair(u)
        place = place.astype(BF16)
        y_lo = y_lo + jnp.dot(place, lo.astype(BF16), preferred_element_type=F32)
        y_hi = y_hi + jnp.dot(place, hi.astype(BF16), preferred_element_type=F32)
    ms = (jnp.sum(y_lo * y_lo, axis=-1, keepdims=True) + jnp.sum(y_hi * y_hi, axis=-1, keepdims=True)) / D
    inv = lax.rsqrt(ms + EPS)
    out = jnp.concatenate([y_lo * inv * fg_ref[:, :HALF], y_hi * inv * fg_ref[:, HALF:]], axis=-1)

    @pl.when(i < npt)
    def _():
        op_ref[...] = out

    @pl.when(i >= npt)
    def _():
        osm_ref[...] = out


def _combine(toff, c8, dst, out_sorted, tpos, gates, x1, fg, tp, tm):
    t = x1.shape[0]
    npt = tp // tm
    tok = lambda i, *_: (i, 0)
    grid_spec = pltpu.PrefetchScalarGridSpec(
        num_scalar_prefetch=3,
        grid=(t // tm,),
        in_specs=[pl.BlockSpec(memory_space=pl.ANY),
                  pl.BlockSpec((tm, LANE), tok), pl.BlockSpec((tm, LANE), tok), pl.BlockSpec((tm, D), tok),
                  pl.BlockSpec((1, D), lambda i, *_: (0, 0))],
        out_specs=[pl.BlockSpec((tm, D), lambda i, *_: (jnp.minimum(i, npt - 1), 0)),
                   pl.BlockSpec((tm, D), lambda i, *_: (jnp.maximum(i - npt, 0), 0))],
        scratch_shapes=[pltpu.VMEM((2, SORT_ROWS, HALF), U32), pltpu.SemaphoreType.DMA((2,))],
    )
    return pl.pallas_call(
        functools.partial(_combine_kernel, npt=npt, nt=t // tm),
        grid_spec=grid_spec,
        out_shape=[jax.ShapeDtypeStruct((tp, D), F32), jax.ShapeDtypeStruct((t - tp, D), F32)],
        compiler_params=_cp(("arbitrary",), 56),
        name="combine",
    )(toff, c8, dst, out_sorted, tpos, gates, x1, fg)


def _rope_tables(n_prompt, prompt_len, sample_len):
    half = DK // 2
    inv = ROPE_BASE ** (-jnp.arange(half, dtype=F32) / half)
    pos = jnp.concatenate([jnp.tile(jnp.arange(prompt_len, dtype=F32), n_prompt),
                           jnp.arange(sample_len, dtype=F32)])
    ang = pos[:, None] * inv[None, :]
    cos, sin = jnp.cos(ang), jnp.sin(ang)
    return jnp.concatenate([cos, cos], axis=-1), jnp.concatenate([-sin, sin], axis=-1)


def _layer(xp, xs, n_prompt, prompt_len, sample_len, mix_norm_g, w_in, ret_norm_g, w_ret_o, conv_w, conv_b, lru_w_a,
           lru_b_a, lru_w_x, lru_b_x, lru_lambda, w_lru_o, w_out, moe_norm_g, w_router, b_router, w_e_gate,
           b_e_gate, w_e_up, b_e_up, w_e_down, b_e_down, final_norm_g):
    tp = xp.shape[0]
    t = tp + xs.shape[0]
    seqs = _Seqs(n_prompt, prompt_len, sample_len, SEQ_CHUNK)
    row = lambda v: v.reshape(1, -1)

    z = _inproj(xp, xs, row(mix_norm_g), w_in.astype(BF16))

    cosf, sinf = _rope_tables(n_prompt, prompt_len, sample_len)
    y_ret = _retention(z, cosf, sinf, row(ret_norm_g), seqs, SEQ_CHUNK)

    wa, wx = lru_w_a.astype(BF16), lru_w_x.astype(BF16)
    ba, bx = lru_b_a[:, None, :], lru_b_x[:, None, :]
    c8 = (-LRU_C * jax.nn.softplus(-lru_lambda))[:, None, :]
    lru_args = (z, conv_w, row(conv_b), wa, ba, wx, bx, c8)
    lru_seqs = _Seqs(n_prompt, prompt_len, sample_len, LRU_CHUNK)
    h_fwd = _lru_call(*lru_args, None, 0, lru_seqs, LRU_CHUNK, LRU_CB)
    y_lru = _lru_call(*lru_args, h_fwd, 1, lru_seqs, LRU_CHUNK, LRU_CB)

    merged = _proj_merge(y_ret, y_lru, w_ret_o.astype(BF16), w_lru_o.astype(BF16), z)
    x1 = _proj_out(merged, w_out.astype(BF16), xp, xs)

    w_pad = jnp.zeros((D, LANE), F32).at[:, :N_EXP].set(w_router)
    b_pad = jnp.full((1, LANE), -1e30, F32).at[0, :N_EXP].set(b_router)
    w_hi = w_pad.astype(BF16)
    w_lo = (w_pad - w_hi.astype(F32)).astype(BF16)
    xn2, tpos, tpos_t, gates, meta = _router(x1, row(moe_norm_g), w_hi, w_lo, b_pad, TOK_TILE)

    nt = t // TOK_TILE
    c8 = meta[:, 0, :N_EXP].astype(I32)
    toff = meta[:, 1, :N_EXP].astype(I32)
    run = meta[:, 2, :N_EXP].astype(I32)
    total = run[-1] + c8[-1]
    padded = (total + MOE_ROWS - 1) // MOE_ROWS * MOE_ROWS
    pad_ends = jnp.cumsum(padded)
    pad_starts = pad_ends - padded
    dst = (pad_starts[None, :] + run).reshape(-1)
    zblk = jnp.where(padded > 0, pad_ends - MOE_ROWS, -1)
    n_blocks = (t * TOP_K + (SEG_ALIGN - 1) * nt * N_EXP) // MOE_ROWS + 1 + N_EXP
    n_used = (pad_ends[-1] // MOE_ROWS).astype(I32)
    blk_ids = jnp.minimum(jnp.arange(n_blocks, dtype=I32), n_used - 1)
    block_expert = jnp.minimum(
        jnp.sum((pad_ends[None, :] <= (blk_ids * MOE_ROWS)[:, None]).astype(I32), axis=-1), N_EXP - 1)
    toff, c8 = toff.reshape(-1), c8.reshape(-1)

    xg = _dispatch(toff, c8, dst, zblk, tpos_t, xn2, n_blocks * MOE_ROWS)
    out_sorted = _experts(block_expert, n_used.reshape(1), xg,
                          w_e_gate, b_e_gate[:, None, :], w_e_up, b_e_up[:, None, :],
                          w_e_down, b_e_down[:, None, :], MOE_ROWS, MOE_TF, MOE_TN)
    return _combine(toff, c8, dst, out_sorted, tpos, gates, x1, row(final_norm_g), tp, TOK_TILE)


def kernel(x_prompt, x_sample, mix_norm_g, w_in, ret_norm_g, w_ret_o, conv_w, conv_b, lru_w_a, lru_b_a, lru_w_x,
           lru_b_x, lru_lambda, w_lru_o, w_out, moe_norm_g, w_router, b_router, w_e_gate, b_e_gate, w_e_up, b_e_up,
           w_e_down, b_e_down, final_norm_g):
    assert mix_norm_g.shape[0] == 1, "one layer"
    n_prompt, prompt_len, _ = x_prompt.shape
    n_sample, sample_len, _ = x_sample.shape
    assert n_sample == 1
    tp = n_prompt * prompt_len
    yp, ys = _layer(x_prompt.reshape(tp, D), x_sample.reshape(sample_len, D), n_prompt, prompt_len, sample_len,
                    mix_norm_g[0], w_in[0], ret_norm_g[0], w_ret_o[0], conv_w[0],
                    conv_b[0], lru_w_a[0], lru_b_a[0], lru_w_x[0], lru_b_x[0], lru_lambda[0], w_lru_o[0], w_out[0],
                    moe_norm_g[0], w_router[0], b_router[0], w_e_gate[0], b_e_gate[0], w_e_up[0], b_e_up[0],
                    w_e_down[0], b_e_down[0], final_norm_g)
    return yp.reshape(x_prompt.shape), ys.reshape(x_sample.shape)
```

```python
import functools
import math

import jax
import jax.numpy as jnp
from jax import lax
from jax.experimental import pallas as pl
from jax.experimental.pallas import tpu as pltpu

F32 = jnp.float32
BF16 = jnp.bfloat16
U32 = jnp.uint32
I32 = jnp.int32

D = 2048
HEADS = 8
DK = 128
DV = 256
QK_W = HEADS * DK
V_W = HEADS * DV
LRU_BLOCK = 128
N_EXP = 32
TOP_K = 4
D_FF = 2048
SWIGLU_LIMIT = 7.0
SWIGLU_ALPHA = 1.702
ROPE_BASE = 10000.0
LRU_C = 8.0
EPS = 1e-6
IN_W = 2 * QK_W + 2 * V_W + 2 * D + 2 * D

HALF = D // 2
LANE = 128
HALO = 16
HI_MASK = 0xFFFF0000

SEQ_CHUNK = 256
LRU_CHUNK = 512
LRU_CB = 512
LRU_SEG = 4
LRU_SUPER = 8 * LRU_SEG
MOE_ROWS = 512
MOE_TF = 1024
MOE_TN = 1024
TOK_TILE = 512
SEG_ALIGN = 8
PERM_CHUNK = 256
SORT_ROWS = TOK_TILE * 4 + 32 * SEG_ALIGN
SEG_MAXBIT = (TOK_TILE // SEG_ALIGN).bit_length() - 1


def _cp(sem, vmem_mb):
    return pltpu.CompilerParams(dimension_semantics=sem, vmem_limit_bytes=vmem_mb << 20)


def _bits(x):
    return lax.bitcast_convert_type(x, U32)


def _pack_pair(lo, hi, rounded=False):
    if not rounded:
        lo = lo.astype(BF16).astype(F32)
        hi = hi.astype(BF16).astype(F32)
    return (_bits(lo) >> 16) | (_bits(hi) & jnp.uint32(HI_MASK))


def _unpack_pair(u):
    lo = lax.bitcast_convert_type(u << 16, F32)
    hi = lax.bitcast_convert_type(u & jnp.uint32(HI_MASK), F32)
    return lo, hi


class _Seqs:
    def __init__(self, n_prompt, prompt_len, sample_len, chunk):
        assert prompt_len % chunk == 0 and sample_len % chunk == 0
        self.cps = prompt_len // chunk
        self.npc = n_prompt * self.cps
        self.nch = self.npc + sample_len // chunk

    def is_first(self, ci):
        return ((ci < self.npc) & (ci % self.cps == 0)) | (ci == self.npc)

    def is_last(self, ci):
        return ((ci < self.npc) & (ci % self.cps == self.cps - 1)) | (ci == self.nch - 1)


def _two_group_specs(tm, npt, width=D):
    return [pl.BlockSpec((tm, width), lambda i, j: (jnp.minimum(i, npt - 1), 0)),
            pl.BlockSpec((tm, width), lambda i, j: (jnp.maximum(i - npt, 0), 0))]


def _inproj_kernel(xp_ref, xs_ref, g_ref, w_ref, o_ref, xn_ref, *, npt):
    i = pl.program_id(0)

    def norm(x_ref):
        x = x_ref[...]
        y = x * lax.rsqrt(jnp.mean(x * x, axis=-1, keepdims=True) + EPS)
        xn_ref[...] = (y * g_ref[...]).astype(BF16)

    @pl.when(pl.program_id(1) == 0)
    def _():
        pl.when(i < npt)(lambda: norm(xp_ref))
        pl.when(i >= npt)(lambda: norm(xs_ref))

    o_ref[...] = jnp.dot(xn_ref[...], w_ref[...], preferred_element_type=F32).astype(o_ref.dtype)


def _inproj(xp, xs, g, w, tm=1024, tn=1024):
    t, n = xp.shape[0] + xs.shape[0], w.shape[1]
    npt = xp.shape[0] // tm
    return pl.pallas_call(
        functools.partial(_inproj_kernel, npt=npt),
        grid=(t // tm, n // tn),
        in_specs=_two_group_specs(tm, npt) + [pl.BlockSpec((1, D), lambda i, j: (0, 0)),
                                              pl.BlockSpec((D, tn), lambda i, j: (0, j))],
        out_specs=pl.BlockSpec((tm, tn), lambda i, j: (i, j)),
        out_shape=jax.ShapeDtypeStruct((t, n), BF16),
        scratch_shapes=[pltpu.VMEM((tm, D), BF16)],
        compiler_params=_cp(("arbitrary", "arbitrary"), 56),
        name="inproj",
    )(xp, xs, g, w)


def _rot(x, cosf, sinf):
    return x * cosf + pltpu.roll(x, DK // 2, 1) * sinf


def _log_gamma(h):
    return math.log1p(-(2.0 ** (-5 - h)))


def _decay_tables(dec_ref, c, q_exponent, k_exponent):
    p = lax.broadcasted_iota(I32, (c, DK), 0).astype(F32)
    for h in range(HEADS):
        lg = _log_gamma(h)
        dec_ref[h, 0] = jnp.exp(lg * q_exponent(p))
        dec_ref[h, 1] = jnp.exp(lg * k_exponent(p))


def _ret_bwd_kernel(q_ref, k_ref, v_ref, cos_ref, sin_ref, o_ref, s_ref, dec_ref, *, seqs, c):
    ci = seqs.nch - 1 - pl.program_id(0)

    @pl.when(pl.program_id(0) == 0)
    def _():
        _decay_tables(dec_ref, c, lambda p: c - p, lambda p: p)

    @pl.when(seqs.is_last(ci))
    def _():
        s_ref[...] = jnp.zeros_like(s_ref)

    cosf = cos_ref[...]
    sinf = sin_ref[...]
    for h in range(HEADS):
        lg = _log_gamma(h)
        q = _rot(q_ref[:, h * DK:(h + 1) * DK].astype(F32), cosf, sinf)
        k = _rot(k_ref[:, h * DK:(h + 1) * DK].astype(F32), cosf, sinf) * (DK ** -0.5)
        v = v_ref[:, h * DV:(h + 1) * DV]
        qd = (q * dec_ref[h, 0]).astype(BF16)
        kd = (k * dec_ref[h, 1]).astype(BF16)
        s = s_ref[h]
        o_ref[:, h * DV:(h + 1) * DV] = jnp.dot(qd, s.astype(BF16), preferred_element_type=F32)
        s_ref[h] = math.exp(lg * c) * s + lax.dot_general(
            kd, v, (((0,), (0,)), ((), ())), preferred_element_type=F32)


def _ret_fwd_kernel(q_ref, k_ref, v_ref, cos_ref, sin_ref, bwd_ref, gate_ref, gn_ref, o_ref, s_ref, dm_ref,
                    dec_ref, *, seqs, c):
    ci = pl.program_id(0)

    @pl.when(ci == 0)
    def _():
        _decay_tables(dec_ref, c, lambda p: p + 1.0, lambda p: c - 1.0 - p)
        r = lax.broadcasted_iota(I32, (c, c), 0)
        col = lax.broadcasted_iota(I32, (c, c), 1)
        dist = jnp.abs(r - col).astype(F32)
        for h in range(HEADS):
            dm_ref[h] = jnp.exp(_log_gamma(h) * dist)

    @pl.when(seqs.is_first(ci))
    def _():
        s_ref[...] = jnp.zeros_like(s_ref)

    cosf = cos_ref[...]
    sinf = sin_ref[...]
    for h in range(HEADS):
        lg = _log_gamma(h)
        q = _rot(q_ref[:, h * DK:(h + 1) * DK].astype(F32), cosf, sinf)
        k = _rot(k_ref[:, h * DK:(h + 1) * DK].astype(F32), cosf, sinf) * (DK ** -0.5)
        v = v_ref[:, h * DV:(h + 1) * DV]
        scores = lax.dot_general(q.astype(BF16), k.astype(BF16), (((1,), (1,)), ((), ())),
                                 preferred_element_type=F32) * dm_ref[h]
        tot = jnp.dot(scores.astype(BF16), v, preferred_element_type=F32)
        qd = (q * dec_ref[h, 0]).astype(BF16)
        kd = (k * dec_ref[h, 1]).astype(BF16)
        s = s_ref[h]
        tot = tot + jnp.dot(qd, s.astype(BF16), preferred_element_type=F32)
        s_ref[h] = math.exp(lg * c) * s + lax.dot_general(
            kd, v, (((0,), (0,)), ((), ())), preferred_element_type=F32)
        tot = tot + bwd_ref[:, h * DV:(h + 1) * DV]
        mu = jnp.mean(tot, axis=-1, keepdims=True)
        cen = tot - mu
        var = jnp.mean(cen * cen, axis=-1, keepdims=True)
        yn = cen * lax.rsqrt(var + EPS) * gn_ref[:, h * DV:(h + 1) * DV]
        g = gate_ref[:, h * DV:(h + 1) * DV].astype(F32)
        o_ref[:, h * DV:(h + 1) * DV] = (yn * (g * jax.nn.sigmoid(g))).astype(o_ref.dtype)


def _retention(z, cosf, sinf, gn, seqs, c):
    t = z.shape[0]
    nch = seqs.nch
    rev = lambda i: nch - 1 - i
    qkv_specs = lambda f: [pl.BlockSpec((c, QK_W), lambda i: (f(i), 0)),
                           pl.BlockSpec((c, QK_W), lambda i: (f(i), 1)),
                           pl.BlockSpec((c, V_W), lambda i: (f(i), 1)),
                           pl.BlockSpec((c, DK), lambda i: (f(i), 0)),
                           pl.BlockSpec((c, DK), lambda i: (f(i), 0))]
    bwd = pl.pallas_call(
        functools.partial(_ret_bwd_kernel, seqs=seqs, c=c),
        grid=(nch,),
        in_specs=qkv_specs(rev),
        out_specs=pl.BlockSpec((c, V_W), lambda i: (rev(i), 0)),
        out_shape=jax.ShapeDtypeStruct((t, V_W), F32),
        scratch_shapes=[pltpu.VMEM((HEADS, DK, DV), F32), pltpu.VMEM((HEADS, 2, c, DK), F32)],
        compiler_params=_cp(("arbitrary",), 32),
        name="ret_bwd",
    )(z, z, z, cosf, sinf)
    fwd_id = lambda i: i
    return pl.pallas_call(
        functools.partial(_ret_fwd_kernel, seqs=seqs, c=c),
        grid=(nch,),
        in_specs=qkv_specs(fwd_id) + [pl.BlockSpec((c, V_W), lambda i: (i, 0)),
                                      pl.BlockSpec((c, V_W), lambda i: (i, 2)),
                                      pl.BlockSpec((1, V_W), lambda i: (0, 0))],
        out_specs=pl.BlockSpec((c, V_W), lambda i: (i, 0)),
        out_shape=jax.ShapeDtypeStruct((t, V_W), BF16),
        scratch_shapes=[pltpu.VMEM((HEADS, DK, DV), F32), pltpu.VMEM((HEADS, c, c), F32),
                        pltpu.VMEM((HEADS, 2, c, DK), F32)],
        compiler_params=_cp(("arbitrary",), 40),
        name="ret_fwd",
    )(z, z, z, cosf, sinf, bwd, z, gn)


def _lru_kernel(*refs, reverse, seqs, c, cb):
    if reverse:
        (x_ref, xp_ref, xn_ref, cw_ref, cbias_ref, wa_ref, ba_ref, wx_ref, bx_ref, c8_ref, hf_ref, g_ref,
         o_ref, a_s, u_s, h_s, carry_s) = refs
    else:
        (x_ref, xp_ref, xn_ref, cw_ref, cbias_ref, wa_ref, ba_ref, wx_ref, bx_ref, c8_ref,
         o_ref, a_s, u_s, h_s, carry_s) = refs
    t = pl.program_id(1)
    ci = seqs.nch - 1 - t if reverse else t
    first = seqs.is_first(ci)
    last = seqs.is_last(ci)

    x = x_ref[...].astype(F32)
    prev = jnp.where(first, 0.0, xp_ref[...].astype(F32))
    nxt = jnp.where(last, 0.0, xn_ref[...].astype(F32))
    row8 = lax.broadcasted_iota(I32, (8, 1), 0)

    def patch(arr, at, rows8):
        parts = ([arr[:at]] if at > 0 else []) + [rows8] + ([arr[at + 8:]] if at + 8 < c else [])
        return jnp.concatenate(parts, axis=0)

    xm1 = pltpu.roll(x, 1, 0)
    xm1 = patch(xm1, 0, jnp.where(row8 == 0, prev[HALO - 1:HALO], xm1[0:8]))
    xm2 = pltpu.roll(x, 2, 0)
    xm2 = patch(xm2, 0, jnp.where(row8 == 0, prev[HALO - 2:HALO - 1],
                                  jnp.where(row8 == 1, prev[HALO - 1:HALO], xm2[0:8])))
    xp1 = pltpu.roll(x, c - 1, 0)
    xp1 = patch(xp1, c - 8, jnp.where(row8 == 7, nxt[0:1], xp1[c - 8:c]))
    cw = cw_ref[...]
    xc = cw[0:1] * xm2 + cw[1:2] * xm1 + cw[2:3] * x + cw[3:4] * xp1 + cbias_ref[...]

    nslab = cb // LRU_BLOCK
    for gi in range(nslab):
        sl = slice(gi * LRU_BLOCK, (gi + 1) * LRU_BLOCK)
        xs = xc[:, sl]
        xb = xs.astype(BF16)
        r = jax.nn.sigmoid(jnp.dot(xb, wa_ref[gi], preferred_element_type=F32) + ba_ref[:, sl])
        ig = jax.nn.sigmoid(jnp.dot(xb, wx_ref[gi], preferred_element_type=F32) + bx_ref[:, sl])
        a = jnp.exp(c8_ref[:, sl] * r)
        a_s[gi] = a
        v = 1.0 - a * a
        u_s[gi] = jnp.where(v > 0.0, v * lax.rsqrt(v), 0.0) * (ig * xs)

    @pl.when(last if reverse else first)
    def _():
        carry_s[...] = jnp.zeros_like(carry_s)

    sub = lax.broadcasted_iota(I32, (8, LRU_BLOCK), 0)
    nsuper = c // LRU_SUPER
    steps = list(range(LRU_SEG))
    if reverse:
        steps = steps[::-1]

    def super_group(q, carries):
        qi = nsuper - 1 - q if reverse else q
        base = qi * LRU_SUPER
        out = []
        for gi in range(nslab):
            hs, ps = {}, {}
            h = p = None
            for j in steps:
                rows = pl.ds(base + j, 8, stride=LRU_SEG)
                a = a_s[gi, rows, :]
                u = u_s[gi, rows, :]
                h = u if h is None else a * h + u
                p = a if p is None else a * p
                hs[j], ps[j] = h, p
            eh, ep = h, p
            for s in (1, 2, 4):
                shift = 8 - s if reverse else s
                m = (sub < 8 - s) if reverse else (sub >= s)
                eh_sh = pltpu.roll(eh, shift, 0)
                ep_sh = pltpu.roll(ep, shift, 0)
                eh = eh + ep * jnp.where(m, eh_sh, 0.0)
                ep = ep * jnp.where(m, ep_sh, 1.0)
            end = eh + ep * carries[gi]
            if reverse:
                enter = jnp.where(sub == 7, carries[gi], pltpu.roll(end, 7, 0))
                out.append(jnp.broadcast_to(end[0:1], (8, LRU_BLOCK)))
            else:
                enter = jnp.where(sub == 0, carries[gi], pltpu.roll(end, 1, 0))
                out.append(jnp.broadcast_to(end[7:8], (8, LRU_BLOCK)))
            for j in steps:
                h_s[gi, pl.ds(base + j, 8, stride=LRU_SEG), :] = hs[j] + ps[j] * enter
        return tuple(out)

    carries = lax.fori_loop(0, nsuper, super_group, tuple(carry_s[gi] for gi in range(nslab)))
    for gi in range(nslab):
        carry_s[gi] = carries[gi]

    for gi in range(nslab):
        sl = slice(gi * LRU_BLOCK, (gi + 1) * LRU_BLOCK)
        if reverse:
            g = g_ref[:, sl].astype(F32)
            o_ref[:, sl] = ((hf_ref[:, sl] + h_s[gi]) * jax.nn.gelu(g, approximate=True)).astype(o_ref.dtype)
        else:
            o_ref[:, sl] = h_s[gi]


def _lru_call(z, conv_w, conv_b, wa, ba, wx, bx, c8, hf, direction, seqs, c, cb):
    t = z.shape[0]
    nch = seqs.nch
    reverse = direction == 1
    tmap = (lambda ti: nch - 1 - ti) if reverse else (lambda ti: ti)
    xcol = (2 * QK_W + 2 * V_W) // cb
    gcol = (2 * QK_W + 2 * V_W + D) // cb
    hb = c // HALO
    nhb = t // HALO
    nb = cb // LRU_BLOCK
    in_specs = [
        pl.BlockSpec((c, cb), lambda ch, ti: (tmap(ti), xcol + ch)),
        pl.BlockSpec((HALO, cb), lambda ch, ti: (jnp.maximum(tmap(ti) * hb - 1, 0), xcol + ch)),
        pl.BlockSpec((HALO, cb), lambda ch, ti: (jnp.minimum((tmap(ti) + 1) * hb, nhb - 1), xcol + ch)),
        pl.BlockSpec((4, cb), lambda ch, ti: (0, ch)),
        pl.BlockSpec((1, cb), lambda ch, ti: (0, ch)),
        pl.BlockSpec((None, nb, LRU_BLOCK, LRU_BLOCK), lambda ch, ti: (direction, ch, 0, 0)),
        pl.BlockSpec((None, 1, cb), lambda ch, ti: (direction, 0, ch)),
        pl.BlockSpec((None, nb, LRU_BLOCK, LRU_BLOCK), lambda ch, ti: (direction, ch, 0, 0)),
        pl.BlockSpec((None, 1, cb), lambda ch, ti: (direction, 0, ch)),
        pl.BlockSpec((None, 1, cb), lambda ch, ti: (direction, 0, ch)),
    ]
    args = [z, z, z, conv_w, conv_b, wa, ba, wx, bx, c8]
    if reverse:
        in_specs += [pl.BlockSpec((c, cb), lambda ch, ti: (tmap(ti), ch)),
                     pl.BlockSpec((c, cb), lambda ch, ti: (tmap(ti), gcol + ch))]
        args += [hf, z]
    return pl.pallas_call(
        functools.partial(_lru_kernel, reverse=reverse, seqs=seqs, c=c, cb=cb),
        grid=(D // cb, nch),
        in_specs=in_specs,
        out_specs=pl.BlockSpec((c, cb), lambda ch, ti: (tmap(ti), ch)),
        out_shape=jax.ShapeDtypeStruct((t, D), BF16 if reverse else F32),
        scratch_shapes=[pltpu.VMEM((nb, c, LRU_BLOCK), F32), pltpu.VMEM((nb, c, LRU_BLOCK), F32),
                        pltpu.VMEM((nb, c, LRU_BLOCK), F32), pltpu.VMEM((nb, 8, LRU_BLOCK), F32)],
        compiler_params=_cp(("arbitrary", "arbitrary"),32),
        name="lru_rev" if reverse else "lru_fwd",
    )(*args)


def _proj_merge_kernel(yr_ref, yl_ref, wr_ref, wl_ref, mr_ref, ml_ref, o_ref):
    r = jnp.dot(yr_ref[...], wr_ref[...], preferred_element_type=F32)
    l = jnp.dot(yl_ref[...], wl_ref[...], preferred_element_type=F32)
    o_ref[...] = (jax.nn.sigmoid(mr_ref[...].astype(F32)) * r
                  + jax.nn.sigmoid(ml_ref[...].astype(F32)) * l).astype(o_ref.dtype)


def _proj_merge(y_ret, y_lru, w_ret_o, w_lru_o, z, tm=512, tn=1024):
    t = y_ret.shape[0]
    mcol = (2 * QK_W + 2 * V_W + 2 * D) // tn
    return pl.pallas_call(
        _proj_merge_kernel,
        grid=(D // tn, t // tm),
        in_specs=[pl.BlockSpec((tm, V_W), lambda j, i: (i, 0)),
                  pl.BlockSpec((tm, D), lambda j, i: (i, 0)),
                  pl.BlockSpec((V_W, tn), lambda j, i: (0, j)),
                  pl.BlockSpec((D, tn), lambda j, i: (0, j)),
                  pl.BlockSpec((tm, tn), lambda j, i: (i, mcol + j)),
                  pl.BlockSpec((tm, tn), lambda j, i: (i, mcol + D // tn + j))],
        out_specs=pl.BlockSpec((tm, tn), lambda j, i: (i, j)),
        out_shape=jax.ShapeDtypeStruct((t, D), BF16),
        compiler_params=_cp(("arbitrary", "arbitrary"), 48),
        name="proj_merge",
    )(y_ret, y_lru, w_ret_o, w_lru_o, z, z)


def _proj_out_kernel(m_ref, w_ref, xp_ref, xs_ref, o_ref, *, npt):
    i = pl.program_id(1)
    y = jnp.dot(m_ref[...], w_ref[...], preferred_element_type=F32)

    @pl.when(i < npt)
    def _():
        o_ref[...] = xp_ref[...] + y

    @pl.when(i >= npt)
    def _():
        o_ref[...] = xs_ref[...] + y


def _proj_out(merged, w_out, xp, xs, tm=512, tn=1024):
    t = merged.shape[0]
    npt = xp.shape[0] // tm
    return pl.pallas_call(
        functools.partial(_proj_out_kernel, npt=npt),
        grid=(D // tn, t // tm),
        in_specs=[pl.BlockSpec((tm, D), lambda j, i: (i, 0)),
                  pl.BlockSpec((D, tn), lambda j, i: (0, j)),
                  pl.BlockSpec((tm, tn), lambda j, i: (jnp.minimum(i, npt - 1), j)),
                  pl.BlockSpec((tm, tn), lambda j, i: (jnp.maximum(i - npt, 0), j))],
        out_specs=pl.BlockSpec((tm, tn), lambda j, i: (i, j)),
        out_shape=jax.ShapeDtypeStruct((t, D), F32),
        compiler_params=_cp(("arbitrary", "arbitrary"), 48),
        name="proj_out",
    )(merged, w_out, xp, xs)


def _router_kernel(x_ref, g_ref, whi_ref, wlo_ref, b_ref, xn_ref, tpos_ref, tpos_t_ref, gate_ref, meta_ref, tri_ref,
                   run_ref, *, tm):
    i = pl.program_id(0)

    @pl.when(i == 0)
    def _():
        r = lax.broadcasted_iota(I32, (tm, tm), 0)
        col = lax.broadcasted_iota(I32, (tm, tm), 1)
        tri_ref[...] = (col < r).astype(BF16)
        run_ref[...] = jnp.zeros_like(run_ref)

    x = x_ref[...]
    xn = x * lax.rsqrt(jnp.mean(x * x, axis=-1, keepdims=True) + EPS) * g_ref[...]
    xh = xn.astype(BF16)
    xn_ref[...] = xh

    xl = (xn - xh.astype(F32)).astype(BF16)
    logits = (jnp.dot(xh, whi_ref[...], preferred_element_type=F32)
              + (jnp.dot(xl, whi_ref[...], preferred_element_type=F32)
                 + jnp.dot(xh, wlo_ref[...], preferred_element_type=F32))) + b_ref[...]
    lane = lax.broadcasted_iota(I32, (tm, LANE), 1)
    lane_f = lane.astype(F32)
    vals, idxs = [], []
    cur = logits
    for _ in range(TOP_K):
        m = jnp.max(cur, axis=-1, keepdims=True)
        idx = jnp.min(jnp.where(cur == m, lane_f, float(LANE)), axis=-1, keepdims=True).astype(I32)
        vals.append(m)
        idxs.append(idx)
        cur = jnp.where(lane == idx, -jnp.inf, cur)
    exps = [jnp.exp(v - vals[0]) for v in vals]
    denom = exps[0] + exps[1] + exps[2] + exps[3]

    onehot = jnp.zeros((tm, LANE), F32)
    for idx in idxs:
        onehot = onehot + (lane == idx).astype(F32)
    before = jnp.dot(tri_ref[...], onehot.astype(BF16), preferred_element_type=F32)

    cnt = jnp.broadcast_to(jnp.sum(onehot, axis=0, keepdims=True), (8, LANE))
    cnt8 = jnp.floor((cnt + (SEG_ALIGN - 1)) * (1.0 / SEG_ALIGN)) * SEG_ALIGN
    lane8 = lax.broadcasted_iota(I32, (8, LANE), 1)
    incl = cnt8
    s = 1
    while s < LANE:
        incl = incl + jnp.where(lane8 >= s, pltpu.roll(incl, s, 1), 0.0)
        s *= 2
    toff = incl - cnt8

    pos = before + toff[0:1]
    tpos_out = jnp.zeros((tm, LANE), F32)
    gate_out = jnp.zeros((tm, LANE), F32)
    for k in range(TOP_K):
        tpos_k = jnp.sum(jnp.where(lane == idxs[k], pos, 0.0), axis=-1, keepdims=True)
        tpos_out = jnp.where(lane == k, tpos_k, tpos_out)
        gate_out = jnp.where(lane == k, exps[k] / denom, gate_out)
    tpos_ref[...] = tpos_out.astype(I32)
    tpos_t_ref[...] = tpos_out.T[0:8].astype(I32)
    gate_ref[...] = gate_out

    sub8 = lax.broadcasted_iota(I32, (8, LANE), 0)
    run = jnp.broadcast_to(run_ref[...], (8, LANE))
    meta_ref[...] = jnp.where(sub8 == 0, cnt8, jnp.where(sub8 == 1, toff, jnp.where(sub8 == 2, run, 0.0)))
    run_ref[...] = run_ref[...] + cnt8[0:1]


def _router(x1, g, w_hi, w_lo, b_pad, tm):
    t = x1.shape[0]
    nt = t // tm
    tok = lambda i: (i, 0)
    fixed = lambda i: (0, 0)
    per_tile = lambda i: (i, 0, 0)
    return pl.pallas_call(
        functools.partial(_router_kernel, tm=tm),
        grid=(nt,),
        in_specs=[pl.BlockSpec((tm, D), tok), pl.BlockSpec((1, D), fixed),
                  pl.BlockSpec((D, LANE), fixed), pl.BlockSpec((D, LANE), fixed), pl.BlockSpec((1, LANE), fixed)],
        out_specs=[pl.BlockSpec((tm, D), tok), pl.BlockSpec((tm, LANE), tok),
                   pl.BlockSpec((None, 8, tm), per_tile), pl.BlockSpec((tm, LANE), tok),
                   pl.BlockSpec((None, 8, LANE), per_tile)],
        out_shape=[jax.ShapeDtypeStruct((t, D), BF16), jax.ShapeDtypeStruct((t, LANE), I32),
                   jax.ShapeDtypeStruct((nt, 8, tm), I32), jax.ShapeDtypeStruct((t, LANE), F32),
                   jax.ShapeDtypeStruct((nt, 8, LANE), F32)],
        scratch_shapes=[pltpu.VMEM((tm, tm), BF16), pltpu.VMEM((1, LANE), F32)],
        compiler_params=_cp(("arbitrary",), 32),
        name="router",
    )(x1, g, w_hi, w_lo, b_pad)


N_CLASS = SEG_MAXBIT + 1
USED_LANE = N_CLASS + 1


def _copy_lists(toff, c8, dst):
    nt = c8.shape[0]
    n = (c8 // SEG_ALIGN)[:, None, :]
    b = jnp.arange(N_CLASS, dtype=I32)[None, :, None]
    bits = (n >> b) & 1
    off = ((n >> (b + 1)) << (b + 1)) * SEG_ALIGN
    order = jnp.argsort(1 - bits, axis=-1, stable=True)
    src = jnp.take_along_axis(toff[:, None, :] + off, order, axis=-1)
    dstl = jnp.take_along_axis(dst[:, None, :] + off, order, axis=-1)
    src_t = jnp.zeros((nt, 8, LANE), I32).at[:, :N_CLASS, :N_EXP].set(src)
    src_t = src_t.at[:, N_CLASS, :N_CLASS].set(jnp.sum(bits, axis=-1))
    src_t = src_t.at[:, N_CLASS, USED_LANE].set(toff[:, -1] + c8[:, -1])
    dst_t = jnp.zeros((nt, 8, LANE), I32).at[:, :N_CLASS, :N_EXP].set(dstl)
    return src_t, dst_t


def _class_copies(src_ref, dst_ref, make_copy, act):
    for b in range(N_CLASS):
        rows = SEG_ALIGN << b

        def body(s, carry, b=b, rows=rows):
            act(make_copy(pl.multiple_of(src_ref[b, s], SEG_ALIGN), pl.multiple_of(dst_ref[b, s], SEG_ALIGN), rows))
            return carry

        lax.fori_loop(0, src_ref[N_CLASS, b], body, 0)


def _dispatch_kernel(zblk_ref, src_ref, dst_ref, tpos_t_ref, x_ref, xg_ref, sorted_ref, zero_ref, sem):
    i = pl.program_id(0)

    def zero_copy(e):
        start = pl.multiple_of(jnp.maximum(zblk_ref[e], 0), MOE_ROWS)
        return pltpu.make_async_copy(zero_ref, xg_ref.at[pl.ds(start, MOE_ROWS), :], sem)

    def for_nonempty(act):
        def body(e, carry):
            pl.when(zblk_ref[e] >= 0)(lambda: act(zero_copy(e)))
            return carry
        lax.fori_loop(0, N_EXP, body, 0)

    @pl.when(i == 0)
    def _():
        zero_ref[...] = jnp.zeros_like(zero_ref)
        for_nonempty(lambda cp: cp.start())
        for_nonempty(lambda cp: cp.wait())

    x = x_ref[...]
    for c in range(SORT_ROWS // PERM_CHUNK):
        p = c * PERM_CHUNK + lax.broadcasted_iota(I32, (PERM_CHUNK, 1), 0)
        perm = jnp.zeros((PERM_CHUNK, x.shape[0]), F32)
        for k in range(TOP_K):
            perm = perm + jnp.where(p == tpos_t_ref[k:k + 1, :], 1.0, 0.0)
        rows = jnp.dot(perm.astype(BF16), x, preferred_element_type=F32)
        sorted_ref[c * PERM_CHUNK:(c + 1) * PERM_CHUNK, :] = _pack_pair(rows[:, :HALF], rows[:, HALF:], rounded=True)

    def make_copy(t_off, d, rows):
        return pltpu.make_async_copy(sorted_ref.at[pl.ds(t_off, rows), :], xg_ref.at[pl.ds(d, rows), :], sem)

    _class_copies(src_ref, dst_ref, make_copy, lambda cp: cp.start())
    _class_copies(src_ref, dst_ref, make_copy, lambda cp: cp.wait())


def _dispatch(zblk, src_lists, dst_lists, tpos_t, xn, rows):
    nt, _, tm = tpos_t.shape
    per_tile = lambda i, *_: (i, 0, 0)
    grid_spec = pltpu.PrefetchScalarGridSpec(
        num_scalar_prefetch=1,
        grid=(nt,),
        in_specs=[pl.BlockSpec((None, 8, LANE), per_tile, memory_space=pltpu.SMEM),
                  pl.BlockSpec((None, 8, LANE), per_tile, memory_space=pltpu.SMEM),
                  pl.BlockSpec((None, 8, tm), per_tile),
                  pl.BlockSpec((tm, D), lambda i, *_: (i, 0))],
        out_specs=pl.BlockSpec(memory_space=pl.ANY),
        scratch_shapes=[pltpu.VMEM((SORT_ROWS, HALF), U32), pltpu.VMEM((MOE_ROWS, HALF), U32),
                        pltpu.SemaphoreType.DMA(())],
    )
    return pl.pallas_call(
        _dispatch_kernel,
        grid_spec=grid_spec,
        out_shape=jax.ShapeDtypeStruct((rows, HALF), U32),
        compiler_params=_cp(("arbitrary",), 40),
        name="dispatch",
    )(zblk, src_lists, dst_lists, tpos_t, xn)


def _expert_changed(be_ref, i):
    return (i == 0) | (be_ref[i] != be_ref[jnp.maximum(i - 1, 0)])


def _expert_up_kernel(be_ref, nu_ref, xg_ref, wg_ref, bg_ref, wu_ref, bu_ref, h_ref, wgb_ref, wub_ref):
    i = pl.program_id(1)

    @pl.when(i < nu_ref[0])
    def _():
        @pl.when(_expert_changed(be_ref, i))
        def _():
            wgb_ref[...] = wg_ref[...].astype(BF16)
            wub_ref[...] = wu_ref[...].astype(BF16)

        lo, hi = _unpack_pair(xg_ref[...])
        xb = jnp.concatenate([lo.astype(BF16), hi.astype(BF16)], axis=-1)
        gate = jnp.minimum(jnp.dot(xb, wgb_ref[...], preferred_element_type=F32) + bg_ref[...], SWIGLU_LIMIT)
        up = jnp.clip(jnp.dot(xb, wub_ref[...], preferred_element_type=F32) + bu_ref[...],
                      -SWIGLU_LIMIT, SWIGLU_LIMIT)
        h_ref[...] = ((up + 1.0) * gate * jax.nn.sigmoid(SWIGLU_ALPHA * gate)).astype(h_ref.dtype)


def _expert_down_kernel(be_ref, nu_ref, h_ref, wlo_ref, whi_ref, blo_ref, bhi_ref, o_ref, wlob_ref, whib_ref):
    i = pl.program_id(1)

    @pl.when(i < nu_ref[0])
    def _():
        @pl.when(_expert_changed(be_ref, i))
        def _():
            wlob_ref[...] = wlo_ref[...].astype(BF16)
            whib_ref[...] = whi_ref[...].astype(BF16)

        h = h_ref[...]
        lo = jnp.dot(h, wlob_ref[...], preferred_element_type=F32) + blo_ref[...]
        hi = jnp.dot(h, whib_ref[...], preferred_element_type=F32) + bhi_ref[...]
        o_ref[...] = _pack_pair(lo, hi)


def _experts(block_expert, n_used, xg, wg, bg, wu, bu, wd, bd, rows_blk, tf, tn):
    rows = xg.shape[0]
    nb = rows // rows_blk
    nj = D_FF // tf
    nn = HALF // tn

    def blk(i, nu):
        return jnp.minimum(i, nu[0] - 1)

    up_spec = pltpu.PrefetchScalarGridSpec(
        num_scalar_prefetch=2,
        grid=(nj, nb),
        in_specs=[pl.BlockSpec((rows_blk, HALF), lambda j, i, be, nu: (blk(i, nu), 0)),
                  pl.BlockSpec((None, D, tf), lambda j, i, be, nu: (be[i], 0, j)),
                  pl.BlockSpec((None, 1, tf), lambda j, i, be, nu: (be[i], 0, j)),
                  pl.BlockSpec((None, D, tf), lambda j, i, be, nu: (be[i], 0, j)),
                  pl.BlockSpec((None, 1, tf), lambda j, i, be, nu: (be[i], 0, j))],
        out_specs=pl.BlockSpec((rows_blk, tf), lambda j, i, be, nu: (blk(i, nu), j)),
        scratch_shapes=[pltpu.VMEM((D, tf), BF16), pltpu.VMEM((D, tf), BF16)],
    )
    hid = pl.pallas_call(
        _expert_up_kernel,
        grid_spec=up_spec,
        out_shape=jax.ShapeDtypeStruct((rows, D_FF), BF16),
        compiler_params=_cp(("arbitrary", "arbitrary"), 60),
        name="experts_up",
    )(block_expert, n_used, xg, wg, bg, wu, bu)

    down_spec = pltpu.PrefetchScalarGridSpec(
        num_scalar_prefetch=2,
        grid=(nn, nb),
        in_specs=[pl.BlockSpec((rows_blk, D_FF), lambda n, i, be, nu: (blk(i, nu), 0)),
                  pl.BlockSpec((None, D_FF, tn), lambda n, i, be, nu: (be[i], 0, n)),
                  pl.BlockSpec((None, D_FF, tn), lambda n, i, be, nu: (be[i], 0, nn + n)),
                  pl.BlockSpec((None, 1, tn), lambda n, i, be, nu: (be[i], 0, n)),
                  pl.BlockSpec((None, 1, tn), lambda n, i, be, nu: (be[i], 0, nn + n))],
        out_specs=pl.BlockSpec((rows_blk, tn), lambda n, i, be, nu: (blk(i, nu), n)),
        scratch_shapes=[pltpu.VMEM((D_FF, tn), BF16), pltpu.VMEM((D_FF, tn), BF16)],
    )
    return pl.pallas_call(
        _expert_down_kernel,
        grid_spec=down_spec,
        out_shape=jax.ShapeDtypeStruct((rows, HALF), U32),
        compiler_params=_cp(("arbitrary", "arbitrary"), 60),
        name="experts_down",
    )(block_expert, n_used, hid, wd, wd, bd, bd)


def _combine_kernel(src_ref, dst_ref, src_next_ref, dst_next_ref, os_ref, tpos_ref, gate_ref, x1_ref, fg_ref,
                    op_ref, osm_ref, sorted_ref, sem, *, npt, nt):
    i = pl.program_id(0)
    slot = i % 2

    def copies(lists, buf, act):
        def make_copy(t_off, d, rows):
            return pltpu.make_async_copy(os_ref.at[pl.ds(d, rows), :], sorted_ref.at[buf, pl.ds(t_off, rows), :],
                                         sem.at[buf])
        _class_copies(*lists, make_copy, act)

    this_tile = (src_ref, dst_ref)
    pl.when(i == 0)(lambda: copies(this_tile, 0, lambda cp: cp.start()))
    pl.when(i + 1 < nt)(lambda: copies((src_next_ref, dst_next_ref), 1 - slot, lambda cp: cp.start()))
    copies(this_tile, slot, lambda cp: cp.wait())

    used = src_ref[N_CLASS, USED_LANE]
    tpos = tpos_ref[...]
    gates = gate_ref[...]
    y_lo = x1_ref[:, :HALF]
    y_hi = x1_ref[:, HALF:]
    for c in range(SORT_ROWS // PERM_CHUNK):
        p_lane = c * PERM_CHUNK + lax.broadcasted_iota(I32, (1, PERM_CHUNK), 1)
        place = jnp.zeros((tpos.shape[0], PERM_CHUNK), F32)
        for k in range(TOP_K):
            place = place + jnp.where(tpos[:, k:k + 1] == p_lane, gates[:, k:k + 1], 0.0)
        p_row = c * PERM_CHUNK + lax.broadcasted_iota(I32, (PERM_CHUNK, 1), 0)
        u = jnp.where(p_row < used, sorted_ref[slot, c * PERM_CHUNK:(c + 1) * PERM_CHUNK, :], jnp.uint32(0))
        lo, hi = _unpack_pair(u)
        place = place.astype(BF16)
        y_lo = y_lo + jnp.dot(place, lo.astype(BF16), preferred_element_type=F32)
        y_hi = y_hi + jnp.dot(place, hi.astype(BF16), preferred_element_type=F32)
    ms = (jnp.sum(y_lo * y_lo, axis=-1, keepdims=True) + jnp.sum(y_hi * y_hi, axis=-1, keepdims=True)) / D
    inv = lax.rsqrt(ms + EPS)
    out = jnp.concatenate([y_lo * inv * fg_ref[:, :HALF], y_hi * inv * fg_ref[:, HALF:]], axis=-1)

    @pl.when(i < npt)
    def _():
        op_ref[...] = out

    @pl.when(i >= npt)
    def _():
        osm_ref[...] = out


def _combine(src_lists, dst_lists, out_sorted, tpos, gates, x1, fg, tp, tm):
    t = x1.shape[0]
    npt = tp // tm
    nt = t // tm
    tok = lambda i: (i, 0)
    this_tile = lambda i: (i, 0, 0)
    next_tile = lambda i: (jnp.minimum(i + 1, nt - 1), 0, 0)
    lists = lambda index_map: pl.BlockSpec((None, 8, LANE), index_map, memory_space=pltpu.SMEM)
    return pl.pallas_call(
        functools.partial(_combine_kernel, npt=npt, nt=nt),
        grid=(nt,),
        in_specs=[lists(this_tile), lists(this_tile), lists(next_tile), lists(next_tile),
                  pl.BlockSpec(memory_space=pl.ANY),
                  pl.BlockSpec((tm, LANE), tok), pl.BlockSpec((tm, LANE), tok), pl.BlockSpec((tm, D), tok),
                  pl.BlockSpec((1, D), lambda i: (0, 0))],
        out_specs=[pl.BlockSpec((tm, D), lambda i: (jnp.minimum(i, npt - 1), 0)),
                   pl.BlockSpec((tm, D), lambda i: (jnp.maximum(i - npt, 0), 0))],
        out_shape=[jax.ShapeDtypeStruct((tp, D), F32), jax.ShapeDtypeStruct((t - tp, D), F32)],
        scratch_shapes=[pltpu.VMEM((2, SORT_ROWS, HALF), U32), pltpu.SemaphoreType.DMA((2,))],
        compiler_params=_cp(("arbitrary",), 56),
        name="combine",
    )(src_lists, dst_lists, src_lists, dst_lists, out_sorted, tpos, gates, x1, fg)


def _rope_tables(n_prompt, prompt_len, sample_len):
    half = DK // 2
    inv = ROPE_BASE ** (-jnp.arange(half, dtype=F32) / half)
    pos = jnp.concatenate([jnp.tile(jnp.arange(prompt_len, dtype=F32), n_prompt),
                           jnp.arange(sample_len, dtype=F32)])
    ang = pos[:, None] * inv[None, :]
    cos, sin = jnp.cos(ang), jnp.sin(ang)
    return jnp.concatenate([cos, cos], axis=-1), jnp.concatenate([-sin, sin], axis=-1)


def _layer(xp, xs, n_prompt, prompt_len, sample_len, mix_norm_g, w_in, ret_norm_g, w_ret_o, conv_w, conv_b, lru_w_a,
           lru_b_a, lru_w_x, lru_b_x, lru_lambda, w_lru_o, w_out, moe_norm_g, w_router, b_router, w_e_gate,
           b_e_gate, w_e_up, b_e_up, w_e_down, b_e_down, final_norm_g):
    tp = xp.shape[0]
    t = tp + xs.shape[0]
    seqs = _Seqs(n_prompt, prompt_len, sample_len, SEQ_CHUNK)
    row = lambda v: v.reshape(1, -1)

    z = _inproj(xp, xs, row(mix_norm_g), w_in.astype(BF16))

    cosf, sinf = _rope_tables(n_prompt, prompt_len, sample_len)
    y_ret = _retention(z, cosf, sinf, row(ret_norm_g), seqs, SEQ_CHUNK)

    wa, wx = lru_w_a.astype(BF16), lru_w_x.astype(BF16)
    ba, bx = lru_b_a[:, None, :], lru_b_x[:, None, :]
    c8 = (-LRU_C * jax.nn.softplus(-lru_lambda))[:, None, :]
    lru_args = (z, conv_w, row(conv_b), wa, ba, wx, bx, c8)
    lru_seqs = _Seqs(n_prompt, prompt_len, sample_len, LRU_CHUNK)
    h_fwd = _lru_call(*lru_args, None, 0, lru_seqs, LRU_CHUNK, LRU_CB)
    y_lru = _lru_call(*lru_args, h_fwd, 1, lru_seqs, LRU_CHUNK, LRU_CB)

    merged = _proj_merge(y_ret, y_lru, w_ret_o.astype(BF16), w_lru_o.astype(BF16), z)
    x1 = _proj_out(merged, w_out.astype(BF16), xp, xs)

    w_pad = jnp.zeros((D, LANE), F32).at[:, :N_EXP].set(w_router)
    b_pad = jnp.full((1, LANE), -1e30, F32).at[0, :N_EXP].set(b_router)
    w_hi = w_pad.astype(BF16)
    w_lo = (w_pad - w_hi.astype(F32)).astype(BF16)
    xn2, tpos, tpos_t, gates, meta = _router(x1, row(moe_norm_g), w_hi, w_lo, b_pad, TOK_TILE)

    nt = t // TOK_TILE
    c8 = meta[:, 0, :N_EXP].astype(I32)
    toff = meta[:, 1, :N_EXP].astype(I32)
    run = meta[:, 2, :N_EXP].astype(I32)
    total = run[-1] + c8[-1]
    padded = (total + MOE_ROWS - 1) // MOE_ROWS * MOE_ROWS
    pad_ends = jnp.cumsum(padded)
    pad_starts = pad_ends - padded
    src_lists, dst_lists = _copy_lists(toff, c8, pad_starts[None, :] + run)
    zblk = jnp.where(padded > 0, pad_ends - MOE_ROWS, -1)
    n_blocks = (t * TOP_K + (SEG_ALIGN - 1) * nt * N_EXP) // MOE_ROWS + 1 + N_EXP
    n_used = (pad_ends[-1] // MOE_ROWS).astype(I32)
    blk_ids = jnp.minimum(jnp.arange(n_blocks, dtype=I32), n_used - 1)
    block_expert = jnp.minimum(
        jnp.sum((pad_ends[None, :] <= (blk_ids * MOE_ROWS)[:, None]).astype(I32), axis=-1), N_EXP - 1)

    xg = _dispatch(zblk, src_lists, dst_lists, tpos_t, xn2, n_blocks * MOE_ROWS)
    out_sorted = _experts(block_expert, n_used.reshape(1), xg,
                          w_e_gate, b_e_gate[:, None, :], w_e_up, b_e_up[:, None, :],
                          w_e_down, b_e_down[:, None, :], MOE_ROWS, MOE_TF, MOE_TN)
    return _combine(src_lists, dst_lists, out_sorted, tpos, gates, x1, row(final_norm_g), tp, TOK_TILE)


def kernel(x_prompt, x_sample, mix_norm_g, w_in, ret_norm_g, w_ret_o, conv_w, conv_b, lru_w_a, lru_b_a, lru_w_x,
           lru_b_x, lru_lambda, w_lru_o, w_out, moe_norm_g, w_router, b_router, w_e_gate, b_e_gate, w_e_up, b_e_up,
           w_e_down, b_e_down, final_norm_g):
    assert mix_norm_g.shape[0] == 1, "one layer"
    n_prompt, prompt_len, _ = x_prompt.shape
    n_sample, sample_len, _ = x_sample.shape
    assert n_sample == 1
    tp = n_prompt * prompt_len
    yp, ys = _layer(x_prompt.reshape(tp, D), x_sample.reshape(sample_len, D), n_prompt, prompt_len, sample_len,
                    mix_norm_g[0], w_in[0], ret_norm_g[0], w_ret_o[0], conv_w[0],
                    conv_b[0], lru_w_a[0], lru_b_a[0], lru_w_x[0], lru_b_x[0], lru_lambda[0], w_lru_o[0], w_out[0],
                    moe_norm_g[0], w_router[0], b_router[0], w_e_gate[0], b_e_gate[0], w_e_up[0], b_e_up[0],
                    w_e_down[0], b_e_down[0], final_norm_g)
    return yp.reshape(x_prompt.shape), ys.reshape(x_sample.shape)
```

```python
import functools
import math

import jax
import jax.numpy as jnp
from jax import lax
from jax.experimental import pallas as pl
from jax.experimental.pallas import tpu as pltpu

F32 = jnp.float32
BF16 = jnp.bfloat16
U32 = jnp.uint32
I32 = jnp.int32

D = 2048
HEADS = 8
DK = 128
DV = 256
QK_W = HEADS * DK
V_W = HEADS * DV
LRU_BLOCK = 128
N_EXP = 32
TOP_K = 4
D_FF = 2048
SWIGLU_LIMIT = 7.0
SWIGLU_ALPHA = 1.702
ROPE_BASE = 10000.0
LRU_C = 8.0
EPS = 1e-6
IN_W = 2 * QK_W + 2 * V_W + 2 * D + 2 * D

HALF = D // 2
LANE = 128
HALO = 16
HI_MASK = 0xFFFF0000

SEQ_CHUNK = 256
LRU_CHUNK = 512
LRU_CB = 1024
LRU_SEG = 4
LRU_SUPER = 8 * LRU_SEG
MOE_ROWS = 512
MOE_TF = 1024
MOE_TN = 1024
TOK_TILE = 512
SEG_ALIGN = 8
PERM_CHUNK = 256
SORT_ROWS = TOK_TILE * 4 + 32 * SEG_ALIGN
SEG_MAXBIT = (TOK_TILE // SEG_ALIGN).bit_length() - 1


def _cp(sem, vmem_mb):
    return pltpu.CompilerParams(dimension_semantics=sem, vmem_limit_bytes=vmem_mb << 20)


def _bits(x):
    return lax.bitcast_convert_type(x, U32)


def _pack_pair(lo, hi, rounded=False):
    if not rounded:
        lo = lo.astype(BF16).astype(F32)
        hi = hi.astype(BF16).astype(F32)
    return (_bits(lo) >> 16) | (_bits(hi) & jnp.uint32(HI_MASK))


def _unpack_pair(u):
    lo = lax.bitcast_convert_type(u << 16, F32)
    hi = lax.bitcast_convert_type(u & jnp.uint32(HI_MASK), F32)
    return lo, hi


class _Seqs:
    def __init__(self, n_prompt, prompt_len, sample_len, chunk):
        assert prompt_len % chunk == 0 and sample_len % chunk == 0
        self.cps = prompt_len // chunk
        self.npc = n_prompt * self.cps
        self.nch = self.npc + sample_len // chunk

    def is_first(self, ci):
        return ((ci < self.npc) & (ci % self.cps == 0)) | (ci == self.npc)

    def is_last(self, ci):
        return ((ci < self.npc) & (ci % self.cps == self.cps - 1)) | (ci == self.nch - 1)

    def pos_chunk(self, ci):
        return jnp.where(ci < self.npc, ci % self.cps, ci - self.npc)


def _two_group_specs(tm, npt, width=D):
    return [pl.BlockSpec((tm, width), lambda i, j: (jnp.minimum(i, npt - 1), 0)),
            pl.BlockSpec((tm, width), lambda i, j: (jnp.maximum(i - npt, 0), 0))]


def _inproj_kernel(xp_ref, xs_ref, g_ref, w_ref, o_ref, xn_ref, *, npt):
    i = pl.program_id(0)

    def norm(x_ref):
        x = x_ref[...]
        y = x * lax.rsqrt(jnp.mean(x * x, axis=-1, keepdims=True) + EPS)
        xn_ref[...] = (y * g_ref[...]).astype(BF16)

    @pl.when(pl.program_id(1) == 0)
    def _():
        pl.when(i < npt)(lambda: norm(xp_ref))
        pl.when(i >= npt)(lambda: norm(xs_ref))

    o_ref[...] = jnp.dot(xn_ref[...], w_ref[...], preferred_element_type=F32).astype(o_ref.dtype)


def _inproj(xp, xs, g, w, tm=1024, tn=1024):
    t, n = xp.shape[0] + xs.shape[0], w.shape[1]
    npt = xp.shape[0] // tm
    return pl.pallas_call(
        functools.partial(_inproj_kernel, npt=npt),
        grid=(t // tm, n // tn),
        in_specs=_two_group_specs(tm, npt) + [pl.BlockSpec((1, D), lambda i, j: (0, 0)),
                                              pl.BlockSpec((D, tn), lambda i, j: (0, j))],
        out_specs=pl.BlockSpec((tm, tn), lambda i, j: (i, j)),
        out_shape=jax.ShapeDtypeStruct((t, n), BF16),
        scratch_shapes=[pltpu.VMEM((tm, D), BF16)],
        compiler_params=_cp(("arbitrary", "arbitrary"), 56),
        name="inproj",
    )(xp, xs, g, w)


def _rot(x, cosf, sinf):
    return x * cosf + pltpu.roll(x, DK // 2, 1) * sinf


def _log_gamma(h):
    return math.log1p(-(2.0 ** (-5 - h)))


def _decay_tables(dec_ref, c, q_exponent, k_exponent):
    p = lax.broadcasted_iota(I32, (c, DK), 0).astype(F32)
    for h in range(HEADS):
        lg = _log_gamma(h)
        dec_ref[h, 0] = jnp.exp(lg * q_exponent(p))
        dec_ref[h, 1] = jnp.exp(lg * k_exponent(p))


def _ret_bwd_kernel(q_ref, k_ref, v_ref, cos_ref, sin_ref, o_ref, s_ref, dec_ref, *, seqs, c):
    ci = seqs.nch - 1 - pl.program_id(0)

    @pl.when(pl.program_id(0) == 0)
    def _():
        _decay_tables(dec_ref, c, lambda p: c - p, lambda p: p)

    @pl.when(seqs.is_last(ci))
    def _():
        s_ref[...] = jnp.zeros_like(s_ref)

    cosf = cos_ref[...]
    sinf = sin_ref[...]
    for h in range(HEADS):
        lg = _log_gamma(h)
        q = _rot(q_ref[:, h * DK:(h + 1) * DK].astype(F32), cosf, sinf)
        k = _rot(k_ref[:, h * DK:(h + 1) * DK].astype(F32), cosf, sinf) * (DK ** -0.5)
        v = v_ref[:, h * DV:(h + 1) * DV]
        qd = (q * dec_ref[h, 0]).astype(BF16)
        kd = (k * dec_ref[h, 1]).astype(BF16)
        s = s_ref[h]
        o_ref[:, h * DV:(h + 1) * DV] = jnp.dot(qd, s.astype(BF16), preferred_element_type=F32)
        s_ref[h] = math.exp(lg * c) * s + lax.dot_general(
            kd, v, (((0,), (0,)), ((), ())), preferred_element_type=F32)


def _ret_fwd_kernel(q_ref, k_ref, v_ref, cos_ref, sin_ref, bwd_ref, gate_ref, gn_ref, o_ref, s_ref, dm_ref,
                    dec_ref, *, seqs, c):
    ci = pl.program_id(0)

    @pl.when(ci == 0)
    def _():
        _decay_tables(dec_ref, c, lambda p: p + 1.0, lambda p: c - 1.0 - p)
        r = lax.broadcasted_iota(I32, (c, c), 0)
        col = lax.broadcasted_iota(I32, (c, c), 1)
        dist = jnp.abs(r - col).astype(F32)
        for h in range(HEADS):
            dm_ref[h] = jnp.exp(_log_gamma(h) * dist)

    @pl.when(seqs.is_first(ci))
    def _():
        s_ref[...] = jnp.zeros_like(s_ref)

    cosf = cos_ref[...]
    sinf = sin_ref[...]
    for h in range(HEADS):
        lg = _log_gamma(h)
        q = _rot(q_ref[:, h * DK:(h + 1) * DK].astype(F32), cosf, sinf)
        k = _rot(k_ref[:, h * DK:(h + 1) * DK].astype(F32), cosf, sinf) * (DK ** -0.5)
        v = v_ref[:, h * DV:(h + 1) * DV]
        scores = lax.dot_general(q.astype(BF16), k.astype(BF16), (((1,), (1,)), ((), ())),
                                 preferred_element_type=F32) * dm_ref[h]
        tot = jnp.dot(scores.astype(BF16), v, preferred_element_type=F32)
        qd = (q * dec_ref[h, 0]).astype(BF16)
        kd = (k * dec_ref[h, 1]).astype(BF16)
        s = s_ref[h]
        tot = tot + jnp.dot(qd, s.astype(BF16), preferred_element_type=F32)
        s_ref[h] = math.exp(lg * c) * s + lax.dot_general(
            kd, v, (((0,), (0,)), ((), ())), preferred_element_type=F32)
        tot = tot + bwd_ref[:, h * DV:(h + 1) * DV]
        mu = jnp.mean(tot, axis=-1, keepdims=True)
        cen = tot - mu
        var = jnp.mean(cen * cen, axis=-1, keepdims=True)
        yn = cen * lax.rsqrt(var + EPS) * gn_ref[:, h * DV:(h + 1) * DV]
        g = gate_ref[:, h * DV:(h + 1) * DV].astype(F32)
        o_ref[:, h * DV:(h + 1) * DV] = (yn * (g * jax.nn.sigmoid(g))).astype(o_ref.dtype)


def _retention(z, cosf, sinf, gn, seqs, c):
    t = z.shape[0]
    nch = seqs.nch
    rev = lambda i: nch - 1 - i
    qkv_specs = lambda f: [pl.BlockSpec((c, QK_W), lambda i: (f(i), 0)),
                           pl.BlockSpec((c, QK_W), lambda i: (f(i), 1)),
                           pl.BlockSpec((c, V_W), lambda i: (f(i), 1)),
                           pl.BlockSpec((c, DK), lambda i: (seqs.pos_chunk(f(i)), 0)),
                           pl.BlockSpec((c, DK), lambda i: (seqs.pos_chunk(f(i)), 0))]
    bwd = pl.pallas_call(
        functools.partial(_ret_bwd_kernel, seqs=seqs, c=c),
        grid=(nch,),
        in_specs=qkv_specs(rev),
        out_specs=pl.BlockSpec((c, V_W), lambda i: (rev(i), 0)),
        out_shape=jax.ShapeDtypeStruct((t, V_W), F32),
        scratch_shapes=[pltpu.VMEM((HEADS, DK, DV), F32), pltpu.VMEM((HEADS, 2, c, DK), F32)],
        compiler_params=_cp(("arbitrary",), 32),
        name="ret_bwd",
    )(z, z, z, cosf, sinf)
    fwd_id = lambda i: i
    return pl.pallas_call(
        functools.partial(_ret_fwd_kernel, seqs=seqs, c=c),
        grid=(nch,),
        in_specs=qkv_specs(fwd_id) + [pl.BlockSpec((c, V_W), lambda i: (i, 0)),
                                      pl.BlockSpec((c, V_W), lambda i: (i, 2)),
                                      pl.BlockSpec((1, V_W), lambda i: (0, 0))],
        out_specs=pl.BlockSpec((c, V_W), lambda i: (i, 0)),
        out_shape=jax.ShapeDtypeStruct((t, V_W), BF16),
        scratch_shapes=[pltpu.VMEM((HEADS, DK, DV), F32), pltpu.VMEM((HEADS, c, c), F32),
                        pltpu.VMEM((HEADS, 2, c, DK), F32)],
        compiler_params=_cp(("arbitrary",), 40),
        name="ret_fwd",
    )(z, z, z, cosf, sinf, bwd, z, gn)


def _lru_kernel(*refs, reverse, seqs, c, cb):
    if reverse:
        (x_ref, xp_ref, xn_ref, cw_ref, cbias_ref, wa_ref, ba_ref, wx_ref, bx_ref, c8_ref, hf_ref, g_ref,
         o_ref, a_s, u_s, h_s, carry_s) = refs
    else:
        (x_ref, xp_ref, xn_ref, cw_ref, cbias_ref, wa_ref, ba_ref, wx_ref, bx_ref, c8_ref,
         o_ref, a_s, u_s, h_s, carry_s) = refs
    t = pl.program_id(1)
    ci = seqs.nch - 1 - t if reverse else t
    first = seqs.is_first(ci)
    last = seqs.is_last(ci)

    x = x_ref[...].astype(F32)
    prev = jnp.where(first, 0.0, xp_ref[...].astype(F32))
    nxt = jnp.where(last, 0.0, xn_ref[...].astype(F32))
    row8 = lax.broadcasted_iota(I32, (8, 1), 0)

    def patch(arr, at, rows8):
        parts = ([arr[:at]] if at > 0 else []) + [rows8] + ([arr[at + 8:]] if at + 8 < c else [])
        return jnp.concatenate(parts, axis=0)

    xm1 = pltpu.roll(x, 1, 0)
    xm1 = patch(xm1, 0, jnp.where(row8 == 0, prev[HALO - 1:HALO], xm1[0:8]))
    xm2 = pltpu.roll(x, 2, 0)
    xm2 = patch(xm2, 0, jnp.where(row8 == 0, prev[HALO - 2:HALO - 1],
                                  jnp.where(row8 == 1, prev[HALO - 1:HALO], xm2[0:8])))
    xp1 = pltpu.roll(x, c - 1, 0)
    xp1 = patch(xp1, c - 8, jnp.where(row8 == 7, nxt[0:1], xp1[c - 8:c]))
    cw = cw_ref[...]
    xc = cw[0:1] * xm2 + cw[1:2] * xm1 + cw[2:3] * x + cw[3:4] * xp1 + cbias_ref[...]

    nslab = cb // LRU_BLOCK
    for gi in range(nslab):
        sl = slice(gi * LRU_BLOCK, (gi + 1) * LRU_BLOCK)
        xs = xc[:, sl]
        xb = xs.astype(BF16)
        r = jax.nn.sigmoid(jnp.dot(xb, wa_ref[gi], preferred_element_type=F32) + ba_ref[:, sl])
        ig = jax.nn.sigmoid(jnp.dot(xb, wx_ref[gi], preferred_element_type=F32) + bx_ref[:, sl])
        a = jnp.exp(c8_ref[:, sl] * r)
        a_s[gi] = a
        v = 1.0 - a * a
        u_s[gi] = jnp.where(v > 0.0, v * lax.rsqrt(v), 0.0) * (ig * xs)

    @pl.when(last if reverse else first)
    def _():
        carry_s[...] = jnp.zeros_like(carry_s)

    sub = lax.broadcasted_iota(I32, (8, LRU_BLOCK), 0)
    nsuper = c // LRU_SUPER
    steps = list(range(LRU_SEG))
    if reverse:
        steps = steps[::-1]

    def super_group(q, carries):
        qi = nsuper - 1 - q if reverse else q
        base = qi * LRU_SUPER
        out = []
        for gi in range(nslab):
            hs, ps = {}, {}
            h = p = None
            for j in steps:
                rows = pl.ds(base + j, 8, stride=LRU_SEG)
                a = a_s[gi, rows, :]
                u = u_s[gi, rows, :]
                h = u if h is None else a * h + u
                p = a if p is None else a * p
                hs[j], ps[j] = h, p
            eh, ep = h, p
            for s in (1, 2, 4):
                shift = 8 - s if reverse else s
                m = (sub < 8 - s) if reverse else (sub >= s)
                eh_sh = pltpu.roll(eh, shift, 0)
                ep_sh = pltpu.roll(ep, shift, 0)
                eh = eh + ep * jnp.where(m, eh_sh, 0.0)
                ep = ep * jnp.where(m, ep_sh, 1.0)
            end = eh + ep * carries[gi]
            if reverse:
                enter = jnp.where(sub == 7, carries[gi], pltpu.roll(end, 7, 0))
                out.append(jnp.broadcast_to(end[0:1], (8, LRU_BLOCK)))
            else:
                enter = jnp.where(sub == 0, carries[gi], pltpu.roll(end, 1, 0))
                out.append(jnp.broadcast_to(end[7:8], (8, LRU_BLOCK)))
            for j in steps:
                h_s[gi, pl.ds(base + j, 8, stride=LRU_SEG), :] = hs[j] + ps[j] * enter
        return tuple(out)

    carries = lax.fori_loop(0, nsuper, super_group, tuple(carry_s[gi] for gi in range(nslab)))
    for gi in range(nslab):
        carry_s[gi] = carries[gi]

    for gi in range(nslab):
        sl = slice(gi * LRU_BLOCK, (gi + 1) * LRU_BLOCK)
        if reverse:
            g = g_ref[:, sl].astype(F32)
            o_ref[:, sl] = ((hf_ref[:, sl] + h_s[gi]) * jax.nn.gelu(g, approximate=True)).astype(o_ref.dtype)
        else:
            o_ref[:, sl] = h_s[gi]


def _lru_call(z, conv_w, conv_b, wa, ba, wx, bx, c8, hf, direction, seqs, c, cb):
    t = z.shape[0]
    nch = seqs.nch
    reverse = direction == 1
    tmap = (lambda ti: nch - 1 - ti) if reverse else (lambda ti: ti)
    xcol = (2 * QK_W + 2 * V_W) // cb
    gcol = (2 * QK_W + 2 * V_W + D) // cb
    hb = c // HALO
    nhb = t // HALO
    nb = cb // LRU_BLOCK
    in_specs = [
        pl.BlockSpec((c, cb), lambda ch, ti: (tmap(ti), xcol + ch)),
        pl.BlockSpec((HALO, cb), lambda ch, ti: (jnp.maximum(tmap(ti) * hb - 1, 0), xcol + ch)),
        pl.BlockSpec((HALO, cb), lambda ch, ti: (jnp.minimum((tmap(ti) + 1) * hb, nhb - 1), xcol + ch)),
        pl.BlockSpec((4, cb), lambda ch, ti: (0, ch)),
        pl.BlockSpec((1, cb), lambda ch, ti: (0, ch)),
        pl.BlockSpec((None, nb, LRU_BLOCK, LRU_BLOCK), lambda ch, ti: (direction, ch, 0, 0)),
        pl.BlockSpec((None, 1, cb), lambda ch, ti: (direction, 0, ch)),
        pl.BlockSpec((None, nb, LRU_BLOCK, LRU_BLOCK), lambda ch, ti: (direction, ch, 0, 0)),
        pl.BlockSpec((None, 1, cb), lambda ch, ti: (direction, 0, ch)),
        pl.BlockSpec((None, 1, cb), lambda ch, ti: (direction, 0, ch)),
    ]
    args = [z, z, z, conv_w, conv_b, wa, ba, wx, bx, c8]
    if reverse:
        in_specs += [pl.BlockSpec((c, cb), lambda ch, ti: (tmap(ti), ch)),
                     pl.BlockSpec((c, cb), lambda ch, ti: (tmap(ti), gcol + ch))]
        args += [hf, z]
    return pl.pallas_call(
        functools.partial(_lru_kernel, reverse=reverse, seqs=seqs, c=c, cb=cb),
        grid=(D // cb, nch),
        in_specs=in_specs,
        out_specs=pl.BlockSpec((c, cb), lambda ch, ti: (tmap(ti), ch)),
        out_shape=jax.ShapeDtypeStruct((t, D), BF16 if reverse else F32),
        scratch_shapes=[pltpu.VMEM((nb, c, LRU_BLOCK), F32), pltpu.VMEM((nb, c, LRU_BLOCK), F32),
                        pltpu.VMEM((nb, c, LRU_BLOCK), F32), pltpu.VMEM((nb, 8, LRU_BLOCK), F32)],
        compiler_params=_cp(("arbitrary", "arbitrary"),32),
        name="lru_rev" if reverse else "lru_fwd",
    )(*args)


def _proj_merge_kernel(yr_ref, yl_ref, wr_ref, wl_ref, mr_ref, ml_ref, o_ref):
    r = jnp.dot(yr_ref[...], wr_ref[...], preferred_element_type=F32)
    l = jnp.dot(yl_ref[...], wl_ref[...], preferred_element_type=F32)
    o_ref[...] = (jax.nn.sigmoid(mr_ref[...].astype(F32)) * r
                  + jax.nn.sigmoid(ml_ref[...].astype(F32)) * l).astype(o_ref.dtype)


def _proj_merge(y_ret, y_lru, w_ret_o, w_lru_o, z, tm=512, tn=1024):
    t = y_ret.shape[0]
    mcol = (2 * QK_W + 2 * V_W + 2 * D) // tn
    return pl.pallas_call(
        _proj_merge_kernel,
        grid=(D // tn, t // tm),
        in_specs=[pl.BlockSpec((tm, V_W), lambda j, i: (i, 0)),
                  pl.BlockSpec((tm, D), lambda j, i: (i, 0)),
                  pl.BlockSpec((V_W, tn), lambda j, i: (0, j)),
                  pl.BlockSpec((D, tn), lambda j, i: (0, j)),
                  pl.BlockSpec((tm, tn), lambda j, i: (i, mcol + j)),
                  pl.BlockSpec((tm, tn), lambda j, i: (i, mcol + D // tn + j))],
        out_specs=pl.BlockSpec((tm, tn), lambda j, i: (i, j)),
        out_shape=jax.ShapeDtypeStruct((t, D), BF16),
        compiler_params=_cp(("arbitrary", "arbitrary"), 48),
        name="proj_merge",
    )(y_ret, y_lru, w_ret_o, w_lru_o, z, z)


def _proj_out_kernel(m_ref, w_ref, xp_ref, xs_ref, o_ref, *, npt):
    i = pl.program_id(1)
    y = jnp.dot(m_ref[...], w_ref[...], preferred_element_type=F32)

    @pl.when(i < npt)
    def _():
        o_ref[...] = xp_ref[...] + y

    @pl.when(i >= npt)
    def _():
        o_ref[...] = xs_ref[...] + y


def _proj_out(merged, w_out, xp, xs, tm=512, tn=1024):
    t = merged.shape[0]
    npt = xp.shape[0] // tm
    return pl.pallas_call(
        functools.partial(_proj_out_kernel, npt=npt),
        grid=(D // tn, t // tm),
        in_specs=[pl.BlockSpec((tm, D), lambda j, i: (i, 0)),
                  pl.BlockSpec((D, tn), lambda j, i: (0, j)),
                  pl.BlockSpec((tm, tn), lambda j, i: (jnp.minimum(i, npt - 1), j)),
                  pl.BlockSpec((tm, tn), lambda j, i: (jnp.maximum(i - npt, 0), j))],
        out_specs=pl.BlockSpec((tm, tn), lambda j, i: (i, j)),
        out_shape=jax.ShapeDtypeStruct((t, D), F32),
        compiler_params=_cp(("arbitrary", "arbitrary"), 48),
        name="proj_out",
    )(merged, w_out, xp, xs)


def _router_kernel(x_ref, g_ref, whi_ref, wlo_ref, b_ref, xn_ref, tpos_ref, tpos_t_ref, gate_ref, meta_ref, tri_ref,
                   run_ref, *, tm):
    i = pl.program_id(0)

    @pl.when(i == 0)
    def _():
        r = lax.broadcasted_iota(I32, (tm, tm), 0)
        col = lax.broadcasted_iota(I32, (tm, tm), 1)
        tri_ref[...] = (col < r).astype(BF16)
        run_ref[...] = jnp.zeros_like(run_ref)

    x = x_ref[...]
    xn = x * lax.rsqrt(jnp.mean(x * x, axis=-1, keepdims=True) + EPS) * g_ref[...]
    xh = xn.astype(BF16)
    xn_ref[...] = xh

    xl = (xn - xh.astype(F32)).astype(BF16)
    logits = (jnp.dot(xh, whi_ref[...], preferred_element_type=F32)
              + (jnp.dot(xl, whi_ref[...], preferred_element_type=F32)
                 + jnp.dot(xh, wlo_ref[...], preferred_element_type=F32))) + b_ref[...]
    lane = lax.broadcasted_iota(I32, (tm, LANE), 1)
    lane_f = lane.astype(F32)
    vals, idxs = [], []
    cur = logits
    for _ in range(TOP_K):
        m = jnp.max(cur, axis=-1, keepdims=True)
        idx = jnp.min(jnp.where(cur == m, lane_f, float(LANE)), axis=-1, keepdims=True).astype(I32)
        vals.append(m)
        idxs.append(idx)
        cur = jnp.where(lane == idx, -jnp.inf, cur)
    exps = [jnp.exp(v - vals[0]) for v in vals]
    denom = exps[0] + exps[1] + exps[2] + exps[3]

    onehot = jnp.zeros((tm, LANE), F32)
    for idx in idxs:
        onehot = onehot + (lane == idx).astype(F32)
    before = jnp.dot(tri_ref[...], onehot.astype(BF16), preferred_element_type=F32)

    cnt = jnp.broadcast_to(jnp.sum(onehot, axis=0, keepdims=True), (8, LANE))
    cnt8 = jnp.floor((cnt + (SEG_ALIGN - 1)) * (1.0 / SEG_ALIGN)) * SEG_ALIGN
    lane8 = lax.broadcasted_iota(I32, (8, LANE), 1)
    incl = cnt8
    s = 1
    while s < LANE:
        incl = incl + jnp.where(lane8 >= s, pltpu.roll(incl, s, 1), 0.0)
        s *= 2
    toff = incl - cnt8

    pos = before + toff[0:1]
    tpos_out = jnp.zeros((tm, LANE), F32)
    gate_out = jnp.zeros((tm, LANE), F32)
    for k in range(TOP_K):
        tpos_k = jnp.sum(jnp.where(lane == idxs[k], pos, 0.0), axis=-1, keepdims=True)
        tpos_out = jnp.where(lane == k, tpos_k, tpos_out)
        gate_out = jnp.where(lane == k, exps[k] / denom, gate_out)
    tpos_ref[...] = tpos_out.astype(I32)
    tpos_t_ref[...] = tpos_out.T[0:8].astype(I32)
    gate_ref[...] = gate_out

    sub8 = lax.broadcasted_iota(I32, (8, LANE), 0)
    run = jnp.broadcast_to(run_ref[...], (8, LANE))
    meta_ref[...] = jnp.where(sub8 == 0, cnt8, jnp.where(sub8 == 1, toff, jnp.where(sub8 == 2, run, 0.0)))
    run_ref[...] = run_ref[...] + cnt8[0:1]


def _router(x1, g, w_hi, w_lo, b_pad, tm):
    t = x1.shape[0]
    nt = t // tm
    tok = lambda i: (i, 0)
    fixed = lambda i: (0, 0)
    per_tile = lambda i: (i, 0, 0)
    return pl.pallas_call(
        functools.partial(_router_kernel, tm=tm),
        grid=(nt,),
        in_specs=[pl.BlockSpec((tm, D), tok), pl.BlockSpec((1, D), fixed),
                  pl.BlockSpec((D, LANE), fixed), pl.BlockSpec((D, LANE), fixed), pl.BlockSpec((1, LANE), fixed)],
        out_specs=[pl.BlockSpec((tm, D), tok), pl.BlockSpec((tm, LANE), tok),
                   pl.BlockSpec((None, 8, tm), per_tile), pl.BlockSpec((tm, LANE), tok),
                   pl.BlockSpec((None, 8, LANE), per_tile)],
        out_shape=[jax.ShapeDtypeStruct((t, D), BF16), jax.ShapeDtypeStruct((t, LANE), I32),
                   jax.ShapeDtypeStruct((nt, 8, tm), I32), jax.ShapeDtypeStruct((t, LANE), F32),
                   jax.ShapeDtypeStruct((nt, 8, LANE), F32)],
        scratch_shapes=[pltpu.VMEM((tm, tm), BF16), pltpu.VMEM((1, LANE), F32)],
        compiler_params=_cp(("arbitrary",), 32),
        name="router",
    )(x1, g, w_hi, w_lo, b_pad)


N_CLASS = SEG_MAXBIT + 1
USED_LANE = N_CLASS + 1


def _copy_lists(toff, c8, dst):
    nt = c8.shape[0]
    n = (c8 // SEG_ALIGN)[:, None, :]
    b = jnp.arange(N_CLASS, dtype=I32)[None, :, None]
    bits = (n >> b) & 1
    off = ((n >> (b + 1)) << (b + 1)) * SEG_ALIGN
    order = jnp.argsort(1 - bits, axis=-1, stable=True)
    src = jnp.take_along_axis(toff[:, None, :] + off, order, axis=-1)
    dstl = jnp.take_along_axis(dst[:, None, :] + off, order, axis=-1)
    src_t = jnp.zeros((nt, 8, LANE), I32).at[:, :N_CLASS, :N_EXP].set(src)
    src_t = src_t.at[:, N_CLASS, :N_CLASS].set(jnp.sum(bits, axis=-1))
    src_t = src_t.at[:, N_CLASS, USED_LANE].set(toff[:, -1] + c8[:, -1])
    dst_t = jnp.zeros((nt, 8, LANE), I32).at[:, :N_CLASS, :N_EXP].set(dstl)
    return src_t, dst_t


def _class_copies(src_ref, dst_ref, make_copy, act):
    for b in range(N_CLASS):
        rows = SEG_ALIGN << b

        def body(s, carry, b=b, rows=rows):
            act(make_copy(pl.multiple_of(src_ref[b, s], SEG_ALIGN), pl.multiple_of(dst_ref[b, s], SEG_ALIGN), rows))
            return carry

        lax.fori_loop(0, src_ref[N_CLASS, b], body, 0)


def _dispatch_kernel(zblk_ref, src_ref, dst_ref, src_prev_ref, dst_prev_ref, tpos_t_ref, x_ref, xg_ref, sorted_ref,
                     zero_ref, sem, *, nt):
    i = pl.program_id(0)
    slot = i % 2

    def zero_copy(e):
        start = pl.multiple_of(jnp.maximum(zblk_ref[e], 0), MOE_ROWS)
        return pltpu.make_async_copy(zero_ref, xg_ref.at[pl.ds(start, MOE_ROWS), :], sem.at[0])

    def for_nonempty(act):
        def body(e, carry):
            pl.when(zblk_ref[e] >= 0)(lambda: act(zero_copy(e)))
            return carry
        lax.fori_loop(0, N_EXP, body, 0)

    @pl.when(i == 0)
    def _():
        zero_ref[...] = jnp.zeros_like(zero_ref)
        for_nonempty(lambda cp: cp.start())
        for_nonempty(lambda cp: cp.wait())

    x = x_ref[...]
    for c in range(SORT_ROWS // PERM_CHUNK):
        p = c * PERM_CHUNK + lax.broadcasted_iota(I32, (PERM_CHUNK, 1), 0)
        perm = jnp.zeros((PERM_CHUNK, x.shape[0]), F32)
        for k in range(TOP_K):
            perm = perm + jnp.where(p == tpos_t_ref[k:k + 1, :], 1.0, 0.0)
        rows = jnp.dot(perm.astype(BF16), x, preferred_element_type=F32)
        sorted_ref[slot, c * PERM_CHUNK:(c + 1) * PERM_CHUNK, :] = _pack_pair(rows[:, :HALF], rows[:, HALF:],
                                                                              rounded=True)

    def copies(lists, buf, act):
        def make_copy(t_off, d, rows):
            return pltpu.make_async_copy(sorted_ref.at[buf, pl.ds(t_off, rows), :], xg_ref.at[pl.ds(d, rows), :],
                                         sem.at[buf])
        _class_copies(*lists, make_copy, act)

    this_tile = (src_ref, dst_ref)
    pl.when(i > 0)(lambda: copies((src_prev_ref, dst_prev_ref), 1 - slot, lambda cp: cp.wait()))
    copies(this_tile, slot, lambda cp: cp.start())
    pl.when(i == nt - 1)(lambda: copies(this_tile, slot, lambda cp: cp.wait()))


def _dispatch(zblk, src_lists, dst_lists, tpos_t, xn, rows):
    nt, _, tm = tpos_t.shape
    this_tile = lambda i, *_: (i, 0, 0)
    prev_tile = lambda i, *_: (jnp.maximum(i - 1, 0), 0, 0)
    lists = lambda index_map: pl.BlockSpec((None, 8, LANE), index_map, memory_space=pltpu.SMEM)
    grid_spec = pltpu.PrefetchScalarGridSpec(
        num_scalar_prefetch=1,
        grid=(nt,),
        in_specs=[lists(this_tile), lists(this_tile), lists(prev_tile), lists(prev_tile),
                  pl.BlockSpec((None, 8, tm), this_tile),
                  pl.BlockSpec((tm, D), lambda i, *_: (i, 0))],
        out_specs=pl.BlockSpec(memory_space=pl.ANY),
        scratch_shapes=[pltpu.VMEM((2, SORT_ROWS, HALF), U32), pltpu.VMEM((MOE_ROWS, HALF), U32),
                        pltpu.SemaphoreType.DMA((2,))],
    )
    return pl.pallas_call(
        functools.partial(_dispatch_kernel, nt=nt),
        grid_spec=grid_spec,
        out_shape=jax.ShapeDtypeStruct((rows, HALF), U32),
        compiler_params=_cp(("arbitrary",), 40),
        name="dispatch",
    )(zblk, src_lists, dst_lists, src_lists, dst_lists, tpos_t, xn)


def _expert_changed(be_ref, i):
    return (i == 0) | (be_ref[i] != be_ref[jnp.maximum(i - 1, 0)])


def _expert_up_kernel(be_ref, nu_ref, xg_ref, wg_ref, bg_ref, wu_ref, bu_ref, h_ref, wgb_ref, wub_ref):
    i = pl.program_id(1)

    @pl.when(i < nu_ref[0])
    def _():
        @pl.when(_expert_changed(be_ref, i))
        def _():
            wgb_ref[...] = wg_ref[...].astype(BF16)
            wub_ref[...] = wu_ref[...].astype(BF16)

        lo, hi = _unpack_pair(xg_ref[...])
        xb = jnp.concatenate([lo.astype(BF16), hi.astype(BF16)], axis=-1)
        gate = jnp.minimum(jnp.dot(xb, wgb_ref[...], preferred_element_type=F32) + bg_ref[...], SWIGLU_LIMIT)
        up = jnp.clip(jnp.dot(xb, wub_ref[...], preferred_element_type=F32) + bu_ref[...],
                      -SWIGLU_LIMIT, SWIGLU_LIMIT)
        h_ref[...] = ((up + 1.0) * gate * jax.nn.sigmoid(SWIGLU_ALPHA * gate)).astype(h_ref.dtype)


def _expert_down_kernel(be_ref, nu_ref, h_ref, wlo_ref, whi_ref, blo_ref, bhi_ref, o_ref, wlob_ref, whib_ref):
    i = pl.program_id(1)

    @pl.when(i < nu_ref[0])
    def _():
        @pl.when(_expert_changed(be_ref, i))
        def _():
            wlob_ref[...] = wlo_ref[...].astype(BF16)
            whib_ref[...] = whi_ref[...].astype(BF16)

        h = h_ref[...]
        lo = jnp.dot(h, wlob_ref[...], preferred_element_type=F32) + blo_ref[...]
        hi = jnp.dot(h, whib_ref[...], preferred_element_type=F32) + bhi_ref[...]
        o_ref[...] = _pack_pair(lo, hi)


def _experts(block_expert, n_used, xg, wg, bg, wu, bu, wd, bd, rows_blk, tf, tn):
    rows = xg.shape[0]
    nb = rows // rows_blk
    nj = D_FF // tf
    nn = HALF // tn

    def blk(i, nu):
        return jnp.minimum(i, nu[0] - 1)

    up_spec = pltpu.PrefetchScalarGridSpec(
        num_scalar_prefetch=2,
        grid=(nj, nb),
        in_specs=[pl.BlockSpec((rows_blk, HALF), lambda j, i, be, nu: (blk(i, nu), 0)),
                  pl.BlockSpec((None, D, tf), lambda j, i, be, nu: (be[i], 0, j)),
                  pl.BlockSpec((None, 1, tf), lambda j, i, be, nu: (be[i], 0, j)),
                  pl.BlockSpec((None, D, tf), lambda j, i, be, nu: (be[i], 0, j)),
                  pl.BlockSpec((None, 1, tf), lambda j, i, be, nu: (be[i], 0, j))],
        out_specs=pl.BlockSpec((rows_blk, tf), lambda j, i, be, nu: (blk(i, nu), j)),
        scratch_shapes=[pltpu.VMEM((D, tf), BF16), pltpu.VMEM((D, tf), BF16)],
    )
    hid = pl.pallas_call(
        _expert_up_kernel,
        grid_spec=up_spec,
        out_shape=jax.ShapeDtypeStruct((rows, D_FF), BF16),
        compiler_params=_cp(("arbitrary", "arbitrary"), 60),
        name="experts_up",
    )(block_expert, n_used, xg, wg, bg, wu, bu)

    down_spec = pltpu.PrefetchScalarGridSpec(
        num_scalar_prefetch=2,
        grid=(nn, nb),
        in_specs=[pl.BlockSpec((rows_blk, D_FF), lambda n, i, be, nu: (blk(i, nu), 0)),
                  pl.BlockSpec((None, D_FF, tn), lambda n, i, be, nu: (be[i], 0, n)),
                  pl.BlockSpec((None, D_FF, tn), lambda n, i, be, nu: (be[i], 0, nn + n)),
                  pl.BlockSpec((None, 1, tn), lambda n, i, be, nu: (be[i], 0, n)),
                  pl.BlockSpec((None, 1, tn), lambda n, i, be, nu: (be[i], 0, nn + n))],
        out_specs=pl.BlockSpec((rows_blk, tn), lambda n, i, be, nu: (blk(i, nu), n)),
        scratch_shapes=[pltpu.VMEM((D_FF, tn), BF16), pltpu.VMEM((D_FF, tn), BF16)],
    )
    return pl.pallas_call(
        _expert_down_kernel,
        grid_spec=down_spec,
        out_shape=jax.ShapeDtypeStruct((rows, HALF), U32),
        compiler_params=_cp(("arbitrary", "arbitrary"), 60),
        name="experts_down",
    )(block_expert, n_used, hid, wd, wd, bd, bd)


def _combine_kernel(src_ref, dst_ref, src_next_ref, dst_next_ref, os_ref, tpos_ref, gate_ref, x1_ref, fg_ref,
                    op_ref, osm_ref, sorted_ref, sem, *, npt, nt):
    i = pl.program_id(0)
    slot = i % 2

    def copies(lists, buf, act):
        def make_copy(t_off, d, rows):
            return pltpu.make_async_copy(os_ref.at[pl.ds(d, rows), :], sorted_ref.at[buf, pl.ds(t_off, rows), :],
                                         sem.at[buf])
        _class_copies(*lists, make_copy, act)

    this_tile = (src_ref, dst_ref)
    pl.when(i == 0)(lambda: copies(this_tile, 0, lambda cp: cp.start()))
    pl.when(i + 1 < nt)(lambda: copies((src_next_ref, dst_next_ref), 1 - slot, lambda cp: cp.start()))
    copies(this_tile, slot, lambda cp: cp.wait())

    used = src_ref[N_CLASS, USED_LANE]
    tpos = tpos_ref[...]
    gates = gate_ref[...]
    y_lo = x1_ref[:, :HALF]
    y_hi = x1_ref[:, HALF:]
    for c in range(SORT_ROWS // PERM_CHUNK):
        p_lane = c * PERM_CHUNK + lax.broadcasted_iota(I32, (1, PERM_CHUNK), 1)
        place = jnp.zeros((tpos.shape[0], PERM_CHUNK), F32)
        for k in range(TOP_K):
            place = place + jnp.where(tpos[:, k:k + 1] == p_lane, gates[:, k:k + 1], 0.0)
        p_row = c * PERM_CHUNK + lax.broadcasted_iota(I32, (PERM_CHUNK, 1), 0)
        u = jnp.where(p_row < used, sorted_ref[slot, c * PERM_CHUNK:(c + 1) * PERM_CHUNK, :], jnp.uint32(0))
        lo, hi = _unpack_pair(u)
        place = place.astype(BF16)
        y_lo = y_lo + jnp.dot(place, lo.astype(BF16), preferred_element_type=F32)
        y_hi = y_hi + jnp.dot(place, hi.astype(BF16), preferred_element_type=F32)
    ms = (jnp.sum(y_lo * y_lo, axis=-1, keepdims=True) + jnp.sum(y_hi * y_hi, axis=-1, keepdims=True)) / D
    inv = lax.rsqrt(ms + EPS)
    out = jnp.concatenate([y_lo * inv * fg_ref[:, :HALF], y_hi * inv * fg_ref[:, HALF:]], axis=-1)

    @pl.when(i < npt)
    def _():
        op_ref[...] = out

    @pl.when(i >= npt)
    def _():
        osm_ref[...] = out


def _combine(src_lists, dst_lists, out_sorted, tpos, gates, x1, fg, tp, tm):
    t = x1.shape[0]
    npt = tp // tm
    nt = t // tm
    tok = lambda i: (i, 0)
    this_tile = lambda i: (i, 0, 0)
    next_tile = lambda i: (jnp.minimum(i + 1, nt - 1), 0, 0)
    lists = lambda index_map: pl.BlockSpec((None, 8, LANE), index_map, memory_space=pltpu.SMEM)
    return pl.pallas_call(
        functools.partial(_combine_kernel, npt=npt, nt=nt),
        grid=(nt,),
        in_specs=[lists(this_tile), lists(this_tile), lists(next_tile), lists(next_tile),
                  pl.BlockSpec(memory_space=pl.ANY),
                  pl.BlockSpec((tm, LANE), tok), pl.BlockSpec((tm, LANE), tok), pl.BlockSpec((tm, D), tok),
                  pl.BlockSpec((1, D), lambda i: (0, 0))],
        out_specs=[pl.BlockSpec((tm, D), lambda i: (jnp.minimum(i, npt - 1), 0)),
                   pl.BlockSpec((tm, D), lambda i: (jnp.maximum(i - npt, 0), 0))],
        out_shape=[jax.ShapeDtypeStruct((tp, D), F32), jax.ShapeDtypeStruct((t - tp, D), F32)],
        scratch_shapes=[pltpu.VMEM((2, SORT_ROWS, HALF), U32), pltpu.SemaphoreType.DMA((2,))],
        compiler_params=_cp(("arbitrary",), 56),
        name="combine",
    )(src_lists, dst_lists, src_lists, dst_lists, out_sorted, tpos, gates, x1, fg)


def _rope_tables(length):
    half = DK // 2
    inv = ROPE_BASE ** (-jnp.arange(half, dtype=F32) / half)
    ang = jnp.arange(length, dtype=F32)[:, None] * inv[None, :]
    cos, sin = jnp.cos(ang), jnp.sin(ang)
    return jnp.concatenate([cos, cos], axis=-1), jnp.concatenate([-sin, sin], axis=-1)


def _layer(xp, xs, n_prompt, prompt_len, sample_len, mix_norm_g, w_in, ret_norm_g, w_ret_o, conv_w, conv_b, lru_w_a,
           lru_b_a, lru_w_x, lru_b_x, lru_lambda, w_lru_o, w_out, moe_norm_g, w_router, b_router, w_e_gate,
           b_e_gate, w_e_up, b_e_up, w_e_down, b_e_down, final_norm_g):
    tp = xp.shape[0]
    t = tp + xs.shape[0]
    seqs = _Seqs(n_prompt, prompt_len, sample_len, SEQ_CHUNK)
    row = lambda v: v.reshape(1, -1)

    z = _inproj(xp, xs, row(mix_norm_g), w_in.astype(BF16))

    cosf, sinf = _rope_tables(max(prompt_len, sample_len))
    y_ret = _retention(z, cosf, sinf, row(ret_norm_g), seqs, SEQ_CHUNK)

    wa, wx = lru_w_a.astype(BF16), lru_w_x.astype(BF16)
    ba, bx = lru_b_a[:, None, :], lru_b_x[:, None, :]
    c8 = (-LRU_C * jax.nn.softplus(-lru_lambda))[:, None, :]
    lru_args = (z, conv_w, row(conv_b), wa, ba, wx, bx, c8)
    lru_seqs = _Seqs(n_prompt, prompt_len, sample_len, LRU_CHUNK)
    h_fwd = _lru_call(*lru_args, None, 0, lru_seqs, LRU_CHUNK, LRU_CB)
    y_lru = _lru_call(*lru_args, h_fwd, 1, lru_seqs, LRU_CHUNK, LRU_CB)

    merged = _proj_merge(y_ret, y_lru, w_ret_o.astype(BF16), w_lru_o.astype(BF16), z)
    x1 = _proj_out(merged, w_out.astype(BF16), xp, xs)

    w_pad = jnp.zeros((D, LANE), F32).at[:, :N_EXP].set(w_router)
    b_pad = jnp.full((1, LANE), -1e30, F32).at[0, :N_EXP].set(b_router)
    w_hi = w_pad.astype(BF16)
    w_lo = (w_pad - w_hi.astype(F32)).astype(BF16)
    xn2, tpos, tpos_t, gates, meta = _router(x1, row(moe_norm_g), w_hi, w_lo, b_pad, TOK_TILE)

    nt = t // TOK_TILE
    c8 = meta[:, 0, :N_EXP].astype(I32)
    toff = meta[:, 1, :N_EXP].astype(I32)
    run = meta[:, 2, :N_EXP].astype(I32)
    total = run[-1] + c8[-1]
    padded = (total + MOE_ROWS - 1) // MOE_ROWS * MOE_ROWS
    pad_ends = jnp.cumsum(padded)
    pad_starts = pad_ends - padded
    src_lists, dst_lists = _copy_lists(toff, c8, pad_starts[None, :] + run)
    zblk = jnp.where(padded > 0, pad_ends - MOE_ROWS, -1)
    n_blocks = (t * TOP_K + (SEG_ALIGN - 1) * nt * N_EXP) // MOE_ROWS + 1 + N_EXP
    n_used = (pad_ends[-1] // MOE_ROWS).astype(I32)
    blk_ids = jnp.minimum(jnp.arange(n_blocks, dtype=I32), n_used - 1)
    block_expert = jnp.minimum(
        jnp.sum((pad_ends[None, :] <= (blk_ids * MOE_ROWS)[:, None]).astype(I32), axis=-1), N_EXP - 1)

    xg = _dispatch(zblk, src_lists, dst_lists, tpos_t, xn2, n_blocks * MOE_ROWS)
    out_sorted = _experts(block_expert, n_used.reshape(1), xg,
                          w_e_gate, b_e_gate[:, None, :], w_e_up, b_e_up[:, None, :],
                          w_e_down, b_e_down[:, None, :], MOE_ROWS, MOE_TF, MOE_TN)
    return _combine(src_lists, dst_lists, out_sorted, tpos, gates, x1, row(final_norm_g), tp, TOK_TILE)


def kernel(x_prompt, x_sample, mix_norm_g, w_in, ret_norm_g, w_ret_o, conv_w, conv_b, lru_w_a, lru_b_a, lru_w_x,
           lru_b_x, lru_lambda, w_lru_o, w_out, moe_norm_g, w_router, b_router, w_e_gate, b_e_gate, w_e_up, b_e_up,
           w_e_down, b_e_down, final_norm_g):
    assert mix_norm_g.shape[0] == 1, "one layer"
    n_prompt, prompt_len, _ = x_prompt.shape
    n_sample, sample_len, _ = x_sample.shape
    assert n_sample == 1
    tp = n_prompt * prompt_len
    yp, ys = _layer(x_prompt.reshape(tp, D), x_sample.reshape(sample_len, D), n_prompt, prompt_len, sample_len,
                    mix_norm_g[0], w_in[0], ret_norm_g[0], w_ret_o[0], conv_w[0],
                    conv_b[0], lru_w_a[0], lru_b_a[0], lru_w_x[0], lru_b_x[0], lru_lambda[0], w_lru_o[0], w_out[0],
                    moe_norm_g[0], w_router[0], b_router[0], w_e_gate[0], b_e_gate[0], w_e_up[0], b_e_up[0],
                    w_e_down[0], b_e_down[0], final_norm_g)
    return yp.reshape(x_prompt.shape), ys.reshape(x_sample.shape)
```

```python
import functools
import math

import jax
import jax.numpy as jnp
from jax import lax
from jax.experimental import pallas as pl
from jax.experimental.pallas import tpu as pltpu

F32 = jnp.float32
BF16 = jnp.bfloat16
U32 = jnp.uint32
I32 = jnp.int32

D = 2048
HEADS = 8
DK = 128
DV = 256
QK_W = HEADS * DK
V_W = HEADS * DV
LRU_BLOCK = 128
N_EXP = 32
TOP_K = 4
D_FF = 2048
SWIGLU_LIMIT = 7.0
SWIGLU_ALPHA = 1.702
ROPE_BASE = 10000.0
LRU_C = 8.0
EPS = 1e-6
IN_W = 2 * QK_W + 2 * V_W + 2 * D + 2 * D

HALF = D // 2
LANE = 128
HALO = 16
HI_MASK = 0xFFFF0000

SEQ_CHUNK = 256
LRU_CHUNK = 512
LRU_CB = 1024
LRU_SEG = 4
LRU_SUPER = 8 * LRU_SEG
MOE_ROWS = 512
MOE_TF = 1024
MOE_TN = 1024
TOK_TILE = 512
SEG_ALIGN = 8
PERM_CHUNK = 256
SORT_ROWS = TOK_TILE * 4 + 32 * SEG_ALIGN
SEG_MAXBIT = (TOK_TILE // SEG_ALIGN).bit_length() - 1


def _cp(sem, vmem_mb):
    return pltpu.CompilerParams(dimension_semantics=sem, vmem_limit_bytes=vmem_mb << 20)


def _bits(x):
    return lax.bitcast_convert_type(x, U32)


def _pack_pair(lo, hi, rounded=False):
    if not rounded:
        lo = lo.astype(BF16).astype(F32)
        hi = hi.astype(BF16).astype(F32)
    return (_bits(lo) >> 16) | (_bits(hi) & jnp.uint32(HI_MASK))


def _unpack_pair(u):
    lo = lax.bitcast_convert_type(u << 16, F32)
    hi = lax.bitcast_convert_type(u & jnp.uint32(HI_MASK), F32)
    return lo, hi


class _Seqs:
    def __init__(self, n_prompt, prompt_len, sample_len, chunk):
        assert prompt_len % chunk == 0 and sample_len % chunk == 0
        self.cps = prompt_len // chunk
        self.npc = n_prompt * self.cps
        self.nch = self.npc + sample_len // chunk

    def is_first(self, ci):
        return ((ci < self.npc) & (ci % self.cps == 0)) | (ci == self.npc)

    def is_last(self, ci):
        return ((ci < self.npc) & (ci % self.cps == self.cps - 1)) | (ci == self.nch - 1)

    def pos_chunk(self, ci):
        return jnp.where(ci < self.npc, ci % self.cps, ci - self.npc)


def _two_group_specs(tm, npt, width=D):
    return [pl.BlockSpec((tm, width), lambda i, j: (jnp.minimum(i, npt - 1), 0)),
            pl.BlockSpec((tm, width), lambda i, j: (jnp.maximum(i - npt, 0), 0))]


def _inproj_kernel(xp_ref, xs_ref, g_ref, w_ref, o_ref, xn_ref, *, npt):
    i = pl.program_id(0)

    def norm(x_ref):
        x = x_ref[...]
        y = x * lax.rsqrt(jnp.mean(x * x, axis=-1, keepdims=True) + EPS)
        xn_ref[...] = (y * g_ref[...]).astype(BF16)

    @pl.when(pl.program_id(1) == 0)
    def _():
        pl.when(i < npt)(lambda: norm(xp_ref))
        pl.when(i >= npt)(lambda: norm(xs_ref))

    o_ref[...] = jnp.dot(xn_ref[...], w_ref[...], preferred_element_type=F32).astype(o_ref.dtype)


def _inproj(xp, xs, g, w, tm=1024, tn=1024):
    t, n = xp.shape[0] + xs.shape[0], w.shape[1]
    npt = xp.shape[0] // tm
    return pl.pallas_call(
        functools.partial(_inproj_kernel, npt=npt),
        grid=(t // tm, n // tn),
        in_specs=_two_group_specs(tm, npt) + [pl.BlockSpec((1, D), lambda i, j: (0, 0)),
                                              pl.BlockSpec((D, tn), lambda i, j: (0, j))],
        out_specs=pl.BlockSpec((tm, tn), lambda i, j: (i, j)),
        out_shape=jax.ShapeDtypeStruct((t, n), BF16),
        scratch_shapes=[pltpu.VMEM((tm, D), BF16)],
        compiler_params=_cp(("arbitrary", "arbitrary"), 56),
        name="inproj",
    )(xp, xs, g, w)


def _rot(x, cosf, sinf):
    return x * cosf + pltpu.roll(x, DK // 2, 1) * sinf


def _log_gamma(h):
    return math.log1p(-(2.0 ** (-5 - h)))


def _decay_tables(dec_ref, c, q_exponent, k_exponent):
    p = lax.broadcasted_iota(I32, (c, DK), 0).astype(F32)
    for h in range(HEADS):
        lg = _log_gamma(h)
        dec_ref[h, 0] = jnp.exp(lg * q_exponent(p))
        dec_ref[h, 1] = jnp.exp(lg * k_exponent(p))


def _ret_bwd_kernel(q_ref, k_ref, v_ref, cos_ref, sin_ref, o_ref, s_ref, dec_ref, *, seqs, c):
    ci = seqs.nch - 1 - pl.program_id(0)

    @pl.when(pl.program_id(0) == 0)
    def _():
        _decay_tables(dec_ref, c, lambda p: c - p, lambda p: p)

    @pl.when(seqs.is_last(ci))
    def _():
        s_ref[...] = jnp.zeros_like(s_ref)

    cosf = cos_ref[...]
    sinf = sin_ref[...]
    for h in range(HEADS):
        lg = _log_gamma(h)
        q = _rot(q_ref[:, h * DK:(h + 1) * DK].astype(F32), cosf, sinf)
        k = _rot(k_ref[:, h * DK:(h + 1) * DK].astype(F32), cosf, sinf) * (DK ** -0.5)
        v = v_ref[:, h * DV:(h + 1) * DV]
        qd = (q * dec_ref[h, 0]).astype(BF16)
        kd = (k * dec_ref[h, 1]).astype(BF16)
        s = s_ref[h]
        o_ref[:, h * DV:(h + 1) * DV] = jnp.dot(qd, s.astype(BF16), preferred_element_type=F32)
        s_ref[h] = math.exp(lg * c) * s + lax.dot_general(
            kd, v, (((0,), (0,)), ((), ())), preferred_element_type=F32)


def _ret_fwd_kernel(q_ref, k_ref, v_ref, cos_ref, sin_ref, bwd_ref, gate_ref, gn_ref, o_ref, s_ref, dm_ref,
                    dec_ref, *, seqs, c):
    ci = pl.program_id(0)

    @pl.when(ci == 0)
    def _():
        _decay_tables(dec_ref, c, lambda p: p + 1.0, lambda p: c - 1.0 - p)
        r = lax.broadcasted_iota(I32, (c, c), 0)
        col = lax.broadcasted_iota(I32, (c, c), 1)
        dist = jnp.abs(r - col).astype(F32)
        for h in range(HEADS):
            dm_ref[h] = jnp.exp(_log_gamma(h) * dist)

    @pl.when(seqs.is_first(ci))
    def _():
        s_ref[...] = jnp.zeros_like(s_ref)

    cosf = cos_ref[...]
    sinf = sin_ref[...]
    for h in range(HEADS):
        lg = _log_gamma(h)
        q = _rot(q_ref[:, h * DK:(h + 1) * DK].astype(F32), cosf, sinf)
        k = _rot(k_ref[:, h * DK:(h + 1) * DK].astype(F32), cosf, sinf) * (DK ** -0.5)
        v = v_ref[:, h * DV:(h + 1) * DV]
        scores = lax.dot_general(q.astype(BF16), k.astype(BF16), (((1,), (1,)), ((), ())),
                                 preferred_element_type=F32) * dm_ref[h]
        tot = jnp.dot(scores.astype(BF16), v, preferred_element_type=F32)
        qd = (q * dec_ref[h, 0]).astype(BF16)
        kd = (k * dec_ref[h, 1]).astype(BF16)
        s = s_ref[h]
        tot = tot + jnp.dot(qd, s.astype(BF16), preferred_element_type=F32)
        s_ref[h] = math.exp(lg * c) * s + lax.dot_general(
            kd, v, (((0,), (0,)), ((), ())), preferred_element_type=F32)
        tot = tot + bwd_ref[:, h * DV:(h + 1) * DV]
        mu = jnp.mean(tot, axis=-1, keepdims=True)
        cen = tot - mu
        var = jnp.mean(cen * cen, axis=-1, keepdims=True)
        yn = cen * lax.rsqrt(var + EPS) * gn_ref[:, h * DV:(h + 1) * DV]
        g = gate_ref[:, h * DV:(h + 1) * DV].astype(F32)
        o_ref[:, h * DV:(h + 1) * DV] = (yn * (g * jax.nn.sigmoid(g))).astype(o_ref.dtype)


def _retention(z, cosf, sinf, gn, seqs, c):
    t = z.shape[0]
    nch = seqs.nch
    rev = lambda i: nch - 1 - i
    qkv_specs = lambda f: [pl.BlockSpec((c, QK_W), lambda i: (f(i), 0)),
                           pl.BlockSpec((c, QK_W), lambda i: (f(i), 1)),
                           pl.BlockSpec((c, V_W), lambda i: (f(i), 1)),
                           pl.BlockSpec((c, DK), lambda i: (seqs.pos_chunk(f(i)), 0)),
                           pl.BlockSpec((c, DK), lambda i: (seqs.pos_chunk(f(i)), 0))]
    bwd = pl.pallas_call(
        functools.partial(_ret_bwd_kernel, seqs=seqs, c=c),
        grid=(nch,),
        in_specs=qkv_specs(rev),
        out_specs=pl.BlockSpec((c, V_W), lambda i: (rev(i), 0)),
        out_shape=jax.ShapeDtypeStruct((t, V_W), F32),
        scratch_shapes=[pltpu.VMEM((HEADS, DK, DV), F32), pltpu.VMEM((HEADS, 2, c, DK), F32)],
        compiler_params=_cp(("arbitrary",), 32),
        name="ret_bwd",
    )(z, z, z, cosf, sinf)
    fwd_id = lambda i: i
    return pl.pallas_call(
        functools.partial(_ret_fwd_kernel, seqs=seqs, c=c),
        grid=(nch,),
        in_specs=qkv_specs(fwd_id) + [pl.BlockSpec((c, V_W), lambda i: (i, 0)),
                                      pl.BlockSpec((c, V_W), lambda i: (i, 2)),
                                      pl.BlockSpec((1, V_W), lambda i: (0, 0))],
        out_specs=pl.BlockSpec((c, V_W), lambda i: (i, 0)),
        out_shape=jax.ShapeDtypeStruct((t, V_W), BF16),
        scratch_shapes=[pltpu.VMEM((HEADS, DK, DV), F32), pltpu.VMEM((HEADS, c, c), F32),
                        pltpu.VMEM((HEADS, 2, c, DK), F32)],
        compiler_params=_cp(("arbitrary",), 40),
        name="ret_fwd",
    )(z, z, z, cosf, sinf, bwd, z, gn)


def _lru_kernel(*refs, reverse, seqs, c, cb):
    if reverse:
        (x_ref, xp_ref, xn_ref, cw_ref, cbias_ref, wa_ref, ba_ref, wx_ref, bx_ref, c8_ref, hf_ref, g_ref,
         o_ref, a_s, u_s, h_s, carry_s) = refs
    else:
        (x_ref, xp_ref, xn_ref, cw_ref, cbias_ref, wa_ref, ba_ref, wx_ref, bx_ref, c8_ref,
         o_ref, a_s, u_s, h_s, carry_s) = refs
    t = pl.program_id(1)
    ci = seqs.nch - 1 - t if reverse else t
    first = seqs.is_first(ci)
    last = seqs.is_last(ci)

    x = x_ref[...].astype(F32)
    prev = jnp.where(first, 0.0, xp_ref[...].astype(F32))
    nxt = jnp.where(last, 0.0, xn_ref[...].astype(F32))
    row8 = lax.broadcasted_iota(I32, (8, 1), 0)

    def patch(arr, at, rows8):
        parts = ([arr[:at]] if at > 0 else []) + [rows8] + ([arr[at + 8:]] if at + 8 < c else [])
        return jnp.concatenate(parts, axis=0)

    xm1 = pltpu.roll(x, 1, 0)
    xm1 = patch(xm1, 0, jnp.where(row8 == 0, prev[HALO - 1:HALO], xm1[0:8]))
    xm2 = pltpu.roll(x, 2, 0)
    xm2 = patch(xm2, 0, jnp.where(row8 == 0, prev[HALO - 2:HALO - 1],
                                  jnp.where(row8 == 1, prev[HALO - 1:HALO], xm2[0:8])))
    xp1 = pltpu.roll(x, c - 1, 0)
    xp1 = patch(xp1, c - 8, jnp.where(row8 == 7, nxt[0:1], xp1[c - 8:c]))
    cw = cw_ref[...]
    xc = cw[0:1] * xm2 + cw[1:2] * xm1 + cw[2:3] * x + cw[3:4] * xp1 + cbias_ref[...]

    nslab = cb // LRU_BLOCK
    for gi in range(nslab):
        sl = slice(gi * LRU_BLOCK, (gi + 1) * LRU_BLOCK)
        xs = xc[:, sl]
        xb = xs.astype(BF16)
        r = jax.nn.sigmoid(jnp.dot(xb, wa_ref[gi], preferred_element_type=F32) + ba_ref[:, sl])
        ig = jax.nn.sigmoid(jnp.dot(xb, wx_ref[gi], preferred_element_type=F32) + bx_ref[:, sl])
        a = jnp.exp2(c8_ref[:, sl] * r)
        a_s[gi] = a
        v = 1.0 - a * a
        u_s[gi] = jnp.where(v > 0.0, v * lax.rsqrt(v), 0.0) * (ig * xs)

    @pl.when(last if reverse else first)
    def _():
        carry_s[...] = jnp.zeros_like(carry_s)

    sub = lax.broadcasted_iota(I32, (8, LRU_BLOCK), 0)
    nsuper = c // LRU_SUPER
    steps = list(range(LRU_SEG))
    if reverse:
        steps = steps[::-1]

    def super_group(q, carries):
        qi = nsuper - 1 - q if reverse else q
        base = qi * LRU_SUPER
        out = []
        for gi in range(nslab):
            hs, ps = {}, {}
            h = p = None
            for j in steps:
                rows = pl.ds(base + j, 8, stride=LRU_SEG)
                a = a_s[gi, rows, :]
                u = u_s[gi, rows, :]
                h = u if h is None else a * h + u
                p = a if p is None else a * p
                hs[j], ps[j] = h, p
            eh, ep = h, p
            for s in (1, 2, 4):
                shift = 8 - s if reverse else s
                m = (sub < 8 - s) if reverse else (sub >= s)
                eh_sh = pltpu.roll(eh, shift, 0)
                ep_sh = pltpu.roll(ep, shift, 0)
                eh = eh + ep * jnp.where(m, eh_sh, 0.0)
                ep = ep * jnp.where(m, ep_sh, 1.0)
            end = eh + ep * carries[gi]
            if reverse:
                enter = jnp.where(sub == 7, carries[gi], pltpu.roll(end, 7, 0))
                out.append(jnp.broadcast_to(end[0:1], (8, LRU_BLOCK)))
            else:
                enter = jnp.where(sub == 0, carries[gi], pltpu.roll(end, 1, 0))
                out.append(jnp.broadcast_to(end[7:8], (8, LRU_BLOCK)))
            for j in steps:
                h_s[gi, pl.ds(base + j, 8, stride=LRU_SEG), :] = hs[j] + ps[j] * enter
        return tuple(out)

    carries = lax.fori_loop(0, nsuper, super_group, tuple(carry_s[gi] for gi in range(nslab)))
    for gi in range(nslab):
        carry_s[gi] = carries[gi]

    for gi in range(nslab):
        sl = slice(gi * LRU_BLOCK, (gi + 1) * LRU_BLOCK)
        if reverse:
            g = g_ref[:, sl].astype(F32)
            o_ref[:, sl] = ((hf_ref[:, sl] + h_s[gi]) * jax.nn.gelu(g, approximate=True)).astype(o_ref.dtype)
        else:
            o_ref[:, sl] = h_s[gi]


def _lru_call(z, conv_w, conv_b, wa, ba, wx, bx, c8, hf, direction, seqs, c, cb):
    t = z.shape[0]
    nch = seqs.nch
    reverse = direction == 1
    tmap = (lambda ti: nch - 1 - ti) if reverse else (lambda ti: ti)
    xcol = (2 * QK_W + 2 * V_W) // cb
    gcol = (2 * QK_W + 2 * V_W + D) // cb
    hb = c // HALO
    nhb = t // HALO
    nb = cb // LRU_BLOCK
    in_specs = [
        pl.BlockSpec((c, cb), lambda ch, ti: (tmap(ti), xcol + ch)),
        pl.BlockSpec((HALO, cb), lambda ch, ti: (jnp.maximum(tmap(ti) * hb - 1, 0), xcol + ch)),
        pl.BlockSpec((HALO, cb), lambda ch, ti: (jnp.minimum((tmap(ti) + 1) * hb, nhb - 1), xcol + ch)),
        pl.BlockSpec((4, cb), lambda ch, ti: (0, ch)),
        pl.BlockSpec((1, cb), lambda ch, ti: (0, ch)),
        pl.BlockSpec((None, nb, LRU_BLOCK, LRU_BLOCK), lambda ch, ti: (direction, ch, 0, 0)),
        pl.BlockSpec((None, 1, cb), lambda ch, ti: (direction, 0, ch)),
        pl.BlockSpec((None, nb, LRU_BLOCK, LRU_BLOCK), lambda ch, ti: (direction, ch, 0, 0)),
        pl.BlockSpec((None, 1, cb), lambda ch, ti: (direction, 0, ch)),
        pl.BlockSpec((None, 1, cb), lambda ch, ti: (direction, 0, ch)),
    ]
    args = [z, z, z, conv_w, conv_b, wa, ba, wx, bx, c8]
    if reverse:
        in_specs += [pl.BlockSpec((c, cb), lambda ch, ti: (tmap(ti), ch)),
                     pl.BlockSpec((c, cb), lambda ch, ti: (tmap(ti), gcol + ch))]
        args += [hf, z]
    return pl.pallas_call(
        functools.partial(_lru_kernel, reverse=reverse, seqs=seqs, c=c, cb=cb),
        grid=(D // cb, nch),
        in_specs=in_specs,
        out_specs=pl.BlockSpec((c, cb), lambda ch, ti: (tmap(ti), ch)),
        out_shape=jax.ShapeDtypeStruct((t, D), BF16 if reverse else F32),
        scratch_shapes=[pltpu.VMEM((nb, c, LRU_BLOCK), F32), pltpu.VMEM((nb, c, LRU_BLOCK), F32),
                        pltpu.VMEM((nb, c, LRU_BLOCK), F32), pltpu.VMEM((nb, 8, LRU_BLOCK), F32)],
        compiler_params=_cp(("arbitrary", "arbitrary"),32),
        name="lru_rev" if reverse else "lru_fwd",
    )(*args)


def _proj_merge_kernel(yr_ref, yl_ref, wr_ref, wl_ref, mr_ref, ml_ref, o_ref):
    r = jnp.dot(yr_ref[...], wr_ref[...], preferred_element_type=F32)
    l = jnp.dot(yl_ref[...], wl_ref[...], preferred_element_type=F32)
    o_ref[...] = (jax.nn.sigmoid(mr_ref[...].astype(F32)) * r
                  + jax.nn.sigmoid(ml_ref[...].astype(F32)) * l).astype(o_ref.dtype)


def _proj_merge(y_ret, y_lru, w_ret_o, w_lru_o, z, tm=1024, tn=1024):
    t = y_ret.shape[0]
    mcol = (2 * QK_W + 2 * V_W + 2 * D) // tn
    return pl.pallas_call(
        _proj_merge_kernel,
        grid=(D // tn, t // tm),
        in_specs=[pl.BlockSpec((tm, V_W), lambda j, i: (i, 0)),
                  pl.BlockSpec((tm, D), lambda j, i: (i, 0)),
                  pl.BlockSpec((V_W, tn), lambda j, i: (0, j)),
                  pl.BlockSpec((D, tn), lambda j, i: (0, j)),
                  pl.BlockSpec((tm, tn), lambda j, i: (i, mcol + j)),
                  pl.BlockSpec((tm, tn), lambda j, i: (i, mcol + D // tn + j))],
        out_specs=pl.BlockSpec((tm, tn), lambda j, i: (i, j)),
        out_shape=jax.ShapeDtypeStruct((t, D), BF16),
        compiler_params=_cp(("arbitrary", "arbitrary"), 56),
        name="proj_merge",
    )(y_ret, y_lru, w_ret_o, w_lru_o, z, z)


def _proj_out_kernel(m_ref, w_ref, xp_ref, xs_ref, o_ref, *, npt):
    i = pl.program_id(1)
    y = jnp.dot(m_ref[...], w_ref[...], preferred_element_type=F32)

    @pl.when(i < npt)
    def _():
        o_ref[...] = xp_ref[...] + y

    @pl.when(i >= npt)
    def _():
        o_ref[...] = xs_ref[...] + y


def _proj_out(merged, w_out, xp, xs, tm=1024, tn=1024):
    t = merged.shape[0]
    npt = xp.shape[0] // tm
    return pl.pallas_call(
        functools.partial(_proj_out_kernel, npt=npt),
        grid=(D // tn, t // tm),
        in_specs=[pl.BlockSpec((tm, D), lambda j, i: (i, 0)),
                  pl.BlockSpec((D, tn), lambda j, i: (0, j)),
                  pl.BlockSpec((tm, tn), lambda j, i: (jnp.minimum(i, npt - 1), j)),
                  pl.BlockSpec((tm, tn), lambda j, i: (jnp.maximum(i - npt, 0), j))],
        out_specs=pl.BlockSpec((tm, tn), lambda j, i: (i, j)),
        out_shape=jax.ShapeDtypeStruct((t, D), F32),
        compiler_params=_cp(("arbitrary", "arbitrary"), 48),
        name="proj_out",
    )(merged, w_out, xp, xs)


def _router_kernel(x_ref, g_ref, whi_ref, wlo_ref, b_ref, xn_ref, tpos_ref, tpos_t_ref, gate_ref, meta_ref, tri_ref,
                   run_ref, *, tm):
    i = pl.program_id(0)

    @pl.when(i == 0)
    def _():
        r = lax.broadcasted_iota(I32, (tm, tm), 0)
        col = lax.broadcasted_iota(I32, (tm, tm), 1)
        tri_ref[...] = (col < r).astype(BF16)
        run_ref[...] = jnp.zeros_like(run_ref)

    x = x_ref[...]
    xn = x * lax.rsqrt(jnp.mean(x * x, axis=-1, keepdims=True) + EPS) * g_ref[...]
    xh = xn.astype(BF16)
    xn_ref[...] = xh

    xl = (xn - xh.astype(F32)).astype(BF16)
    logits = (jnp.dot(xh, whi_ref[...], preferred_element_type=F32)
              + (jnp.dot(xl, whi_ref[...], preferred_element_type=F32)
                 + jnp.dot(xh, wlo_ref[...], preferred_element_type=F32))) + b_ref[...]
    lane = lax.broadcasted_iota(I32, (tm, LANE), 1)
    lane_f = lane.astype(F32)
    vals, idxs = [], []
    cur = logits
    for _ in range(TOP_K):
        m = jnp.max(cur, axis=-1, keepdims=True)
        idx = jnp.min(jnp.where(cur == m, lane_f, float(LANE)), axis=-1, keepdims=True).astype(I32)
        vals.append(m)
        idxs.append(idx)
        cur = jnp.where(lane == idx, -jnp.inf, cur)
    exps = [jnp.exp(v - vals[0]) for v in vals]
    denom = exps[0] + exps[1] + exps[2] + exps[3]

    onehot = jnp.zeros((tm, LANE), F32)
    for idx in idxs:
        onehot = onehot + (lane == idx).astype(F32)
    before = jnp.dot(tri_ref[...], onehot.astype(BF16), preferred_element_type=F32)

    cnt = jnp.broadcast_to(jnp.sum(onehot, axis=0, keepdims=True), (8, LANE))
    cnt8 = jnp.floor((cnt + (SEG_ALIGN - 1)) * (1.0 / SEG_ALIGN)) * SEG_ALIGN
    lane8 = lax.broadcasted_iota(I32, (8, LANE), 1)
    incl = cnt8
    s = 1
    while s < LANE:
        incl = incl + jnp.where(lane8 >= s, pltpu.roll(incl, s, 1), 0.0)
        s *= 2
    toff = incl - cnt8

    pos = before + toff[0:1]
    tpos_out = jnp.zeros((tm, LANE), F32)
    gate_out = jnp.zeros((tm, LANE), F32)
    for k in range(TOP_K):
        tpos_k = jnp.sum(jnp.where(lane == idxs[k], pos, 0.0), axis=-1, keepdims=True)
        tpos_out = jnp.where(lane == k, tpos_k, tpos_out)
        gate_out = jnp.where(lane == k, exps[k] / denom, gate_out)
    tpos_ref[...] = tpos_out.astype(I32)
    tpos_t_ref[...] = tpos_out.T[0:8].astype(I32)
    gate_ref[...] = gate_out

    sub8 = lax.broadcasted_iota(I32, (8, LANE), 0)
    run = jnp.broadcast_to(run_ref[...], (8, LANE))
    meta_ref[...] = jnp.where(sub8 == 0, cnt8, jnp.where(sub8 == 1, toff, jnp.where(sub8 == 2, run, 0.0)))
    run_ref[...] = run_ref[...] + cnt8[0:1]


def _router(x1, g, w_hi, w_lo, b_pad, tm):
    t = x1.shape[0]
    nt = t // tm
    tok = lambda i: (i, 0)
    fixed = lambda i: (0, 0)
    per_tile = lambda i: (i, 0, 0)
    return pl.pallas_call(
        functools.partial(_router_kernel, tm=tm),
        grid=(nt,),
        in_specs=[pl.BlockSpec((tm, D), tok), pl.BlockSpec((1, D), fixed),
                  pl.BlockSpec((D, LANE), fixed), pl.BlockSpec((D, LANE), fixed), pl.BlockSpec((1, LANE), fixed)],
        out_specs=[pl.BlockSpec((tm, D), tok), pl.BlockSpec((tm, LANE), tok),
                   pl.BlockSpec((None, 8, tm), per_tile), pl.BlockSpec((tm, LANE), tok),
                   pl.BlockSpec((None, 8, LANE), per_tile)],
        out_shape=[jax.ShapeDtypeStruct((t, D), BF16), jax.ShapeDtypeStruct((t, LANE), I32),
                   jax.ShapeDtypeStruct((nt, 8, tm), I32), jax.ShapeDtypeStruct((t, LANE), F32),
                   jax.ShapeDtypeStruct((nt, 8, LANE), F32)],
        scratch_shapes=[pltpu.VMEM((tm, tm), BF16), pltpu.VMEM((1, LANE), F32)],
        compiler_params=_cp(("arbitrary",), 32),
        name="router",
    )(x1, g, w_hi, w_lo, b_pad)


N_CLASS = SEG_MAXBIT + 1
USED_LANE = N_CLASS + 1


def _copy_lists(toff, c8, dst):
    nt = c8.shape[0]
    n = (c8 // SEG_ALIGN)[:, None, :]
    b = jnp.arange(N_CLASS, dtype=I32)[None, :, None]
    bits = (n >> b) & 1
    off = ((n >> (b + 1)) << (b + 1)) * SEG_ALIGN
    order = jnp.argsort(1 - bits, axis=-1, stable=True)
    src = jnp.take_along_axis(toff[:, None, :] + off, order, axis=-1)
    dstl = jnp.take_along_axis(dst[:, None, :] + off, order, axis=-1)
    src_t = jnp.zeros((nt, 8, LANE), I32).at[:, :N_CLASS, :N_EXP].set(src)
    src_t = src_t.at[:, N_CLASS, :N_CLASS].set(jnp.sum(bits, axis=-1))
    src_t = src_t.at[:, N_CLASS, USED_LANE].set(toff[:, -1] + c8[:, -1])
    dst_t = jnp.zeros((nt, 8, LANE), I32).at[:, :N_CLASS, :N_EXP].set(dstl)
    return src_t, dst_t


def _class_copies(src_ref, dst_ref, make_copy, act):
    for b in range(N_CLASS):
        rows = SEG_ALIGN << b

        def body(s, carry, b=b, rows=rows):
            act(make_copy(pl.multiple_of(src_ref[b, s], SEG_ALIGN), pl.multiple_of(dst_ref[b, s], SEG_ALIGN), rows))
            return carry

        lax.fori_loop(0, src_ref[N_CLASS, b], body, 0)


def _dispatch_kernel(zblk_ref, src_ref, dst_ref, src_prev_ref, dst_prev_ref, tpos_t_ref, x_ref, xg_ref, sorted_ref,
                     zero_ref, sem, *, nt):
    i = pl.program_id(0)
    slot = i % 2

    def zero_copy(e):
        start = pl.multiple_of(jnp.maximum(zblk_ref[e], 0), MOE_ROWS)
        return pltpu.make_async_copy(zero_ref, xg_ref.at[pl.ds(start, MOE_ROWS), :], sem.at[0])

    def for_nonempty(act):
        def body(e, carry):
            pl.when(zblk_ref[e] >= 0)(lambda: act(zero_copy(e)))
            return carry
        lax.fori_loop(0, N_EXP, body, 0)

    def for_unused(act):
        def body(b, carry):
            start = pl.multiple_of(b * MOE_ROWS, MOE_ROWS)
            act(pltpu.make_async_copy(zero_ref, xg_ref.at[pl.ds(start, MOE_ROWS), :], sem.at[0]))
            return carry
        lax.fori_loop(zblk_ref[N_EXP], xg_ref.shape[0] // MOE_ROWS, body, 0)

    @pl.when(i == 0)
    def _():
        zero_ref[...] = jnp.zeros_like(zero_ref)
        for_nonempty(lambda cp: cp.start())
        for_unused(lambda cp: cp.start())
        for_nonempty(lambda cp: cp.wait())
        for_unused(lambda cp: cp.wait())

    x = x_ref[...]
    for c in range(SORT_ROWS // PERM_CHUNK):
        p = c * PERM_CHUNK + lax.broadcasted_iota(I32, (PERM_CHUNK, 1), 0)
        perm = jnp.zeros((PERM_CHUNK, x.shape[0]), F32)
        for k in range(TOP_K):
            perm = perm + jnp.where(p == tpos_t_ref[k:k + 1, :], 1.0, 0.0)
        rows = jnp.dot(perm.astype(BF16), x, preferred_element_type=F32)
        sorted_ref[slot, c * PERM_CHUNK:(c + 1) * PERM_CHUNK, :] = _pack_pair(rows[:, :HALF], rows[:, HALF:],
                                                                              rounded=True)

    def copies(lists, buf, act):
        def make_copy(t_off, d, rows):
            return pltpu.make_async_copy(sorted_ref.at[buf, pl.ds(t_off, rows), :], xg_ref.at[pl.ds(d, rows), :],
                                         sem.at[buf])
        _class_copies(*lists, make_copy, act)

    this_tile = (src_ref, dst_ref)
    pl.when(i > 0)(lambda: copies((src_prev_ref, dst_prev_ref), 1 - slot, lambda cp: cp.wait()))
    copies(this_tile, slot, lambda cp: cp.start())
    pl.when(i == nt - 1)(lambda: copies(this_tile, slot, lambda cp: cp.wait()))


def _dispatch(zblk, src_lists, dst_lists, tpos_t, xn, rows):
    nt, _, tm = tpos_t.shape
    this_tile = lambda i, *_: (i, 0, 0)
    prev_tile = lambda i, *_: (jnp.maximum(i - 1, 0), 0, 0)
    lists = lambda index_map: pl.BlockSpec((None, 8, LANE), index_map, memory_space=pltpu.SMEM)
    grid_spec = pltpu.PrefetchScalarGridSpec(
        num_scalar_prefetch=1,
        grid=(nt,),
        in_specs=[lists(this_tile), lists(this_tile), lists(prev_tile), lists(prev_tile),
                  pl.BlockSpec((None, 8, tm), this_tile),
                  pl.BlockSpec((tm, D), lambda i, *_: (i, 0))],
        out_specs=pl.BlockSpec(memory_space=pl.ANY),
        scratch_shapes=[pltpu.VMEM((2, SORT_ROWS, HALF), U32), pltpu.VMEM((MOE_ROWS, HALF), U32),
                        pltpu.SemaphoreType.DMA((2,))],
    )
    return pl.pallas_call(
        functools.partial(_dispatch_kernel, nt=nt),
        grid_spec=grid_spec,
        out_shape=jax.ShapeDtypeStruct((rows, HALF), U32),
        compiler_params=_cp(("arbitrary",), 40),
        name="dispatch",
    )(zblk, src_lists, dst_lists, src_lists, dst_lists, tpos_t, xn)


def _expert_changed(be_ref, i):
    return (i == 0) | (be_ref[i] != be_ref[jnp.maximum(i - 1, 0)])


def _expert_up_kernel(be_ref, nu_ref, xg_ref, wg_ref, bg_ref, wu_ref, bu_ref, h_ref, wgb_ref, wub_ref):
    i = pl.program_id(1)

    @pl.when(i < nu_ref[0])
    def _():
        @pl.when(_expert_changed(be_ref, i))
        def _():
            wgb_ref[...] = wg_ref[...].astype(BF16)
            wub_ref[...] = wu_ref[...].astype(BF16)

        lo, hi = _unpack_pair(xg_ref[...])
        xb = jnp.concatenate([lo.astype(BF16), hi.astype(BF16)], axis=-1)
        gate = jnp.minimum(jnp.dot(xb, wgb_ref[...], preferred_element_type=F32) + bg_ref[...], SWIGLU_LIMIT)
        up = jnp.clip(jnp.dot(xb, wub_ref[...], preferred_element_type=F32) + bu_ref[...],
                      -SWIGLU_LIMIT, SWIGLU_LIMIT)
        h_ref[...] = ((up + 1.0) * gate * jax.nn.sigmoid(SWIGLU_ALPHA * gate)).astype(h_ref.dtype)

    @pl.when(i >= nu_ref[0])
    def _():
        h_ref[...] = jnp.zeros_like(h_ref)


def _expert_down_kernel(be_ref, nu_ref, h_ref, wlo_ref, whi_ref, blo_ref, bhi_ref, o_ref, wlob_ref, whib_ref):
    i = pl.program_id(1)

    @pl.when(i < nu_ref[0])
    def _():
        @pl.when(_expert_changed(be_ref, i))
        def _():
            wlob_ref[...] = wlo_ref[...].astype(BF16)
            whib_ref[...] = whi_ref[...].astype(BF16)

        h = h_ref[...]
        lo = jnp.dot(h, wlob_ref[...], preferred_element_type=F32) + blo_ref[...]
        hi = jnp.dot(h, whib_ref[...], preferred_element_type=F32) + bhi_ref[...]
        o_ref[...] = _pack_pair(lo, hi)

    @pl.when(i >= nu_ref[0])
    def _():
        o_ref[...] = jnp.zeros_like(o_ref)


def _experts(block_expert, n_used, xg, wg, bg, wu, bu, wd, bd, rows_blk, tf, tn):
    rows = xg.shape[0]
    nb = rows // rows_blk
    nj = D_FF // tf
    nn = HALF // tn

    def blk(i, nu):
        return jnp.minimum(i, nu[0] - 1)

    up_spec = pltpu.PrefetchScalarGridSpec(
        num_scalar_prefetch=2,
        grid=(nj, nb),
        in_specs=[pl.BlockSpec((rows_blk, HALF), lambda j, i, be, nu: (blk(i, nu), 0)),
                  pl.BlockSpec((None, D, tf), lambda j, i, be, nu: (be[i], 0, j)),
                  pl.BlockSpec((None, 1, tf), lambda j, i, be, nu: (be[i], 0, j)),
                  pl.BlockSpec((None, D, tf), lambda j, i, be, nu: (be[i], 0, j)),
                  pl.BlockSpec((None, 1, tf), lambda j, i, be, nu: (be[i], 0, j))],
        out_specs=pl.BlockSpec((rows_blk, tf), lambda j, i, be, nu: (i, j)),
        scratch_shapes=[pltpu.VMEM((D, tf), BF16), pltpu.VMEM((D, tf), BF16)],
    )
    hid = pl.pallas_call(
        _expert_up_kernel,
        grid_spec=up_spec,
        out_shape=jax.ShapeDtypeStruct((rows, D_FF), BF16),
        compiler_params=_cp(("arbitrary", "arbitrary"), 60),
        name="experts_up",
    )(block_expert, n_used, xg, wg, bg, wu, bu)

    down_spec = pltpu.PrefetchScalarGridSpec(
        num_scalar_prefetch=2,
        grid=(nn, nb),
        in_specs=[pl.BlockSpec((rows_blk, D_FF), lambda n, i, be, nu: (blk(i, nu), 0)),
                  pl.BlockSpec((None, D_FF, tn), lambda n, i, be, nu: (be[i], 0, n)),
                  pl.BlockSpec((None, D_FF, tn), lambda n, i, be, nu: (be[i], 0, nn + n)),
                  pl.BlockSpec((None, 1, tn), lambda n, i, be, nu: (be[i], 0, n)),
                  pl.BlockSpec((None, 1, tn), lambda n, i, be, nu: (be[i], 0, nn + n))],
        out_specs=pl.BlockSpec((rows_blk, tn), lambda n, i, be, nu: (i, n)),
        scratch_shapes=[pltpu.VMEM((D_FF, tn), BF16), pltpu.VMEM((D_FF, tn), BF16)],
    )
    return pl.pallas_call(
        _expert_down_kernel,
        grid_spec=down_spec,
        out_shape=jax.ShapeDtypeStruct((rows, HALF), U32),
        compiler_params=_cp(("arbitrary", "arbitrary"), 60),
        name="experts_down",
    )(block_expert, n_used, hid, wd, wd, bd, bd)


def _combine_kernel(src_ref, dst_ref, src_next_ref, dst_next_ref, os_ref, tpos_ref, gate_ref, x1_ref, fg_ref,
                    op_ref, osm_ref, sorted_ref, sem, *, npt, nt):
    i = pl.program_id(0)
    slot = i % 2

    def copies(lists, buf, act):
        def make_copy(t_off, d, rows):
            return pltpu.make_async_copy(os_ref.at[pl.ds(d, rows), :], sorted_ref.at[buf, pl.ds(t_off, rows), :],
                                         sem.at[buf])
        _class_copies(*lists, make_copy, act)

    this_tile = (src_ref, dst_ref)
    pl.when(i == 0)(lambda: copies(this_tile, 0, lambda cp: cp.start()))
    pl.when(i + 1 < nt)(lambda: copies((src_next_ref, dst_next_ref), 1 - slot, lambda cp: cp.start()))
    copies(this_tile, slot, lambda cp: cp.wait())

    used = src_ref[N_CLASS, USED_LANE]
    tpos = tpos_ref[...]
    gates = gate_ref[...]
    y_lo = x1_ref[:, :HALF]
    y_hi = x1_ref[:, HALF:]
    for c in range(SORT_ROWS // PERM_CHUNK):
        p_lane = c * PERM_CHUNK + lax.broadcasted_iota(I32, (1, PERM_CHUNK), 1)
        place = jnp.zeros((tpos.shape[0], PERM_CHUNK), F32)
        for k in range(TOP_K):
            place = jnp.where(tpos[:, k:k + 1] == p_lane, gates[:, k:k + 1], place)
        p_row = c * PERM_CHUNK + lax.broadcasted_iota(I32, (PERM_CHUNK, 1), 0)
        u = jnp.where(p_row < used, sorted_ref[slot, c * PERM_CHUNK:(c + 1) * PERM_CHUNK, :], jnp.uint32(0))
        lo, hi = _unpack_pair(u)
        place = place.astype(BF16)
        y_lo = y_lo + jnp.dot(place, lo.astype(BF16), preferred_element_type=F32)
        y_hi = y_hi + jnp.dot(place, hi.astype(BF16), preferred_element_type=F32)
    ms = (jnp.sum(y_lo * y_lo, axis=-1, keepdims=True) + jnp.sum(y_hi * y_hi, axis=-1, keepdims=True)) / D
    inv = lax.rsqrt(ms + EPS)
    out = jnp.concatenate([y_lo * inv * fg_ref[:, :HALF], y_hi * inv * fg_ref[:, HALF:]], axis=-1)

    @pl.when(i < npt)
    def _():
        op_ref[...] = out

    @pl.when(i >= npt)
    def _():
        osm_ref[...] = out


def _combine(src_lists, dst_lists, out_sorted, tpos, gates, x1, fg, tp, tm):
    t = x1.shape[0]
    npt = tp // tm
    nt = t // tm
    tok = lambda i: (i, 0)
    this_tile = lambda i: (i, 0, 0)
    next_tile = lambda i: (jnp.minimum(i + 1, nt - 1), 0, 0)
    lists = lambda index_map: pl.BlockSpec((None, 8, LANE), index_map, memory_space=pltpu.SMEM)
    return pl.pallas_call(
        functools.partial(_combine_kernel, npt=npt, nt=nt),
        grid=(nt,),
        in_specs=[lists(this_tile), lists(this_tile), lists(next_tile), lists(next_tile),
                  pl.BlockSpec(memory_space=pl.ANY),
                  pl.BlockSpec((tm, LANE), tok), pl.BlockSpec((tm, LANE), tok), pl.BlockSpec((tm, D), tok),
                  pl.BlockSpec((1, D), lambda i: (0, 0))],
        out_specs=[pl.BlockSpec((tm, D), lambda i: (jnp.minimum(i, npt - 1), 0)),
                   pl.BlockSpec((tm, D), lambda i: (jnp.maximum(i - npt, 0), 0))],
        out_shape=[jax.ShapeDtypeStruct((tp, D), F32), jax.ShapeDtypeStruct((t - tp, D), F32)],
        scratch_shapes=[pltpu.VMEM((2, SORT_ROWS, HALF), U32), pltpu.SemaphoreType.DMA((2,))],
        compiler_params=_cp(("arbitrary",), 56),
        name="combine",
    )(src_lists, dst_lists, src_lists, dst_lists, out_sorted, tpos, gates, x1, fg)


def _rope_tables(length):
    half = DK // 2
    inv = ROPE_BASE ** (-jnp.arange(half, dtype=F32) / half)
    ang = jnp.arange(length, dtype=F32)[:, None] * inv[None, :]
    cos, sin = jnp.cos(ang), jnp.sin(ang)
    return jnp.concatenate([cos, cos], axis=-1), jnp.concatenate([-sin, sin], axis=-1)


def _layer(xp, xs, n_prompt, prompt_len, sample_len, mix_norm_g, w_in, ret_norm_g, w_ret_o, conv_w, conv_b, lru_w_a,
           lru_b_a, lru_w_x, lru_b_x, lru_lambda, w_lru_o, w_out, moe_norm_g, w_router, b_router, w_e_gate,
           b_e_gate, w_e_up, b_e_up, w_e_down, b_e_down, final_norm_g):
    tp = xp.shape[0]
    t = tp + xs.shape[0]
    seqs = _Seqs(n_prompt, prompt_len, sample_len, SEQ_CHUNK)
    row = lambda v: v.reshape(1, -1)

    z = _inproj(xp, xs, row(mix_norm_g), w_in.astype(BF16))

    cosf, sinf = _rope_tables(max(prompt_len, sample_len))
    y_ret = _retention(z, cosf, sinf, row(ret_norm_g), seqs, SEQ_CHUNK)

    wa, wx = lru_w_a.astype(BF16), lru_w_x.astype(BF16)
    ba, bx = lru_b_a[:, None, :], lru_b_x[:, None, :]
    c8 = (-LRU_C * math.log2(math.e) * jax.nn.softplus(-lru_lambda))[:, None, :]
    lru_args = (z, conv_w, row(conv_b), wa, ba, wx, bx, c8)
    lru_seqs = _Seqs(n_prompt, prompt_len, sample_len, LRU_CHUNK)
    h_fwd = _lru_call(*lru_args, None, 0, lru_seqs, LRU_CHUNK, LRU_CB)
    y_lru = _lru_call(*lru_args, h_fwd, 1, lru_seqs, LRU_CHUNK, LRU_CB)

    merged = _proj_merge(y_ret, y_lru, w_ret_o.astype(BF16), w_lru_o.astype(BF16), z)
    x1 = _proj_out(merged, w_out.astype(BF16), xp, xs)

    w_pad = jnp.zeros((D, LANE), F32).at[:, :N_EXP].set(w_router)
    b_pad = jnp.full((1, LANE), -1e30, F32).at[0, :N_EXP].set(b_router)
    w_hi = w_pad.astype(BF16)
    w_lo = (w_pad - w_hi.astype(F32)).astype(BF16)
    xn2, tpos, tpos_t, gates, meta = _router(x1, row(moe_norm_g), w_hi, w_lo, b_pad, TOK_TILE)

    nt = t // TOK_TILE
    c8 = meta[:, 0, :N_EXP].astype(I32)
    toff = meta[:, 1, :N_EXP].astype(I32)
    run = meta[:, 2, :N_EXP].astype(I32)
    total = run[-1] + c8[-1]
    padded = (total + MOE_ROWS - 1) // MOE_ROWS * MOE_ROWS
    pad_ends = jnp.cumsum(padded)
    pad_starts = pad_ends - padded
    src_lists, dst_lists = _copy_lists(toff, c8, pad_starts[None, :] + run)
    n_blocks = (t * TOP_K + (SEG_ALIGN - 1) * nt * N_EXP) // MOE_ROWS + 1 + N_EXP
    n_used = (pad_ends[-1] // MOE_ROWS).astype(I32)
    zblk = jnp.concatenate([jnp.where(padded > 0, pad_ends - MOE_ROWS, -1), n_used.reshape(1)]).astype(I32)
    blk_ids = jnp.minimum(jnp.arange(n_blocks, dtype=I32), n_used - 1)
    block_expert = jnp.minimum(
        jnp.sum((pad_ends[None, :] <= (blk_ids * MOE_ROWS)[:, None]).astype(I32), axis=-1), N_EXP - 1)

    xg = _dispatch(zblk, src_lists, dst_lists, tpos_t, xn2, n_blocks * MOE_ROWS)
    out_sorted = _experts(block_expert, n_used.reshape(1), xg,
                          w_e_gate, b_e_gate[:, None, :], w_e_up, b_e_up[:, None, :],
                          w_e_down, b_e_down[:, None, :], MOE_ROWS, MOE_TF, MOE_TN)
    return _combine(src_lists, dst_lists, out_sorted, tpos, gates, x1, row(final_norm_g), tp, TOK_TILE)


def kernel(x_prompt, x_sample, mix_norm_g, w_in, ret_norm_g, w_ret_o, conv_w, conv_b, lru_w_a, lru_b_a, lru_w_x,
           lru_b_x, lru_lambda, w_lru_o, w_out, moe_norm_g, w_router, b_router, w_e_gate, b_e_gate, w_e_up, b_e_up,
           w_e_down, b_e_down, final_norm_g):
    assert mix_norm_g.shape[0] == 1, "one layer"
    n_prompt, prompt_len, _ = x_prompt.shape
    n_sample, sample_len, _ = x_sample.shape
    assert n_sample == 1
    tp = n_prompt * prompt_len
    yp, ys = _layer(x_prompt.reshape(tp, D), x_sample.reshape(sample_len, D), n_prompt, prompt_len, sample_len,
                    mix_norm_g[0], w_in[0], ret_norm_g[0], w_ret_o[0], conv_w[0],
                    conv_b[0], lru_w_a[0], lru_b_a[0], lru_w_x[0], lru_b_x[0], lru_lambda[0], w_lru_o[0], w_out[0],
                    moe_norm_g[0], w_router[0], b_router[0], w_e_gate[0], b_e_gate[0], w_e_up[0], b_e_up[0],
                    w_e_down[0], b_e_down[0], final_norm_g)
    return yp.reshape(x_prompt.shape), ys.reshape(x_sample.shape)
```

```python
import functools
import math

import jax
import jax.numpy as jnp
from jax import lax
from jax.experimental import pallas as pl
from jax.experimental.pallas import tpu as pltpu

F32 = jnp.float32
BF16 = jnp.bfloat16
U32 = jnp.uint32
I32 = jnp.int32

D = 2048
HEADS = 8
DK = 128
DV = 256
QK_W = HEADS * DK
V_W = HEADS * DV
LRU_BLOCK = 128
N_EXP = 32
TOP_K = 4
D_FF = 2048
SWIGLU_LIMIT = 7.0
SWIGLU_ALPHA = 1.702
ROPE_BASE = 10000.0
LRU_C = 8.0
EPS = 1e-6
IN_W = 2 * QK_W + 2 * V_W + 2 * D + 2 * D

HALF = D // 2
LANE = 128
HALO = 16
HI_MASK = 0xFFFF0000

SEQ_CHUNK = 256
LRU_CHUNK = 512
LRU_CB = 1024
LRU_SEG = 4
LRU_SUPER = 8 * LRU_SEG
MOE_ROWS = 512
MOE_TF = 1024
MOE_TN = 1024
TOK_TILE = 512
SEG_ALIGN = 8
PERM_CHUNK = 256
SORT_ROWS = TOK_TILE * 4 + 32 * SEG_ALIGN
SEG_MAXBIT = (TOK_TILE // SEG_ALIGN).bit_length() - 1


def _cp(sem, vmem_mb):
    return pltpu.CompilerParams(dimension_semantics=sem, vmem_limit_bytes=vmem_mb << 20)


def _bits(x):
    return lax.bitcast_convert_type(x, U32)


def _pack_pair(lo, hi, rounded=False):
    if not rounded:
        lo = lo.astype(BF16).astype(F32)
        hi = hi.astype(BF16).astype(F32)
    return (_bits(lo) >> 16) | (_bits(hi) & jnp.uint32(HI_MASK))


def _unpack_pair(u):
    lo = lax.bitcast_convert_type(u << 16, F32)
    hi = lax.bitcast_convert_type(u & jnp.uint32(HI_MASK), F32)
    return lo, hi


class _Seqs:
    def __init__(self, n_prompt, prompt_len, sample_len, chunk):
        assert prompt_len % chunk == 0 and sample_len % chunk == 0
        self.cps = prompt_len // chunk
        self.npc = n_prompt * self.cps
        self.nch = self.npc + sample_len // chunk

    def is_first(self, ci):
        return ((ci < self.npc) & (ci % self.cps == 0)) | (ci == self.npc)

    def is_last(self, ci):
        return ((ci < self.npc) & (ci % self.cps == self.cps - 1)) | (ci == self.nch - 1)

    def pos_chunk(self, ci):
        return jnp.where(ci < self.npc, ci % self.cps, ci - self.npc)


def _two_group_specs(tm, npt, width=D):
    return [pl.BlockSpec((tm, width), lambda i, j: (jnp.minimum(i, npt - 1), 0)),
            pl.BlockSpec((tm, width), lambda i, j: (jnp.maximum(i - npt, 0), 0))]


def _inproj_kernel(xp_ref, xs_ref, g_ref, w_ref, o_ref, xn_ref, *, npt):
    i = pl.program_id(0)

    def norm(x_ref):
        x = x_ref[...]
        y = x * lax.rsqrt(jnp.mean(x * x, axis=-1, keepdims=True) + EPS)
        xn_ref[...] = (y * g_ref[...]).astype(BF16)

    @pl.when(pl.program_id(1) == 0)
    def _():
        pl.when(i < npt)(lambda: norm(xp_ref))
        pl.when(i >= npt)(lambda: norm(xs_ref))

    o_ref[...] = jnp.dot(xn_ref[...], w_ref[...], preferred_element_type=F32).astype(o_ref.dtype)


def _inproj(xp, xs, g, w, tm=1024, tn=1024):
    t, n = xp.shape[0] + xs.shape[0], w.shape[1]
    npt = xp.shape[0] // tm
    return pl.pallas_call(
        functools.partial(_inproj_kernel, npt=npt),
        grid=(t // tm, n // tn),
        in_specs=_two_group_specs(tm, npt) + [pl.BlockSpec((1, D), lambda i, j: (0, 0)),
                                              pl.BlockSpec((D, tn), lambda i, j: (0, j))],
        out_specs=pl.BlockSpec((tm, tn), lambda i, j: (i, j)),
        out_shape=jax.ShapeDtypeStruct((t, n), BF16),
        scratch_shapes=[pltpu.VMEM((tm, D), BF16)],
        compiler_params=_cp(("arbitrary", "arbitrary"), 56),
        name="inproj",
    )(xp, xs, g, w)


def _rot(x, cosf, sinf):
    return x * cosf + pltpu.roll(x, DK // 2, 1) * sinf


def _log_gamma(h):
    return math.log1p(-(2.0 ** (-5 - h)))


def _decay_tables(dec_ref, c, q_exponent, k_exponent):
    p = lax.broadcasted_iota(I32, (c, DK), 0).astype(F32)
    for h in range(HEADS):
        lg = _log_gamma(h)
        dec_ref[h, 0] = jnp.exp(lg * q_exponent(p))
        dec_ref[h, 1] = jnp.exp(lg * k_exponent(p))


def _ret_bwd_kernel(q_ref, k_ref, v_ref, cos_ref, sin_ref, o_ref, s_ref, dec_ref, *, seqs, c):
    ci = seqs.nch - 1 - pl.program_id(0)

    @pl.when(pl.program_id(0) == 0)
    def _():
        _decay_tables(dec_ref, c, lambda p: c - p, lambda p: p)

    @pl.when(seqs.is_last(ci))
    def _():
        s_ref[...] = jnp.zeros_like(s_ref)

    cosf = cos_ref[...]
    sinf = sin_ref[...]
    for h in range(HEADS):
        lg = _log_gamma(h)
        q = _rot(q_ref[:, h * DK:(h + 1) * DK].astype(F32), cosf, sinf)
        k = _rot(k_ref[:, h * DK:(h + 1) * DK].astype(F32), cosf, sinf) * (DK ** -0.5)
        v = v_ref[:, h * DV:(h + 1) * DV]
        qd = (q * dec_ref[h, 0]).astype(BF16)
        kd = (k * dec_ref[h, 1]).astype(BF16)
        s = s_ref[h]
        o_ref[:, h * DV:(h + 1) * DV] = jnp.dot(qd, s.astype(BF16), preferred_element_type=F32)
        s_ref[h] = math.exp(lg * c) * s + lax.dot_general(
            kd, v, (((0,), (0,)), ((), ())), preferred_element_type=F32)


def _ret_fwd_kernel(q_ref, k_ref, v_ref, cos_ref, sin_ref, bwd_ref, gate_ref, gn_ref, o_ref, s_ref, dm_ref,
                    dec_ref, *, seqs, c):
    ci = pl.program_id(0)

    @pl.when(ci == 0)
    def _():
        _decay_tables(dec_ref, c, lambda p: p + 1.0, lambda p: c - 1.0 - p)
        r = lax.broadcasted_iota(I32, (c, c), 0)
        col = lax.broadcasted_iota(I32, (c, c), 1)
        dist = jnp.abs(r - col).astype(F32)
        for h in range(HEADS):
            dm_ref[h] = jnp.exp(_log_gamma(h) * dist)

    @pl.when(seqs.is_first(ci))
    def _():
        s_ref[...] = jnp.zeros_like(s_ref)

    cosf = cos_ref[...]
    sinf = sin_ref[...]
    for h in range(HEADS):
        lg = _log_gamma(h)
        q = _rot(q_ref[:, h * DK:(h + 1) * DK].astype(F32), cosf, sinf)
        k = _rot(k_ref[:, h * DK:(h + 1) * DK].astype(F32), cosf, sinf) * (DK ** -0.5)
        v = v_ref[:, h * DV:(h + 1) * DV]
        scores = lax.dot_general(q.astype(BF16), k.astype(BF16), (((1,), (1,)), ((), ())),
                                 preferred_element_type=F32) * dm_ref[h]
        tot = jnp.dot(scores.astype(BF16), v, preferred_element_type=F32)
        qd = (q * dec_ref[h, 0]).astype(BF16)
        kd = (k * dec_ref[h, 1]).astype(BF16)
        s = s_ref[h]
        tot = tot + jnp.dot(qd, s.astype(BF16), preferred_element_type=F32)
        s_ref[h] = math.exp(lg * c) * s + lax.dot_general(
            kd, v, (((0,), (0,)), ((), ())), preferred_element_type=F32)
        tot = tot + bwd_ref[:, h * DV:(h + 1) * DV]
        mu = jnp.mean(tot, axis=-1, keepdims=True)
        cen = tot - mu
        var = jnp.mean(cen * cen, axis=-1, keepdims=True)
        yn = cen * lax.rsqrt(var + EPS) * gn_ref[:, h * DV:(h + 1) * DV]
        g = gate_ref[:, h * DV:(h + 1) * DV].astype(F32)
        o_ref[:, h * DV:(h + 1) * DV] = (yn * (g * jax.nn.sigmoid(g))).astype(o_ref.dtype)


def _retention(z, cosf, sinf, gn, seqs, c):
    t = z.shape[0]
    nch = seqs.nch
    rev = lambda i: nch - 1 - i
    qkv_specs = lambda f: [pl.BlockSpec((c, QK_W), lambda i: (f(i), 0)),
                           pl.BlockSpec((c, QK_W), lambda i: (f(i), 1)),
                           pl.BlockSpec((c, V_W), lambda i: (f(i), 1)),
                           pl.BlockSpec((c, DK), lambda i: (seqs.pos_chunk(f(i)), 0)),
                           pl.BlockSpec((c, DK), lambda i: (seqs.pos_chunk(f(i)), 0))]
    bwd = pl.pallas_call(
        functools.partial(_ret_bwd_kernel, seqs=seqs, c=c),
        grid=(nch,),
        in_specs=qkv_specs(rev),
        out_specs=pl.BlockSpec((c, V_W), lambda i: (rev(i), 0)),
        out_shape=jax.ShapeDtypeStruct((t, V_W), F32),
        scratch_shapes=[pltpu.VMEM((HEADS, DK, DV), F32), pltpu.VMEM((HEADS, 2, c, DK), F32)],
        compiler_params=_cp(("arbitrary",), 32),
        name="ret_bwd",
    )(z, z, z, cosf, sinf)
    fwd_id = lambda i: i
    return pl.pallas_call(
        functools.partial(_ret_fwd_kernel, seqs=seqs, c=c),
        grid=(nch,),
        in_specs=qkv_specs(fwd_id) + [pl.BlockSpec((c, V_W), lambda i: (i, 0)),
                                      pl.BlockSpec((c, V_W), lambda i: (i, 2)),
                                      pl.BlockSpec((1, V_W), lambda i: (0, 0))],
        out_specs=pl.BlockSpec((c, V_W), lambda i: (i, 0)),
        out_shape=jax.ShapeDtypeStruct((t, V_W), BF16),
        scratch_shapes=[pltpu.VMEM((HEADS, DK, DV), F32), pltpu.VMEM((HEADS, c, c), F32),
                        pltpu.VMEM((HEADS, 2, c, DK), F32)],
        compiler_params=_cp(("arbitrary",), 40),
        name="ret_fwd",
    )(z, z, z, cosf, sinf, bwd, z, gn)


def _lru_kernel(*refs, reverse, seqs, c, cb):
    if reverse:
        (xc_ref, wa_ref, ba_ref, wx_ref, bx_ref, c8_ref, hf_ref, g_ref, o_ref, a_s, u_s, h_s, carry_s) = refs
    else:
        (x_ref, xp_ref, xn_ref, cw_ref, cbias_ref, wa_ref, ba_ref, wx_ref, bx_ref, c8_ref,
         o_ref, xc_out_ref, a_s, u_s, h_s, carry_s) = refs
    t = pl.program_id(1)
    ci = seqs.nch - 1 - t if reverse else t
    first = seqs.is_first(ci)
    last = seqs.is_last(ci)

    if reverse:
        xc = xc_ref[...]
    else:
        x = x_ref[...].astype(F32)
        prev = jnp.where(first, 0.0, xp_ref[...].astype(F32))
        nxt = jnp.where(last, 0.0, xn_ref[...].astype(F32))
        row8 = lax.broadcasted_iota(I32, (8, 1), 0)

        def patch(arr, at, rows8):
            parts = ([arr[:at]] if at > 0 else []) + [rows8] + ([arr[at + 8:]] if at + 8 < c else [])
            return jnp.concatenate(parts, axis=0)

        xm1 = pltpu.roll(x, 1, 0)
        xm1 = patch(xm1, 0, jnp.where(row8 == 0, prev[HALO - 1:HALO], xm1[0:8]))
        xm2 = pltpu.roll(x, 2, 0)
        xm2 = patch(xm2, 0, jnp.where(row8 == 0, prev[HALO - 2:HALO - 1],
                                      jnp.where(row8 == 1, prev[HALO - 1:HALO], xm2[0:8])))
        xp1 = pltpu.roll(x, c - 1, 0)
        xp1 = patch(xp1, c - 8, jnp.where(row8 == 7, nxt[0:1], xp1[c - 8:c]))
        cw = cw_ref[...]
        xc = cw[0:1] * xm2 + cw[1:2] * xm1 + cw[2:3] * x + cw[3:4] * xp1 + cbias_ref[...]
        xc_out_ref[...] = xc

    nslab = cb // LRU_BLOCK
    for gi in range(nslab):
        sl = slice(gi * LRU_BLOCK, (gi + 1) * LRU_BLOCK)
        xs = xc[:, sl]
        xb = xs.astype(BF16)
        r = jax.nn.sigmoid(jnp.dot(xb, wa_ref[gi], preferred_element_type=F32) + ba_ref[:, sl])
        ig = jax.nn.sigmoid(jnp.dot(xb, wx_ref[gi], preferred_element_type=F32) + bx_ref[:, sl])
        a = jnp.exp2(c8_ref[:, sl] * r)
        a_s[gi] = a
        v = 1.0 - a * a
        u_s[gi] = jnp.where(v > 0.0, v * lax.rsqrt(v), 0.0) * (ig * xs)

    @pl.when(last if reverse else first)
    def _():
        carry_s[...] = jnp.zeros_like(carry_s)

    sub = lax.broadcasted_iota(I32, (8, LRU_BLOCK), 0)
    nsuper = c // LRU_SUPER
    steps = list(range(LRU_SEG))
    if reverse:
        steps = steps[::-1]

    def super_group(q, carries):
        qi = nsuper - 1 - q if reverse else q
        base = qi * LRU_SUPER
        out = []
        for gi in range(nslab):
            hs, ps = {}, {}
            h = p = None
            for j in steps:
                rows = pl.ds(base + j, 8, stride=LRU_SEG)
                a = a_s[gi, rows, :]
                u = u_s[gi, rows, :]
                h = u if h is None else a * h + u
                p = a if p is None else a * p
                hs[j], ps[j] = h, p
            eh, ep = h, p
            for s in (1, 2, 4):
                shift = 8 - s if reverse else s
                m = (sub < 8 - s) if reverse else (sub >= s)
                eh_sh = pltpu.roll(eh, shift, 0)
                ep_sh = pltpu.roll(ep, shift, 0)
                eh = eh + ep * jnp.where(m, eh_sh, 0.0)
                ep = ep * jnp.where(m, ep_sh, 1.0)
            end = eh + ep * carries[gi]
            if reverse:
                enter = jnp.where(sub == 7, carries[gi], pltpu.roll(end, 7, 0))
                out.append(jnp.broadcast_to(end[0:1], (8, LRU_BLOCK)))
            else:
                enter = jnp.where(sub == 0, carries[gi], pltpu.roll(end, 1, 0))
                out.append(jnp.broadcast_to(end[7:8], (8, LRU_BLOCK)))
            for j in steps:
                h_s[gi, pl.ds(base + j, 8, stride=LRU_SEG), :] = hs[j] + ps[j] * enter
        return tuple(out)

    carries = lax.fori_loop(0, nsuper, super_group, tuple(carry_s[gi] for gi in range(nslab)))
    for gi in range(nslab):
        carry_s[gi] = carries[gi]

    for gi in range(nslab):
        sl = slice(gi * LRU_BLOCK, (gi + 1) * LRU_BLOCK)
        if reverse:
            g = g_ref[:, sl].astype(F32)
            o_ref[:, sl] = ((hf_ref[:, sl] + h_s[gi]) * jax.nn.gelu(g, approximate=True)).astype(o_ref.dtype)
        else:
            o_ref[:, sl] = h_s[gi]


def _lru_call(z, conv_w, conv_b, wa, ba, wx, bx, c8, fwd, direction, seqs, c, cb):
    t = z.shape[0]
    nch = seqs.nch
    reverse = direction == 1
    tmap = (lambda ti: nch - 1 - ti) if reverse else (lambda ti: ti)
    xcol = (2 * QK_W + 2 * V_W) // cb
    gcol = (2 * QK_W + 2 * V_W + D) // cb
    hb = c // HALO
    nhb = t // HALO
    nb = cb // LRU_BLOCK
    tile = pl.BlockSpec((c, cb), lambda ch, ti: (tmap(ti), ch))
    gate_specs = [
        pl.BlockSpec((None, nb, LRU_BLOCK, LRU_BLOCK), lambda ch, ti: (direction, ch, 0, 0)),
        pl.BlockSpec((None, 1, cb), lambda ch, ti: (direction, 0, ch)),
        pl.BlockSpec((None, nb, LRU_BLOCK, LRU_BLOCK), lambda ch, ti: (direction, ch, 0, 0)),
        pl.BlockSpec((None, 1, cb), lambda ch, ti: (direction, 0, ch)),
        pl.BlockSpec((None, 1, cb), lambda ch, ti: (direction, 0, ch)),
    ]
    if reverse:
        h_fwd, xc = fwd
        in_specs = [tile] + gate_specs + [tile, pl.BlockSpec((c, cb), lambda ch, ti: (tmap(ti), gcol + ch))]
        args = [xc, wa, ba, wx, bx, c8, h_fwd, z]
        out_specs = tile
        out_shape = jax.ShapeDtypeStruct((t, D), BF16)
    else:
        in_specs = [
            pl.BlockSpec((c, cb), lambda ch, ti: (tmap(ti), xcol + ch)),
            pl.BlockSpec((HALO, cb), lambda ch, ti: (jnp.maximum(tmap(ti) * hb - 1, 0), xcol + ch)),
            pl.BlockSpec((HALO, cb), lambda ch, ti: (jnp.minimum((tmap(ti) + 1) * hb, nhb - 1), xcol + ch)),
            pl.BlockSpec((4, cb), lambda ch, ti: (0, ch)),
            pl.BlockSpec((1, cb), lambda ch, ti: (0, ch)),
        ] + gate_specs
        args = [z, z, z, conv_w, conv_b, wa, ba, wx, bx, c8]
        out_specs = [tile, tile]
        out_shape = [jax.ShapeDtypeStruct((t, D), F32), jax.ShapeDtypeStruct((t, D), F32)]
    return pl.pallas_call(
        functools.partial(_lru_kernel, reverse=reverse, seqs=seqs, c=c, cb=cb),
        grid=(D // cb, nch),
        in_specs=in_specs,
        out_specs=out_specs,
        out_shape=out_shape,
        scratch_shapes=[pltpu.VMEM((nb, c, LRU_BLOCK), F32), pltpu.VMEM((nb, c, LRU_BLOCK), F32),
                        pltpu.VMEM((nb, c, LRU_BLOCK), F32), pltpu.VMEM((nb, 8, LRU_BLOCK), F32)],
        compiler_params=_cp(("arbitrary", "arbitrary"),32),
        name="lru_rev" if reverse else "lru_fwd",
    )(*args)


def _proj_merge_kernel(yr_ref, yl_ref, wr_ref, wl_ref, mr_ref, ml_ref, o_ref):
    r = jnp.dot(yr_ref[...], wr_ref[...], preferred_element_type=F32)
    l = jnp.dot(yl_ref[...], wl_ref[...], preferred_element_type=F32)
    o_ref[...] = (jax.nn.sigmoid(mr_ref[...].astype(F32)) * r
                  + jax.nn.sigmoid(ml_ref[...].astype(F32)) * l).astype(o_ref.dtype)


def _proj_merge(y_ret, y_lru, w_ret_o, w_lru_o, z, tm=1024, tn=1024):
    t = y_ret.shape[0]
    mcol = (2 * QK_W + 2 * V_W + 2 * D) // tn
    return pl.pallas_call(
        _proj_merge_kernel,
        grid=(D // tn, t // tm),
        in_specs=[pl.BlockSpec((tm, V_W), lambda j, i: (i, 0)),
                  pl.BlockSpec((tm, D), lambda j, i: (i, 0)),
                  pl.BlockSpec((V_W, tn), lambda j, i: (0, j)),
                  pl.BlockSpec((D, tn), lambda j, i: (0, j)),
                  pl.BlockSpec((tm, tn), lambda j, i: (i, mcol + j)),
                  pl.BlockSpec((tm, tn), lambda j, i: (i, mcol + D // tn + j))],
        out_specs=pl.BlockSpec((tm, tn), lambda j, i: (i, j)),
        out_shape=jax.ShapeDtypeStruct((t, D), BF16),
        compiler_params=_cp(("arbitrary", "arbitrary"), 56),
        name="proj_merge",
    )(y_ret, y_lru, w_ret_o, w_lru_o, z, z)


def _proj_out_kernel(m_ref, w_ref, xp_ref, xs_ref, o_ref, *, npt):
    i = pl.program_id(1)
    y = jnp.dot(m_ref[...], w_ref[...], preferred_element_type=F32)

    @pl.when(i < npt)
    def _():
        o_ref[...] = xp_ref[...] + y

    @pl.when(i >= npt)
    def _():
        o_ref[...] = xs_ref[...] + y


def _proj_out(merged, w_out, xp, xs, tm=1024, tn=1024):
    t = merged.shape[0]
    npt = xp.shape[0] // tm
    return pl.pallas_call(
        functools.partial(_proj_out_kernel, npt=npt),
        grid=(D // tn, t // tm),
        in_specs=[pl.BlockSpec((tm, D), lambda j, i: (i, 0)),
                  pl.BlockSpec((D, tn), lambda j, i: (0, j)),
                  pl.BlockSpec((tm, tn), lambda j, i: (jnp.minimum(i, npt - 1), j)),
                  pl.BlockSpec((tm, tn), lambda j, i: (jnp.maximum(i - npt, 0), j))],
        out_specs=pl.BlockSpec((tm, tn), lambda j, i: (i, j)),
        out_shape=jax.ShapeDtypeStruct((t, D), F32),
        compiler_params=_cp(("arbitrary", "arbitrary"), 48),
        name="proj_out",
    )(merged, w_out, xp, xs)


def _router_kernel(x_ref, g_ref, whi_ref, wlo_ref, b_ref, xn_ref, tpos_ref, tpos_t_ref, gate_ref, meta_ref, tri_ref,
                   run_ref, *, tm):
    i = pl.program_id(0)

    @pl.when(i == 0)
    def _():
        r = lax.broadcasted_iota(I32, (tm, tm), 0)
        col = lax.broadcasted_iota(I32, (tm, tm), 1)
        tri_ref[...] = (col < r).astype(BF16)
        run_ref[...] = jnp.zeros_like(run_ref)

    x = x_ref[...]
    xn = x * lax.rsqrt(jnp.mean(x * x, axis=-1, keepdims=True) + EPS) * g_ref[...]
    xh = xn.astype(BF16)
    xn_ref[...] = xh

    xl = (xn - xh.astype(F32)).astype(BF16)
    logits = (jnp.dot(xh, whi_ref[...], preferred_element_type=F32)
              + (jnp.dot(xl, whi_ref[...], preferred_element_type=F32)
                 + jnp.dot(xh, wlo_ref[...], preferred_element_type=F32))) + b_ref[...]
    lane = lax.broadcasted_iota(I32, (tm, LANE), 1)
    lane_f = lane.astype(F32)
    vals, idxs = [], []
    cur = logits
    for _ in range(TOP_K):
        m = jnp.max(cur, axis=-1, keepdims=True)
        idx = jnp.min(jnp.where(cur == m, lane_f, float(LANE)), axis=-1, keepdims=True).astype(I32)
        vals.append(m)
        idxs.append(idx)
        cur = jnp.where(lane == idx, -jnp.inf, cur)
    exps = [jnp.exp(v - vals[0]) for v in vals]
    denom = exps[0] + exps[1] + exps[2] + exps[3]

    onehot = jnp.zeros((tm, LANE), F32)
    for idx in idxs:
        onehot = onehot + (lane == idx).astype(F32)
    before = jnp.dot(tri_ref[...], onehot.astype(BF16), preferred_element_type=F32)

    cnt = jnp.broadcast_to(jnp.sum(onehot, axis=0, keepdims=True), (8, LANE))
    cnt8 = jnp.floor((cnt + (SEG_ALIGN - 1)) * (1.0 / SEG_ALIGN)) * SEG_ALIGN
    lane8 = lax.broadcasted_iota(I32, (8, LANE), 1)
    incl = cnt8
    s = 1
    while s < LANE:
        incl = incl + jnp.where(lane8 >= s, pltpu.roll(incl, s, 1), 0.0)
        s *= 2
    toff = incl - cnt8

    pos = before + toff[0:1]
    tpos_out = jnp.zeros((tm, LANE), F32)
    gate_out = jnp.zeros((tm, LANE), F32)
    for k in range(TOP_K):
        tpos_k = jnp.sum(jnp.where(lane == idxs[k], pos, 0.0), axis=-1, keepdims=True)
        tpos_out = jnp.where(lane == k, tpos_k, tpos_out)
        gate_out = jnp.where(lane == k, exps[k] / denom, gate_out)
    tpos_ref[...] = tpos_out.astype(I32)
    tpos_t_ref[...] = tpos_out.T[0:8].astype(I32)
    gate_ref[...] = gate_out

    sub8 = lax.broadcasted_iota(I32, (8, LANE), 0)
    run = jnp.broadcast_to(run_ref[...], (8, LANE))
    meta_ref[...] = jnp.where(sub8 == 0, cnt8, jnp.where(sub8 == 1, toff, jnp.where(sub8 == 2, run, 0.0)))
    run_ref[...] = run_ref[...] + cnt8[0:1]


def _router(x1, g, w_hi, w_lo, b_pad, tm):
    t = x1.shape[0]
    nt = t // tm
    tok = lambda i: (i, 0)
    fixed = lambda i: (0, 0)
    per_tile = lambda i: (i, 0, 0)
    return pl.pallas_call(
        functools.partial(_router_kernel, tm=tm),
        grid=(nt,),
        in_specs=[pl.BlockSpec((tm, D), tok), pl.BlockSpec((1, D), fixed),
                  pl.BlockSpec((D, LANE), fixed), pl.BlockSpec((D, LANE), fixed), pl.BlockSpec((1, LANE), fixed)],
        out_specs=[pl.BlockSpec((tm, D), tok), pl.BlockSpec((tm, LANE), tok),
                   pl.BlockSpec((None, 8, tm), per_tile), pl.BlockSpec((tm, LANE), tok),
                   pl.BlockSpec((None, 8, LANE), per_tile)],
        out_shape=[jax.ShapeDtypeStruct((t, D), BF16), jax.ShapeDtypeStruct((t, LANE), I32),
                   jax.ShapeDtypeStruct((nt, 8, tm), I32), jax.ShapeDtypeStruct((t, LANE), F32),
                   jax.ShapeDtypeStruct((nt, 8, LANE), F32)],
        scratch_shapes=[pltpu.VMEM((tm, tm), BF16), pltpu.VMEM((1, LANE), F32)],
        compiler_params=_cp(("arbitrary",), 32),
        name="router",
    )(x1, g, w_hi, w_lo, b_pad)


N_CLASS = SEG_MAXBIT + 1
USED_LANE = N_CLASS + 1


def _copy_lists(toff, c8, dst):
    nt = c8.shape[0]
    n = (c8 // SEG_ALIGN)[:, None, :]
    b = jnp.arange(N_CLASS, dtype=I32)[None, :, None]
    bits = (n >> b) & 1
    off = ((n >> (b + 1)) << (b + 1)) * SEG_ALIGN
    order = jnp.argsort(1 - bits, axis=-1, stable=True)
    src = jnp.take_along_axis(toff[:, None, :] + off, order, axis=-1)
    dstl = jnp.take_along_axis(dst[:, None, :] + off, order, axis=-1)
    src_t = jnp.zeros((nt, 8, LANE), I32).at[:, :N_CLASS, :N_EXP].set(src)
    src_t = src_t.at[:, N_CLASS, :N_CLASS].set(jnp.sum(bits, axis=-1))
    src_t = src_t.at[:, N_CLASS, USED_LANE].set(toff[:, -1] + c8[:, -1])
    dst_t = jnp.zeros((nt, 8, LANE), I32).at[:, :N_CLASS, :N_EXP].set(dstl)
    return src_t, dst_t


def _class_copies(src_ref, dst_ref, make_copy, act):
    for b in range(N_CLASS):
        rows = SEG_ALIGN << b

        def body(s, carry, b=b, rows=rows):
            act(make_copy(pl.multiple_of(src_ref[b, s], SEG_ALIGN), pl.multiple_of(dst_ref[b, s], SEG_ALIGN), rows))
            return carry

        lax.fori_loop(0, src_ref[N_CLASS, b], body, 0)


def _dispatch_kernel(zblk_ref, src_ref, dst_ref, src_prev_ref, dst_prev_ref, tpos_t_ref, x_ref, xg_ref, sorted_ref,
                     zero_ref, sem, *, nt):
    i = pl.program_id(0)
    slot = i % 2

    def zero_copy(e):
        start = pl.multiple_of(jnp.maximum(zblk_ref[e], 0), MOE_ROWS)
        return pltpu.make_async_copy(zero_ref, xg_ref.at[pl.ds(start, MOE_ROWS), :], sem.at[0])

    def for_nonempty(act):
        def body(e, carry):
            pl.when(zblk_ref[e] >= 0)(lambda: act(zero_copy(e)))
            return carry
        lax.fori_loop(0, N_EXP, body, 0)

    def for_unused(act):
        def body(b, carry):
            start = pl.multiple_of(b * MOE_ROWS, MOE_ROWS)
            act(pltpu.make_async_copy(zero_ref, xg_ref.at[pl.ds(start, MOE_ROWS), :], sem.at[0]))
            return carry
        lax.fori_loop(zblk_ref[N_EXP], xg_ref.shape[0] // MOE_ROWS, body, 0)

    @pl.when(i == 0)
    def _():
        zero_ref[...] = jnp.zeros_like(zero_ref)
        for_nonempty(lambda cp: cp.start())
        for_unused(lambda cp: cp.start())
        for_nonempty(lambda cp: cp.wait())
        for_unused(lambda cp: cp.wait())

    x = x_ref[...]
    for c in range(SORT_ROWS // PERM_CHUNK):
        p = c * PERM_CHUNK + lax.broadcasted_iota(I32, (PERM_CHUNK, 1), 0)
        perm = jnp.zeros((PERM_CHUNK, x.shape[0]), F32)
        for k in range(TOP_K):
            perm = perm + jnp.where(p == tpos_t_ref[k:k + 1, :], 1.0, 0.0)
        rows = jnp.dot(perm.astype(BF16), x, preferred_element_type=F32)
        sorted_ref[slot, c * PERM_CHUNK:(c + 1) * PERM_CHUNK, :] = _pack_pair(rows[:, :HALF], rows[:, HALF:],
                                                                              rounded=True)

    def copies(lists, buf, act):
        def make_copy(t_off, d, rows):
            return pltpu.make_async_copy(sorted_ref.at[buf, pl.ds(t_off, rows), :], xg_ref.at[pl.ds(d, rows), :],
                                         sem.at[buf])
        _class_copies(*lists, make_copy, act)

    this_tile = (src_ref, dst_ref)
    pl.when(i > 0)(lambda: copies((src_prev_ref, dst_prev_ref), 1 - slot, lambda cp: cp.wait()))
    copies(this_tile, slot, lambda cp: cp.start())
    pl.when(i == nt - 1)(lambda: copies(this_tile, slot, lambda cp: cp.wait()))


def _dispatch(zblk, src_lists, dst_lists, tpos_t, xn, rows):
    nt, _, tm = tpos_t.shape
    this_tile = lambda i, *_: (i, 0, 0)
    prev_tile = lambda i, *_: (jnp.maximum(i - 1, 0), 0, 0)
    lists = lambda index_map: pl.BlockSpec((None, 8, LANE), index_map, memory_space=pltpu.SMEM)
    grid_spec = pltpu.PrefetchScalarGridSpec(
        num_scalar_prefetch=1,
        grid=(nt,),
        in_specs=[lists(this_tile), lists(this_tile), lists(prev_tile), lists(prev_tile),
                  pl.BlockSpec((None, 8, tm), this_tile),
                  pl.BlockSpec((tm, D), lambda i, *_: (i, 0))],
        out_specs=pl.BlockSpec(memory_space=pl.ANY),
        scratch_shapes=[pltpu.VMEM((2, SORT_ROWS, HALF), U32), pltpu.VMEM((MOE_ROWS, HALF), U32),
                        pltpu.SemaphoreType.DMA((2,))],
    )
    return pl.pallas_call(
        functools.partial(_dispatch_kernel, nt=nt),
        grid_spec=grid_spec,
        out_shape=jax.ShapeDtypeStruct((rows, HALF), U32),
        compiler_params=_cp(("arbitrary",), 40),
        name="dispatch",
    )(zblk, src_lists, dst_lists, src_lists, dst_lists, tpos_t, xn)


def _expert_changed(be_ref, i):
    return (i == 0) | (be_ref[i] != be_ref[jnp.maximum(i - 1, 0)])


def _expert_up_kernel(be_ref, nu_ref, xg_ref, wg_ref, bg_ref, wu_ref, bu_ref, h_ref, wgb_ref, wub_ref):
    i = pl.program_id(1)

    @pl.when(i < nu_ref[0])
    def _():
        @pl.when(_expert_changed(be_ref, i))
        def _():
            wgb_ref[...] = wg_ref[...].astype(BF16)
            wub_ref[...] = wu_ref[...].astype(BF16)

        lo, hi = _unpack_pair(xg_ref[...])
        xb = jnp.concatenate([lo.astype(BF16), hi.astype(BF16)], axis=-1)
        gate = jnp.minimum(jnp.dot(xb, wgb_ref[...], preferred_element_type=F32) + bg_ref[...], SWIGLU_LIMIT)
        up = jnp.clip(jnp.dot(xb, wub_ref[...], preferred_element_type=F32) + bu_ref[...],
                      -SWIGLU_LIMIT, SWIGLU_LIMIT)
        h_ref[...] = ((up + 1.0) * gate * jax.nn.sigmoid(SWIGLU_ALPHA * gate)).astype(h_ref.dtype)

    @pl.when(i >= nu_ref[0])
    def _():
        h_ref[...] = jnp.zeros_like(h_ref)


def _expert_down_kernel(be_ref, nu_ref, h_ref, wlo_ref, whi_ref, blo_ref, bhi_ref, o_ref, wlob_ref, whib_ref):
    i = pl.program_id(1)

    @pl.when(i < nu_ref[0])
    def _():
        @pl.when(_expert_changed(be_ref, i))
        def _():
            wlob_ref[...] = wlo_ref[...].astype(BF16)
            whib_ref[...] = whi_ref[...].astype(BF16)

        h = h_ref[...]
        lo = jnp.dot(h, wlob_ref[...], preferred_element_type=F32) + blo_ref[...]
        hi = jnp.dot(h, whib_ref[...], preferred_element_type=F32) + bhi_ref[...]
        o_ref[...] = _pack_pair(lo, hi)

    @pl.when(i >= nu_ref[0])
    def _():
        o_ref[...] = jnp.zeros_like(o_ref)


def _experts(block_expert, n_used, xg, wg, bg, wu, bu, wd, bd, rows_blk, tf, tn):
    rows = xg.shape[0]
    nb = rows // rows_blk
    nj = D_FF // tf
    nn = HALF // tn

    def blk(i, nu):
        return jnp.minimum(i, nu[0] - 1)

    up_spec = pltpu.PrefetchScalarGridSpec(
        num_scalar_prefetch=2,
        grid=(nj, nb),
        in_specs=[pl.BlockSpec((rows_blk, HALF), lambda j, i, be, nu: (blk(i, nu), 0)),
                  pl.BlockSpec((None, D, tf), lambda j, i, be, nu: (be[i], 0, j)),
                  pl.BlockSpec((None, 1, tf), lambda j, i, be, nu: (be[i], 0, j)),
                  pl.BlockSpec((None, D, tf), lambda j, i, be, nu: (be[i], 0, j)),
                  pl.BlockSpec((None, 1, tf), lambda j, i, be, nu: (be[i], 0, j))],
        out_specs=pl.BlockSpec((rows_blk, tf), lambda j, i, be, nu: (i, j)),
        scratch_shapes=[pltpu.VMEM((D, tf), BF16), pltpu.VMEM((D, tf), BF16)],
    )
    hid = pl.pallas_call(
        _expert_up_kernel,
        grid_spec=up_spec,
        out_shape=jax.ShapeDtypeStruct((rows, D_FF), BF16),
        compiler_params=_cp(("arbitrary", "arbitrary"), 60),
        name="experts_up",
    )(block_expert, n_used, xg, wg, bg, wu, bu)

    down_spec = pltpu.PrefetchScalarGridSpec(
        num_scalar_prefetch=2,
        grid=(nn, nb),
        in_specs=[pl.BlockSpec((rows_blk, D_FF), lambda n, i, be, nu: (blk(i, nu), 0)),
                  pl.BlockSpec((None, D_FF, tn), lambda n, i, be, nu: (be[i], 0, n)),
                  pl.BlockSpec((None, D_FF, tn), lambda n, i, be, nu: (be[i], 0, nn + n)),
                  pl.BlockSpec((None, 1, tn), lambda n, i, be, nu: (be[i], 0, n)),
                  pl.BlockSpec((None, 1, tn), lambda n, i, be, nu: (be[i], 0, nn + n))],
        out_specs=pl.BlockSpec((rows_blk, tn), lambda n, i, be, nu: (i, n)),
        scratch_shapes=[pltpu.VMEM((D_FF, tn), BF16), pltpu.VMEM((D_FF, tn), BF16)],
    )
    return pl.pallas_call(
        _expert_down_kernel,
        grid_spec=down_spec,
        out_shape=jax.ShapeDtypeStruct((rows, HALF), U32),
        compiler_params=_cp(("arbitrary", "arbitrary"), 60),
        name="experts_down",
    )(block_expert, n_used, hid, wd, wd, bd, bd)


def _combine_kernel(src_ref, dst_ref, src_next_ref, dst_next_ref, os_ref, tpos_ref, gate_ref, x1_ref, fg_ref,
                    op_ref, osm_ref, sorted_ref, sem, *, npt, nt):
    i = pl.program_id(0)
    slot = i % 2

    def copies(lists, buf, act):
        def make_copy(t_off, d, rows):
            return pltpu.make_async_copy(os_ref.at[pl.ds(d, rows), :], sorted_ref.at[buf, pl.ds(t_off, rows), :],
                                         sem.at[buf])
        _class_copies(*lists, make_copy, act)

    this_tile = (src_ref, dst_ref)
    pl.when(i == 0)(lambda: copies(this_tile, 0, lambda cp: cp.start()))
    pl.when(i + 1 < nt)(lambda: copies((src_next_ref, dst_next_ref), 1 - slot, lambda cp: cp.start()))
    copies(this_tile, slot, lambda cp: cp.wait())

    used = src_ref[N_CLASS, USED_LANE]
    tpos = tpos_ref[...]
    gates = gate_ref[...]
    y_lo = x1_ref[:, :HALF]
    y_hi = x1_ref[:, HALF:]
    for c in range(SORT_ROWS // PERM_CHUNK):
        p_lane = c * PERM_CHUNK + lax.broadcasted_iota(I32, (1, PERM_CHUNK), 1)
        place = jnp.zeros((tpos.shape[0], PERM_CHUNK), F32)
        for k in range(TOP_K):
            place = jnp.where(tpos[:, k:k + 1] == p_lane, gates[:, k:k + 1], place)
        p_row = c * PERM_CHUNK + lax.broadcasted_iota(I32, (PERM_CHUNK, 1), 0)
        u = jnp.where(p_row < used, sorted_ref[slot, c * PERM_CHUNK:(c + 1) * PERM_CHUNK, :], jnp.uint32(0))
        lo, hi = _unpack_pair(u)
        place = place.astype(BF16)
        y_lo = y_lo + jnp.dot(place, lo.astype(BF16), preferred_element_type=F32)
        y_hi = y_hi + jnp.dot(place, hi.astype(BF16), preferred_element_type=F32)
    ms = (jnp.sum(y_lo * y_lo, axis=-1, keepdims=True) + jnp.sum(y_hi * y_hi, axis=-1, keepdims=True)) / D
    inv = lax.rsqrt(ms + EPS)
    out = jnp.concatenate([y_lo * inv * fg_ref[:, :HALF], y_hi * inv * fg_ref[:, HALF:]], axis=-1)

    @pl.when(i < npt)
    def _():
        op_ref[...] = out

    @pl.when(i >= npt)
    def _():
        osm_ref[...] = out


def _combine(src_lists, dst_lists, out_sorted, tpos, gates, x1, fg, tp, tm):
    t = x1.shape[0]
    npt = tp // tm
    nt = t // tm
    tok = lambda i: (i, 0)
    this_tile = lambda i: (i, 0, 0)
    next_tile = lambda i: (jnp.minimum(i + 1, nt - 1), 0, 0)
    lists = lambda index_map: pl.BlockSpec((None, 8, LANE), index_map, memory_space=pltpu.SMEM)
    return pl.pallas_call(
        functools.partial(_combine_kernel, npt=npt, nt=nt),
        grid=(nt,),
        in_specs=[lists(this_tile), lists(this_tile), lists(next_tile), lists(next_tile),
                  pl.BlockSpec(memory_space=pl.ANY),
                  pl.BlockSpec((tm, LANE), tok), pl.BlockSpec((tm, LANE), tok), pl.BlockSpec((tm, D), tok),
                  pl.BlockSpec((1, D), lambda i: (0, 0))],
        out_specs=[pl.BlockSpec((tm, D), lambda i: (jnp.minimum(i, npt - 1), 0)),
                   pl.BlockSpec((tm, D), lambda i: (jnp.maximum(i - npt, 0), 0))],
        out_shape=[jax.ShapeDtypeStruct((tp, D), F32), jax.ShapeDtypeStruct((t - tp, D), F32)],
        scratch_shapes=[pltpu.VMEM((2, SORT_ROWS, HALF), U32), pltpu.SemaphoreType.DMA((2,))],
        compiler_params=_cp(("arbitrary",), 56),
        name="combine",
    )(src_lists, dst_lists, src_lists, dst_lists, out_sorted, tpos, gates, x1, fg)


def _rope_tables(length):
    half = DK // 2
    inv = ROPE_BASE ** (-jnp.arange(half, dtype=F32) / half)
    ang = jnp.arange(length, dtype=F32)[:, None] * inv[None, :]
    cos, sin = jnp.cos(ang), jnp.sin(ang)
    return jnp.concatenate([cos, cos], axis=-1), jnp.concatenate([-sin, sin], axis=-1)


def _layer(xp, xs, n_prompt, prompt_len, sample_len, mix_norm_g, w_in, ret_norm_g, w_ret_o, conv_w, conv_b, lru_w_a,
           lru_b_a, lru_w_x, lru_b_x, lru_lambda, w_lru_o, w_out, moe_norm_g, w_router, b_router, w_e_gate,
           b_e_gate, w_e_up, b_e_up, w_e_down, b_e_down, final_norm_g):
    tp = xp.shape[0]
    t = tp + xs.shape[0]
    seqs = _Seqs(n_prompt, prompt_len, sample_len, SEQ_CHUNK)
    row = lambda v: v.reshape(1, -1)

    z = _inproj(xp, xs, row(mix_norm_g), w_in.astype(BF16))

    cosf, sinf = _rope_tables(max(prompt_len, sample_len))
    y_ret = _retention(z, cosf, sinf, row(ret_norm_g), seqs, SEQ_CHUNK)

    wa, wx = lru_w_a.astype(BF16), lru_w_x.astype(BF16)
    ba, bx = lru_b_a[:, None, :], lru_b_x[:, None, :]
    c8 = (-LRU_C * math.log2(math.e) * jax.nn.softplus(-lru_lambda))[:, None, :]
    lru_args = (z, conv_w, row(conv_b), wa, ba, wx, bx, c8)
    lru_seqs = _Seqs(n_prompt, prompt_len, sample_len, LRU_CHUNK)
    fwd_scan = _lru_call(*lru_args, None, 0, lru_seqs, LRU_CHUNK, LRU_CB)
    y_lru = _lru_call(*lru_args, fwd_scan, 1, lru_seqs, LRU_CHUNK, LRU_CB)

    merged = _proj_merge(y_ret, y_lru, w_ret_o.astype(BF16), w_lru_o.astype(BF16), z)
    x1 = _proj_out(merged, w_out.astype(BF16), xp, xs)

    w_pad = jnp.zeros((D, LANE), F32).at[:, :N_EXP].set(w_router)
    b_pad = jnp.full((1, LANE), -1e30, F32).at[0, :N_EXP].set(b_router)
    w_hi = w_pad.astype(BF16)
    w_lo = (w_pad - w_hi.astype(F32)).astype(BF16)
    xn2, tpos, tpos_t, gates, meta = _router(x1, row(moe_norm_g), w_hi, w_lo, b_pad, TOK_TILE)

    nt = t // TOK_TILE
    c8 = meta[:, 0, :N_EXP].astype(I32)
    toff = meta[:, 1, :N_EXP].astype(I32)
    run = meta[:, 2, :N_EXP].astype(I32)
    total = run[-1] + c8[-1]
    padded = (total + MOE_ROWS - 1) // MOE_ROWS * MOE_ROWS
    pad_ends = jnp.cumsum(padded)
    pad_starts = pad_ends - padded
    src_lists, dst_lists = _copy_lists(toff, c8, pad_starts[None, :] + run)
    n_blocks = (t * TOP_K + (SEG_ALIGN - 1) * nt * N_EXP) // MOE_ROWS + 1 + N_EXP
    n_used = (pad_ends[-1] // MOE_ROWS).astype(I32)
    zblk = jnp.concatenate([jnp.where(padded > 0, pad_ends - MOE_ROWS, -1), n_used.reshape(1)]).astype(I32)
    blk_ids = jnp.minimum(jnp.arange(n_blocks, dtype=I32), n_used - 1)
    block_expert = jnp.minimum(
        jnp.sum((pad_ends[None, :] <= (blk_ids * MOE_ROWS)[:, None]).astype(I32), axis=-1), N_EXP - 1)

    xg = _dispatch(zblk, src_lists, dst_lists, tpos_t, xn2, n_blocks * MOE_ROWS)
    out_sorted = _experts(block_expert, n_used.reshape(1), xg,
                          w_e_gate, b_e_gate[:, None, :], w_e_up, b_e_up[:, None, :],
                          w_e_down, b_e_down[:, None, :], MOE_ROWS, MOE_TF, MOE_TN)
    return _combine(src_lists, dst_lists, out_sorted, tpos, gates, x1, row(final_norm_g), tp, TOK_TILE)


def kernel(x_prompt, x_sample, mix_norm_g, w_in, ret_norm_g, w_ret_o, conv_w, conv_b, lru_w_a, lru_b_a, lru_w_x,
           lru_b_x, lru_lambda, w_lru_o, w_out, moe_norm_g, w_router, b_router, w_e_gate, b_e_gate, w_e_up, b_e_up,
           w_e_down, b_e_down, final_norm_g):
    assert mix_norm_g.shape[0] == 1, "one layer"
    n_prompt, prompt_len, _ = x_prompt.shape
    n_sample, sample_len, _ = x_sample.shape
    assert n_sample == 1
    tp = n_prompt * prompt_len
    yp, ys = _layer(x_prompt.reshape(tp, D), x_sample.reshape(sample_len, D), n_prompt, prompt_len, sample_len,
                    mix_norm_g[0], w_in[0], ret_norm_g[0], w_ret_o[0], conv_w[0],
                    conv_b[0], lru_w_a[0], lru_b_a[0], lru_w_x[0], lru_b_x[0], lru_lambda[0], w_lru_o[0], w_out[0],
                    moe_norm_g[0], w_router[0], b_router[0], w_e_gate[0], b_e_gate[0], w_e_up[0], b_e_up[0],
                    w_e_down[0], b_e_down[0], final_norm_g)
    return yp.reshape(x_prompt.shape), ys.reshape(x_sample.shape)
```

```python
import functools
import math

import jax
import jax.numpy as jnp
from jax import lax
from jax.experimental import pallas as pl
from jax.experimental.pallas import tpu as pltpu

F32 = jnp.float32
BF16 = jnp.bfloat16
U32 = jnp.uint32
I32 = jnp.int32

D = 2048
HEADS = 8
DK = 128
DV = 256
QK_W = HEADS * DK
V_W = HEADS * DV
LRU_BLOCK = 128
N_EXP = 32
TOP_K = 4
D_FF = 2048
SWIGLU_LIMIT = 7.0
SWIGLU_ALPHA = 1.702
ROPE_BASE = 10000.0
LRU_C = 8.0
EPS = 1e-6
IN_W = 2 * QK_W + 2 * V_W + 2 * D + 2 * D

HALF = D // 2
LANE = 128
HALO = 16
HI_MASK = 0xFFFF0000

SEQ_CHUNK = 256
LRU_CHUNK = 512
LRU_CB = 1024
LRU_SEG = 4
LRU_SUPER = 8 * LRU_SEG
MOE_ROWS = 512
MOE_TF = 1024
MOE_TN = 1024
TOK_TILE = 512
SEG_ALIGN = 8
PERM_CHUNK = 256
SORT_ROWS = TOK_TILE * 4 + 32 * SEG_ALIGN
SEG_MAXBIT = (TOK_TILE // SEG_ALIGN).bit_length() - 1


def _cp(sem, vmem_mb):
    return pltpu.CompilerParams(dimension_semantics=sem, vmem_limit_bytes=vmem_mb << 20)


def _bits(x):
    return lax.bitcast_convert_type(x, U32)


def _pack_pair(lo, hi, rounded=False):
    if not rounded:
        lo = lo.astype(BF16).astype(F32)
        hi = hi.astype(BF16).astype(F32)
    return (_bits(lo) >> 16) | (_bits(hi) & jnp.uint32(HI_MASK))


def _unpack_pair(u):
    lo = lax.bitcast_convert_type(u << 16, F32)
    hi = lax.bitcast_convert_type(u & jnp.uint32(HI_MASK), F32)
    return lo, hi


class _Seqs:
    def __init__(self, n_prompt, prompt_len, sample_len, chunk):
        assert prompt_len % chunk == 0 and sample_len % chunk == 0
        self.cps = prompt_len // chunk
        self.npc = n_prompt * self.cps
        self.nch = self.npc + sample_len // chunk

    def is_first(self, ci):
        return ((ci < self.npc) & (ci % self.cps == 0)) | (ci == self.npc)

    def is_last(self, ci):
        return ((ci < self.npc) & (ci % self.cps == self.cps - 1)) | (ci == self.nch - 1)

    def pos_chunk(self, ci):
        return jnp.where(ci < self.npc, ci % self.cps, ci - self.npc)


def _two_group_specs(tm, npt, width=D):
    return [pl.BlockSpec((tm, width), lambda i, j: (jnp.minimum(i, npt - 1), 0)),
            pl.BlockSpec((tm, width), lambda i, j: (jnp.maximum(i - npt, 0), 0))]


def _inproj_kernel(xp_ref, xs_ref, g_ref, w_ref, o_ref, xn_ref, *, npt):
    i = pl.program_id(0)

    def norm(x_ref):
        x = x_ref[...]
        y = x * lax.rsqrt(jnp.mean(x * x, axis=-1, keepdims=True) + EPS)
        xn_ref[...] = (y * g_ref[...]).astype(BF16)

    @pl.when(pl.program_id(1) == 0)
    def _():
        pl.when(i < npt)(lambda: norm(xp_ref))
        pl.when(i >= npt)(lambda: norm(xs_ref))

    o_ref[...] = jnp.dot(xn_ref[...], w_ref[...], preferred_element_type=F32).astype(o_ref.dtype)


def _inproj(xp, xs, g, w, tm=1024, tn=1024):
    t, n = xp.shape[0] + xs.shape[0], w.shape[1]
    npt = xp.shape[0] // tm
    return pl.pallas_call(
        functools.partial(_inproj_kernel, npt=npt),
        grid=(t // tm, n // tn),
        in_specs=_two_group_specs(tm, npt) + [pl.BlockSpec((1, D), lambda i, j: (0, 0)),
                                              pl.BlockSpec((D, tn), lambda i, j: (0, j))],
        out_specs=pl.BlockSpec((tm, tn), lambda i, j: (i, j)),
        out_shape=jax.ShapeDtypeStruct((t, n), BF16),
        scratch_shapes=[pltpu.VMEM((tm, D), BF16)],
        compiler_params=_cp(("arbitrary", "arbitrary"), 56),
        name="inproj",
    )(xp, xs, g, w)


def _rot(x, cosf, sinf):
    return x * cosf + pltpu.roll(x, DK // 2, 1) * sinf


def _log_gamma(h):
    return math.log1p(-(2.0 ** (-5 - h)))


def _decay_tables(dec_ref, c, q_exponent, k_exponent):
    p = lax.broadcasted_iota(I32, (c, DK), 0).astype(F32)
    for h in range(HEADS):
        lg = _log_gamma(h)
        dec_ref[h, 0] = jnp.exp(lg * q_exponent(p))
        dec_ref[h, 1] = jnp.exp(lg * k_exponent(p))


def _ret_bwd_kernel(q_ref, k_ref, v_ref, cos_ref, sin_ref, o_ref, s_ref, dec_ref, *, seqs, c):
    ci = seqs.nch - 1 - pl.program_id(0)

    @pl.when(pl.program_id(0) == 0)
    def _():
        _decay_tables(dec_ref, c, lambda p: c - p, lambda p: p)

    @pl.when(seqs.is_last(ci))
    def _():
        s_ref[...] = jnp.zeros_like(s_ref)

    cosf = cos_ref[...]
    sinf = sin_ref[...]
    for h in range(HEADS):
        lg = _log_gamma(h)
        q = _rot(q_ref[:, h * DK:(h + 1) * DK].astype(F32), cosf, sinf)
        k = _rot(k_ref[:, h * DK:(h + 1) * DK].astype(F32), cosf, sinf) * (DK ** -0.5)
        v = v_ref[:, h * DV:(h + 1) * DV]
        qd = (q * dec_ref[h, 0]).astype(BF16)
        kd = (k * dec_ref[h, 1]).astype(BF16)
        s = s_ref[h]
        o_ref[:, h * DV:(h + 1) * DV] = jnp.dot(qd, s.astype(BF16), preferred_element_type=F32)
        s_ref[h] = math.exp(lg * c) * s + lax.dot_general(
            kd, v, (((0,), (0,)), ((), ())), preferred_element_type=F32)


def _ret_fwd_kernel(q_ref, k_ref, v_ref, cos_ref, sin_ref, bwd_ref, gate_ref, gn_ref, o_ref, s_ref, dm_ref,
                    dec_ref, *, seqs, c):
    ci = pl.program_id(0)

    @pl.when(ci == 0)
    def _():
        _decay_tables(dec_ref, c, lambda p: p + 1.0, lambda p: c - 1.0 - p)
        r = lax.broadcasted_iota(I32, (c, c), 0)
        col = lax.broadcasted_iota(I32, (c, c), 1)
        dist = jnp.abs(r - col).astype(F32)
        for h in range(HEADS):
            dm_ref[h] = jnp.exp(_log_gamma(h) * dist)

    @pl.when(seqs.is_first(ci))
    def _():
        s_ref[...] = jnp.zeros_like(s_ref)

    cosf = cos_ref[...]
    sinf = sin_ref[...]
    for h in range(HEADS):
        lg = _log_gamma(h)
        q = _rot(q_ref[:, h * DK:(h + 1) * DK].astype(F32), cosf, sinf)
        k = _rot(k_ref[:, h * DK:(h + 1) * DK].astype(F32), cosf, sinf) * (DK ** -0.5)
        v = v_ref[:, h * DV:(h + 1) * DV]
        scores = lax.dot_general(q.astype(BF16), k.astype(BF16), (((1,), (1,)), ((), ())),
                                 preferred_element_type=F32) * dm_ref[h]
        tot = jnp.dot(scores.astype(BF16), v, preferred_element_type=F32)
        qd = (q * dec_ref[h, 0]).astype(BF16)
        kd = (k * dec_ref[h, 1]).astype(BF16)
        s = s_ref[h]
        tot = tot + jnp.dot(qd, s.astype(BF16), preferred_element_type=F32)
        s_ref[h] = math.exp(lg * c) * s + lax.dot_general(
            kd, v, (((0,), (0,)), ((), ())), preferred_element_type=F32)
        tot = tot + bwd_ref[:, h * DV:(h + 1) * DV]
        mu = jnp.mean(tot, axis=-1, keepdims=True)
        cen = tot - mu
        var = jnp.mean(cen * cen, axis=-1, keepdims=True)
        yn = cen * lax.rsqrt(var + EPS) * gn_ref[:, h * DV:(h + 1) * DV]
        g = gate_ref[:, h * DV:(h + 1) * DV].astype(F32)
        o_ref[:, h * DV:(h + 1) * DV] = (yn * (g * jax.nn.sigmoid(g))).astype(o_ref.dtype)


def _retention(z, cosf, sinf, gn, seqs, c):
    t = z.shape[0]
    nch = seqs.nch
    rev = lambda i: nch - 1 - i
    qkv_specs = lambda f: [pl.BlockSpec((c, QK_W), lambda i: (f(i), 0)),
                           pl.BlockSpec((c, QK_W), lambda i: (f(i), 1)),
                           pl.BlockSpec((c, V_W), lambda i: (f(i), 1)),
                           pl.BlockSpec((c, DK), lambda i: (seqs.pos_chunk(f(i)), 0)),
                           pl.BlockSpec((c, DK), lambda i: (seqs.pos_chunk(f(i)), 0))]
    bwd = pl.pallas_call(
        functools.partial(_ret_bwd_kernel, seqs=seqs, c=c),
        grid=(nch,),
        in_specs=qkv_specs(rev),
        out_specs=pl.BlockSpec((c, V_W), lambda i: (rev(i), 0)),
        out_shape=jax.ShapeDtypeStruct((t, V_W), F32),
        scratch_shapes=[pltpu.VMEM((HEADS, DK, DV), F32), pltpu.VMEM((HEADS, 2, c, DK), F32)],
        compiler_params=_cp(("arbitrary",), 32),
        name="ret_bwd",
    )(z, z, z, cosf, sinf)
    fwd_id = lambda i: i
    return pl.pallas_call(
        functools.partial(_ret_fwd_kernel, seqs=seqs, c=c),
        grid=(nch,),
        in_specs=qkv_specs(fwd_id) + [pl.BlockSpec((c, V_W), lambda i: (i, 0)),
                                      pl.BlockSpec((c, V_W), lambda i: (i, 2)),
                                      pl.BlockSpec((1, V_W), lambda i: (0, 0))],
        out_specs=pl.BlockSpec((c, V_W), lambda i: (i, 0)),
        out_shape=jax.ShapeDtypeStruct((t, V_W), BF16),
        scratch_shapes=[pltpu.VMEM((HEADS, DK, DV), F32), pltpu.VMEM((HEADS, c, c), F32),
                        pltpu.VMEM((HEADS, 2, c, DK), F32)],
        compiler_params=_cp(("arbitrary",), 40),
        name="ret_fwd",
    )(z, z, z, cosf, sinf, bwd, z, gn)


def _lru_kernel(*refs, reverse, seqs, c, cb):
    if reverse:
        (xc_ref, wa_ref, ba_ref, wx_ref, bx_ref, c8_ref, hf_ref, g_ref, o_ref, a_s, u_s, h_s, carry_s) = refs
    else:
        (x_ref, xp_ref, xn_ref, cw_ref, cbias_ref, wa_ref, ba_ref, wx_ref, bx_ref, c8_ref,
         o_ref, xc_out_ref, a_s, u_s, h_s, carry_s) = refs
    t = pl.program_id(1)
    ci = seqs.nch - 1 - t if reverse else t
    first = seqs.is_first(ci)
    last = seqs.is_last(ci)

    if reverse:
        xc = xc_ref[...]
    else:
        x = x_ref[...].astype(F32)
        prev = jnp.where(first, 0.0, xp_ref[...].astype(F32))
        nxt = jnp.where(last, 0.0, xn_ref[...].astype(F32))
        row8 = lax.broadcasted_iota(I32, (8, 1), 0)

        def patch(arr, at, rows8):
            parts = ([arr[:at]] if at > 0 else []) + [rows8] + ([arr[at + 8:]] if at + 8 < c else [])
            return jnp.concatenate(parts, axis=0)

        xm1 = pltpu.roll(x, 1, 0)
        xm1 = patch(xm1, 0, jnp.where(row8 == 0, prev[HALO - 1:HALO], xm1[0:8]))
        xm2 = pltpu.roll(x, 2, 0)
        xm2 = patch(xm2, 0, jnp.where(row8 == 0, prev[HALO - 2:HALO - 1],
                                      jnp.where(row8 == 1, prev[HALO - 1:HALO], xm2[0:8])))
        xp1 = pltpu.roll(x, c - 1, 0)
        xp1 = patch(xp1, c - 8, jnp.where(row8 == 7, nxt[0:1], xp1[c - 8:c]))
        cw = cw_ref[...]
        xc = cw[0:1] * xm2 + cw[1:2] * xm1 + cw[2:3] * x + cw[3:4] * xp1 + cbias_ref[...]
        xc_out_ref[...] = xc

    nslab = cb // LRU_BLOCK
    for gi in range(nslab):
        sl = slice(gi * LRU_BLOCK, (gi + 1) * LRU_BLOCK)
        xs = xc[:, sl]
        xb = xs.astype(BF16)
        r = jax.nn.sigmoid(jnp.dot(xb, wa_ref[gi], preferred_element_type=F32) + ba_ref[:, sl])
        ig = jax.nn.sigmoid(jnp.dot(xb, wx_ref[gi], preferred_element_type=F32) + bx_ref[:, sl])
        a = jnp.exp2(c8_ref[:, sl] * r)
        a_s[gi] = a
        v = 1.0 - a * a
        u_s[gi] = jnp.where(v > 0.0, v * lax.rsqrt(v), 0.0) * (ig * xs)

    @pl.when(last if reverse else first)
    def _():
        carry_s[...] = jnp.zeros_like(carry_s)

    sub = lax.broadcasted_iota(I32, (8, LRU_BLOCK), 0)
    nsuper = c // LRU_SUPER
    steps = list(range(LRU_SEG))
    if reverse:
        steps = steps[::-1]

    def super_group(q, carries):
        qi = nsuper - 1 - q if reverse else q
        base = qi * LRU_SUPER
        out = []
        for gi in range(nslab):
            hs, ps = {}, {}
            h = p = None
            for j in steps:
                rows = pl.ds(base + j, 8, stride=LRU_SEG)
                a = a_s[gi, rows, :]
                u = u_s[gi, rows, :]
                h = u if h is None else a * h + u
                p = a if p is None else a * p
                hs[j], ps[j] = h, p
            eh, ep = h, p
            for s in (1, 2, 4):
                shift = 8 - s if reverse else s
                m = (sub < 8 - s) if reverse else (sub >= s)
                eh_sh = pltpu.roll(eh, shift, 0)
                ep_sh = pltpu.roll(ep, shift, 0)
                eh = eh + ep * jnp.where(m, eh_sh, 0.0)
                ep = ep * jnp.where(m, ep_sh, 1.0)
            end = eh + ep * carries[gi]
            if reverse:
                enter = jnp.where(sub == 7, carries[gi], pltpu.roll(end, 7, 0))
                out.append(jnp.broadcast_to(end[0:1], (8, LRU_BLOCK)))
            else:
                enter = jnp.where(sub == 0, carries[gi], pltpu.roll(end, 1, 0))
                out.append(jnp.broadcast_to(end[7:8], (8, LRU_BLOCK)))
            for j in steps:
                h_s[gi, pl.ds(base + j, 8, stride=LRU_SEG), :] = hs[j] + ps[j] * enter
        return tuple(out)

    carries = lax.fori_loop(0, nsuper, super_group, tuple(carry_s[gi] for gi in range(nslab)))
    for gi in range(nslab):
        carry_s[gi] = carries[gi]

    for gi in range(nslab):
        sl = slice(gi * LRU_BLOCK, (gi + 1) * LRU_BLOCK)
        if reverse:
            g = g_ref[:, sl].astype(F32)
            o_ref[:, sl] = ((hf_ref[:, sl] + h_s[gi]) * jax.nn.gelu(g, approximate=True)).astype(o_ref.dtype)
        else:
            o_ref[:, sl] = h_s[gi]


def _lru_call(z, conv_w, conv_b, wa, ba, wx, bx, c8, fwd, direction, seqs, c, cb):
    t = z.shape[0]
    nch = seqs.nch
    reverse = direction == 1
    tmap = (lambda ti: nch - 1 - ti) if reverse else (lambda ti: ti)
    xcol = (2 * QK_W + 2 * V_W) // cb
    gcol = (2 * QK_W + 2 * V_W + D) // cb
    hb = c // HALO
    nhb = t // HALO
    nb = cb // LRU_BLOCK
    tile = pl.BlockSpec((c, cb), lambda ch, ti: (tmap(ti), ch))
    gate_specs = [
        pl.BlockSpec((None, nb, LRU_BLOCK, LRU_BLOCK), lambda ch, ti: (direction, ch, 0, 0)),
        pl.BlockSpec((None, 1, cb), lambda ch, ti: (direction, 0, ch)),
        pl.BlockSpec((None, nb, LRU_BLOCK, LRU_BLOCK), lambda ch, ti: (direction, ch, 0, 0)),
        pl.BlockSpec((None, 1, cb), lambda ch, ti: (direction, 0, ch)),
        pl.BlockSpec((None, 1, cb), lambda ch, ti: (direction, 0, ch)),
    ]
    if reverse:
        h_fwd, xc = fwd
        in_specs = [tile] + gate_specs + [tile, pl.BlockSpec((c, cb), lambda ch, ti: (tmap(ti), gcol + ch))]
        args = [xc, wa, ba, wx, bx, c8, h_fwd, z]
        out_specs = tile
        out_shape = jax.ShapeDtypeStruct((t, D), BF16)
    else:
        in_specs = [
            pl.BlockSpec((c, cb), lambda ch, ti: (tmap(ti), xcol + ch)),
            pl.BlockSpec((HALO, cb), lambda ch, ti: (jnp.maximum(tmap(ti) * hb - 1, 0), xcol + ch)),
            pl.BlockSpec((HALO, cb), lambda ch, ti: (jnp.minimum((tmap(ti) + 1) * hb, nhb - 1), xcol + ch)),
            pl.BlockSpec((4, cb), lambda ch, ti: (0, ch)),
            pl.BlockSpec((1, cb), lambda ch, ti: (0, ch)),
        ] + gate_specs
        args = [z, z, z, conv_w, conv_b, wa, ba, wx, bx, c8]
        out_specs = [tile, tile]
        out_shape = [jax.ShapeDtypeStruct((t, D), F32), jax.ShapeDtypeStruct((t, D), F32)]
    return pl.pallas_call(
        functools.partial(_lru_kernel, reverse=reverse, seqs=seqs, c=c, cb=cb),
        grid=(D // cb, nch),
        in_specs=in_specs,
        out_specs=out_specs,
        out_shape=out_shape,
        scratch_shapes=[pltpu.VMEM((nb, c, LRU_BLOCK), F32), pltpu.VMEM((nb, c, LRU_BLOCK), F32),
                        pltpu.VMEM((nb, c, LRU_BLOCK), F32), pltpu.VMEM((nb, 8, LRU_BLOCK), F32)],
        compiler_params=_cp(("arbitrary", "arbitrary"),32),
        name="lru_rev" if reverse else "lru_fwd",
    )(*args)


def _proj_merge_kernel(yr_ref, yl_ref, wr_ref, wl_ref, mr_ref, ml_ref, o_ref):
    r = jnp.dot(yr_ref[...], wr_ref[...], preferred_element_type=F32)
    l = jnp.dot(yl_ref[...], wl_ref[...], preferred_element_type=F32)
    o_ref[...] = (jax.nn.sigmoid(mr_ref[...].astype(F32)) * r
                  + jax.nn.sigmoid(ml_ref[...].astype(F32)) * l).astype(o_ref.dtype)


def _proj_merge(y_ret, y_lru, w_ret_o, w_lru_o, z, tm=1024, tn=1024):
    t = y_ret.shape[0]
    mcol = (2 * QK_W + 2 * V_W + 2 * D) // tn
    return pl.pallas_call(
        _proj_merge_kernel,
        grid=(D // tn, t // tm),
        in_specs=[pl.BlockSpec((tm, V_W), lambda j, i: (i, 0)),
                  pl.BlockSpec((tm, D), lambda j, i: (i, 0)),
                  pl.BlockSpec((V_W, tn), lambda j, i: (0, j)),
                  pl.BlockSpec((D, tn), lambda j, i: (0, j)),
                  pl.BlockSpec((tm, tn), lambda j, i: (i, mcol + j)),
                  pl.BlockSpec((tm, tn), lambda j, i: (i, mcol + D // tn + j))],
        out_specs=pl.BlockSpec((tm, tn), lambda j, i: (i, j)),
        out_shape=jax.ShapeDtypeStruct((t, D), BF16),
        compiler_params=_cp(("arbitrary", "arbitrary"), 56),
        name="proj_merge",
    )(y_ret, y_lru, w_ret_o, w_lru_o, z, z)


def _proj_out_kernel(m_ref, w_ref, xp_ref, xs_ref, o_ref, *, npt):
    i = pl.program_id(1)
    y = jnp.dot(m_ref[...], w_ref[...], preferred_element_type=F32)

    @pl.when(i < npt)
    def _():
        o_ref[...] = xp_ref[...] + y

    @pl.when(i >= npt)
    def _():
        o_ref[...] = xs_ref[...] + y


def _proj_out(merged, w_out, xp, xs, tm=1024, tn=1024):
    t = merged.shape[0]
    npt = xp.shape[0] // tm
    return pl.pallas_call(
        functools.partial(_proj_out_kernel, npt=npt),
        grid=(D // tn, t // tm),
        in_specs=[pl.BlockSpec((tm, D), lambda j, i: (i, 0)),
                  pl.BlockSpec((D, tn), lambda j, i: (0, j)),
                  pl.BlockSpec((tm, tn), lambda j, i: (jnp.minimum(i, npt - 1), j)),
                  pl.BlockSpec((tm, tn), lambda j, i: (jnp.maximum(i - npt, 0), j))],
        out_specs=pl.BlockSpec((tm, tn), lambda j, i: (i, j)),
        out_shape=jax.ShapeDtypeStruct((t, D), F32),
        compiler_params=_cp(("arbitrary", "arbitrary"), 48),
        name="proj_out",
    )(merged, w_out, xp, xs)


def _router_kernel(x_ref, g_ref, whi_ref, wlo_ref, b_ref, xn_ref, tpos_ref, tpos_t_ref, gate_ref, meta_ref, tri_ref,
                   run_ref, *, tm):
    i = pl.program_id(0)

    @pl.when(i == 0)
    def _():
        r = lax.broadcasted_iota(I32, (tm, tm), 0)
        col = lax.broadcasted_iota(I32, (tm, tm), 1)
        tri_ref[...] = (col < r).astype(BF16)
        run_ref[...] = jnp.zeros_like(run_ref)

    x = x_ref[...]
    xn = x * lax.rsqrt(jnp.mean(x * x, axis=-1, keepdims=True) + EPS) * g_ref[...]
    xh = xn.astype(BF16)
    xn_ref[...] = xh

    xl = (xn - xh.astype(F32)).astype(BF16)
    logits = (jnp.dot(xh, whi_ref[...], preferred_element_type=F32)
              + (jnp.dot(xl, whi_ref[...], preferred_element_type=F32)
                 + jnp.dot(xh, wlo_ref[...], preferred_element_type=F32))) + b_ref[...]
    lane = lax.broadcasted_iota(I32, (tm, LANE), 1)
    lane_f = lane.astype(F32)
    vals, idxs = [], []
    cur = logits
    for _ in range(TOP_K):
        m = jnp.max(cur, axis=-1, keepdims=True)
        idx = jnp.min(jnp.where(cur == m, lane_f, float(LANE)), axis=-1, keepdims=True).astype(I32)
        vals.append(m)
        idxs.append(idx)
        cur = jnp.where(lane == idx, -jnp.inf, cur)
    exps = [jnp.exp(v - vals[0]) for v in vals]
    denom = exps[0] + exps[1] + exps[2] + exps[3]

    onehot = jnp.zeros((tm, LANE), F32)
    for idx in idxs:
        onehot = onehot + (lane == idx).astype(F32)
    before = jnp.dot(tri_ref[...], onehot.astype(BF16), preferred_element_type=F32)

    cnt = jnp.broadcast_to(jnp.sum(onehot, axis=0, keepdims=True), (8, LANE))
    cnt8 = jnp.floor((cnt + (SEG_ALIGN - 1)) * (1.0 / SEG_ALIGN)) * SEG_ALIGN
    lane8 = lax.broadcasted_iota(I32, (8, LANE), 1)
    incl = cnt8
    s = 1
    while s < LANE:
        incl = incl + jnp.where(lane8 >= s, pltpu.roll(incl, s, 1), 0.0)
        s *= 2
    toff = incl - cnt8

    pos = before + toff[0:1]
    tpos_out = jnp.zeros((tm, LANE), F32)
    gate_out = jnp.zeros((tm, LANE), F32)
    for k in range(TOP_K):
        tpos_k = jnp.sum(jnp.where(lane == idxs[k], pos, 0.0), axis=-1, keepdims=True)
        tpos_out = jnp.where(lane == k, tpos_k, tpos_out)
        gate_out = jnp.where(lane == k, exps[k] / denom, gate_out)
    tpos_ref[...] = tpos_out.astype(I32)
    tpos_t_ref[...] = tpos_out.T[0:8].astype(I32)
    gate_ref[...] = gate_out

    sub8 = lax.broadcasted_iota(I32, (8, LANE), 0)
    run = jnp.broadcast_to(run_ref[...], (8, LANE))
    meta_ref[...] = jnp.where(sub8 == 0, cnt8, jnp.where(sub8 == 1, toff, jnp.where(sub8 == 2, run, 0.0)))
    run_ref[...] = run_ref[...] + cnt8[0:1]


def _router(x1, g, w_hi, w_lo, b_pad, tm):
    t = x1.shape[0]
    nt = t // tm
    tok = lambda i: (i, 0)
    fixed = lambda i: (0, 0)
    per_tile = lambda i: (i, 0, 0)
    return pl.pallas_call(
        functools.partial(_router_kernel, tm=tm),
        grid=(nt,),
        in_specs=[pl.BlockSpec((tm, D), tok), pl.BlockSpec((1, D), fixed),
                  pl.BlockSpec((D, LANE), fixed), pl.BlockSpec((D, LANE), fixed), pl.BlockSpec((1, LANE), fixed)],
        out_specs=[pl.BlockSpec((tm, D), tok), pl.BlockSpec((tm, LANE), tok),
                   pl.BlockSpec((None, 8, tm), per_tile), pl.BlockSpec((tm, LANE), tok),
                   pl.BlockSpec((None, 8, LANE), per_tile)],
        out_shape=[jax.ShapeDtypeStruct((t, D), BF16), jax.ShapeDtypeStruct((t, LANE), I32),
                   jax.ShapeDtypeStruct((nt, 8, tm), I32), jax.ShapeDtypeStruct((t, LANE), F32),
                   jax.ShapeDtypeStruct((nt, 8, LANE), F32)],
        scratch_shapes=[pltpu.VMEM((tm, tm), BF16), pltpu.VMEM((1, LANE), F32)],
        compiler_params=_cp(("arbitrary",), 32),
        name="router",
    )(x1, g, w_hi, w_lo, b_pad)


N_CLASS = SEG_MAXBIT + 1
USED_LANE = N_CLASS + 1


def _copy_lists(toff, c8, dst):
    nt = c8.shape[0]
    n = (c8 // SEG_ALIGN)[:, None, :]
    b = jnp.arange(N_CLASS, dtype=I32)[None, :, None]
    bits = (n >> b) & 1
    off = ((n >> (b + 1)) << (b + 1)) * SEG_ALIGN
    order = jnp.argsort(1 - bits, axis=-1, stable=True)
    src = jnp.take_along_axis(toff[:, None, :] + off, order, axis=-1)
    dstl = jnp.take_along_axis(dst[:, None, :] + off, order, axis=-1)
    src_t = jnp.zeros((nt, 8, LANE), I32).at[:, :N_CLASS, :N_EXP].set(src)
    src_t = src_t.at[:, N_CLASS, :N_CLASS].set(jnp.sum(bits, axis=-1))
    src_t = src_t.at[:, N_CLASS, USED_LANE].set(toff[:, -1] + c8[:, -1])
    dst_t = jnp.zeros((nt, 8, LANE), I32).at[:, :N_CLASS, :N_EXP].set(dstl)
    return src_t, dst_t


def _class_copies(src_ref, dst_ref, make_copy, act):
    for b in range(N_CLASS):
        rows = SEG_ALIGN << b

        def body(s, carry, b=b, rows=rows):
            act(make_copy(pl.multiple_of(src_ref[b, s], SEG_ALIGN), pl.multiple_of(dst_ref[b, s], SEG_ALIGN), rows))
            return carry

        lax.fori_loop(0, src_ref[N_CLASS, b], body, 0)


def _dispatch_kernel(zblk_ref, src_ref, dst_ref, src_prev_ref, dst_prev_ref, tpos_t_ref, x_ref, xg_ref, sorted_ref,
                     zero_ref, sem, *, nt):
    i = pl.program_id(0)
    slot = i % 2

    def zero_copy(e):
        start = pl.multiple_of(jnp.maximum(zblk_ref[e], 0), MOE_ROWS)
        return pltpu.make_async_copy(zero_ref, xg_ref.at[pl.ds(start, MOE_ROWS), :], sem.at[0])

    def for_nonempty(act):
        def body(e, carry):
            pl.when(zblk_ref[e] >= 0)(lambda: act(zero_copy(e)))
            return carry
        lax.fori_loop(0, N_EXP, body, 0)

    def for_unused(act):
        def body(b, carry):
            start = pl.multiple_of(b * MOE_ROWS, MOE_ROWS)
            act(pltpu.make_async_copy(zero_ref, xg_ref.at[pl.ds(start, MOE_ROWS), :], sem.at[0]))
            return carry
        lax.fori_loop(zblk_ref[N_EXP], xg_ref.shape[0] // MOE_ROWS, body, 0)

    @pl.when(i == 0)
    def _():
        zero_ref[...] = jnp.zeros_like(zero_ref)
        for_nonempty(lambda cp: cp.start())
        for_unused(lambda cp: cp.start())
        for_nonempty(lambda cp: cp.wait())
        for_unused(lambda cp: cp.wait())

    x = x_ref[...]
    for c in range(SORT_ROWS // PERM_CHUNK):
        p = c * PERM_CHUNK + lax.broadcasted_iota(I32, (PERM_CHUNK, 1), 0)
        perm = jnp.zeros((PERM_CHUNK, x.shape[0]), F32)
        for k in range(TOP_K):
            perm = perm + jnp.where(p == tpos_t_ref[k:k + 1, :], 1.0, 0.0)
        rows = jnp.dot(perm.astype(BF16), x, preferred_element_type=F32)
        sorted_ref[slot, c * PERM_CHUNK:(c + 1) * PERM_CHUNK, :] = _pack_pair(rows[:, :HALF], rows[:, HALF:],
                                                                              rounded=True)

    def copies(lists, buf, act):
        def make_copy(t_off, d, rows):
            return pltpu.make_async_copy(sorted_ref.at[buf, pl.ds(t_off, rows), :], xg_ref.at[pl.ds(d, rows), :],
                                         sem.at[buf])
        _class_copies(*lists, make_copy, act)

    this_tile = (src_ref, dst_ref)
    pl.when(i > 0)(lambda: copies((src_prev_ref, dst_prev_ref), 1 - slot, lambda cp: cp.wait()))
    copies(this_tile, slot, lambda cp: cp.start())
    pl.when(i == nt - 1)(lambda: copies(this_tile, slot, lambda cp: cp.wait()))


def _dispatch(zblk, src_lists, dst_lists, tpos_t, xn, rows):
    nt, _, tm = tpos_t.shape
    this_tile = lambda i, *_: (i, 0, 0)
    prev_tile = lambda i, *_: (jnp.maximum(i - 1, 0), 0, 0)
    lists = lambda index_map: pl.BlockSpec((None, 8, LANE), index_map, memory_space=pltpu.SMEM)
    grid_spec = pltpu.PrefetchScalarGridSpec(
        num_scalar_prefetch=1,
        grid=(nt,),
        in_specs=[lists(this_tile), lists(this_tile), lists(prev_tile), lists(prev_tile),
                  pl.BlockSpec((None, 8, tm), this_tile),
                  pl.BlockSpec((tm, D), lambda i, *_: (i, 0))],
        out_specs=pl.BlockSpec(memory_space=pl.ANY),
        scratch_shapes=[pltpu.VMEM((2, SORT_ROWS, HALF), U32), pltpu.VMEM((MOE_ROWS, HALF), U32),
                        pltpu.SemaphoreType.DMA((2,))],
    )
    return pl.pallas_call(
        functools.partial(_dispatch_kernel, nt=nt),
        grid_spec=grid_spec,
        out_shape=jax.ShapeDtypeStruct((rows, HALF), U32),
        compiler_params=_cp(("arbitrary",), 40),
        name="dispatch",
    )(zblk, src_lists, dst_lists, src_lists, dst_lists, tpos_t, xn)


def _expert_changed(be_ref, i):
    return (i == 0) | (be_ref[i] != be_ref[jnp.maximum(i - 1, 0)])


def _expert_up_kernel(be_ref, nu_ref, xg_ref, wg_ref, bg_ref, wu_ref, bu_ref, h_ref, wgb_ref, wub_ref):
    i = pl.program_id(1)

    @pl.when(i < nu_ref[0])
    def _():
        @pl.when(_expert_changed(be_ref, i))
        def _():
            wgb_ref[...] = wg_ref[...].astype(BF16)
            wub_ref[...] = wu_ref[...].astype(BF16)

        def compute(n):
            lo, hi = _unpack_pair(xg_ref[0:n, :])
            xb = jnp.concatenate([lo.astype(BF16), hi.astype(BF16)], axis=-1)
            gate = jnp.minimum(jnp.dot(xb, wgb_ref[...], preferred_element_type=F32) + bg_ref[...], SWIGLU_LIMIT)
            up = jnp.clip(jnp.dot(xb, wub_ref[...], preferred_element_type=F32) + bu_ref[...],
                          -SWIGLU_LIMIT, SWIGLU_LIMIT)
            h_ref[0:n, :] = ((up + 1.0) * gate * jax.nn.sigmoid(SWIGLU_ALPHA * gate)).astype(h_ref.dtype)

        rows, half = h_ref.shape[0], h_ref.shape[0] // 2
        pl.when(nu_ref[1 + i] == 0)(lambda: compute(rows))

        @pl.when(nu_ref[1 + i] != 0)
        def _():
            compute(half)
            h_ref[half:, :] = jnp.zeros((rows - half, h_ref.shape[1]), h_ref.dtype)

    @pl.when(i >= nu_ref[0])
    def _():
        h_ref[...] = jnp.zeros_like(h_ref)


def _expert_down_kernel(be_ref, nu_ref, h_ref, wlo_ref, whi_ref, blo_ref, bhi_ref, o_ref, wlob_ref, whib_ref):
    i = pl.program_id(1)

    @pl.when(i < nu_ref[0])
    def _():
        @pl.when(_expert_changed(be_ref, i))
        def _():
            wlob_ref[...] = wlo_ref[...].astype(BF16)
            whib_ref[...] = whi_ref[...].astype(BF16)

        def compute(n):
            h = h_ref[0:n, :]
            lo = jnp.dot(h, wlob_ref[...], preferred_element_type=F32) + blo_ref[...]
            hi = jnp.dot(h, whib_ref[...], preferred_element_type=F32) + bhi_ref[...]
            o_ref[0:n, :] = _pack_pair(lo, hi)

        rows, half = o_ref.shape[0], o_ref.shape[0] // 2
        pl.when(nu_ref[1 + i] == 0)(lambda: compute(rows))

        @pl.when(nu_ref[1 + i] != 0)
        def _():
            compute(half)
            o_ref[half:, :] = jnp.zeros((rows - half, o_ref.shape[1]), o_ref.dtype)

    @pl.when(i >= nu_ref[0])
    def _():
        o_ref[...] = jnp.zeros_like(o_ref)


def _experts(block_expert, n_used, xg, wg, bg, wu, bu, wd, bd, rows_blk, tf, tn):
    rows = xg.shape[0]
    nb = rows // rows_blk
    nj = D_FF // tf
    nn = HALF // tn

    def blk(i, nu):
        return jnp.minimum(i, nu[0] - 1)

    up_spec = pltpu.PrefetchScalarGridSpec(
        num_scalar_prefetch=2,
        grid=(nj, nb),
        in_specs=[pl.BlockSpec((rows_blk, HALF), lambda j, i, be, nu: (blk(i, nu), 0)),
                  pl.BlockSpec((None, D, tf), lambda j, i, be, nu: (be[i], 0, j)),
                  pl.BlockSpec((None, 1, tf), lambda j, i, be, nu: (be[i], 0, j)),
                  pl.BlockSpec((None, D, tf), lambda j, i, be, nu: (be[i], 0, j)),
                  pl.BlockSpec((None, 1, tf), lambda j, i, be, nu: (be[i], 0, j))],
        out_specs=pl.BlockSpec((rows_blk, tf), lambda j, i, be, nu: (i, j)),
        scratch_shapes=[pltpu.VMEM((D, tf), BF16), pltpu.VMEM((D, tf), BF16)],
    )
    hid = pl.pallas_call(
        _expert_up_kernel,
        grid_spec=up_spec,
        out_shape=jax.ShapeDtypeStruct((rows, D_FF), BF16),
        compiler_params=_cp(("arbitrary", "arbitrary"), 60),
        name="experts_up",
    )(block_expert, n_used, xg, wg, bg, wu, bu)

    down_spec = pltpu.PrefetchScalarGridSpec(
        num_scalar_prefetch=2,
        grid=(nn, nb),
        in_specs=[pl.BlockSpec((rows_blk, D_FF), lambda n, i, be, nu: (blk(i, nu), 0)),
                  pl.BlockSpec((None, D_FF, tn), lambda n, i, be, nu: (be[i], 0, n)),
                  pl.BlockSpec((None, D_FF, tn), lambda n, i, be, nu: (be[i], 0, nn + n)),
                  pl.BlockSpec((None, 1, tn), lambda n, i, be, nu: (be[i], 0, n)),
                  pl.BlockSpec((None, 1, tn), lambda n, i, be, nu: (be[i], 0, nn + n))],
        out_specs=pl.BlockSpec((rows_blk, tn), lambda n, i, be, nu: (i, n)),
        scratch_shapes=[pltpu.VMEM((D_FF, tn), BF16), pltpu.VMEM((D_FF, tn), BF16)],
    )
    return pl.pallas_call(
        _expert_down_kernel,
        grid_spec=down_spec,
        out_shape=jax.ShapeDtypeStruct((rows, HALF), U32),
        compiler_params=_cp(("arbitrary", "arbitrary"), 60),
        name="experts_down",
    )(block_expert, n_used, hid, wd, wd, bd, bd)


def _combine_kernel(src_ref, dst_ref, src_next_ref, dst_next_ref, os_ref, tpos_ref, gate_ref, x1_ref, fg_ref,
                    op_ref, osm_ref, sorted_ref, sem, *, npt, nt):
    i = pl.program_id(0)
    slot = i % 2

    def copies(lists, buf, act):
        def make_copy(t_off, d, rows):
            return pltpu.make_async_copy(os_ref.at[pl.ds(d, rows), :], sorted_ref.at[buf, pl.ds(t_off, rows), :],
                                         sem.at[buf])
        _class_copies(*lists, make_copy, act)

    this_tile = (src_ref, dst_ref)
    pl.when(i == 0)(lambda: copies(this_tile, 0, lambda cp: cp.start()))
    pl.when(i + 1 < nt)(lambda: copies((src_next_ref, dst_next_ref), 1 - slot, lambda cp: cp.start()))
    copies(this_tile, slot, lambda cp: cp.wait())

    used = src_ref[N_CLASS, USED_LANE]
    tpos = tpos_ref[...]
    gates = gate_ref[...]
    y_lo = x1_ref[:, :HALF]
    y_hi = x1_ref[:, HALF:]
    for c in range(SORT_ROWS // PERM_CHUNK):
        p_lane = c * PERM_CHUNK + lax.broadcasted_iota(I32, (1, PERM_CHUNK), 1)
        place = jnp.zeros((tpos.shape[0], PERM_CHUNK), F32)
        for k in range(TOP_K):
            place = jnp.where(tpos[:, k:k + 1] == p_lane, gates[:, k:k + 1], place)
        p_row = c * PERM_CHUNK + lax.broadcasted_iota(I32, (PERM_CHUNK, 1), 0)
        u = jnp.where(p_row < used, sorted_ref[slot, c * PERM_CHUNK:(c + 1) * PERM_CHUNK, :], jnp.uint32(0))
        lo, hi = _unpack_pair(u)
        place = place.astype(BF16)
        y_lo = y_lo + jnp.dot(place, lo.astype(BF16), preferred_element_type=F32)
        y_hi = y_hi + jnp.dot(place, hi.astype(BF16), preferred_element_type=F32)
    ms = (jnp.sum(y_lo * y_lo, axis=-1, keepdims=True) + jnp.sum(y_hi * y_hi, axis=-1, keepdims=True)) / D
    inv = lax.rsqrt(ms + EPS)
    out = jnp.concatenate([y_lo * inv * fg_ref[:, :HALF], y_hi * inv * fg_ref[:, HALF:]], axis=-1)

    @pl.when(i < npt)
    def _():
        op_ref[...] = out

    @pl.when(i >= npt)
    def _():
        osm_ref[...] = out


def _combine(src_lists, dst_lists, out_sorted, tpos, gates, x1, fg, tp, tm):
    t = x1.shape[0]
    npt = tp // tm
    nt = t // tm
    tok = lambda i: (i, 0)
    this_tile = lambda i: (i, 0, 0)
    next_tile = lambda i: (jnp.minimum(i + 1, nt - 1), 0, 0)
    lists = lambda index_map: pl.BlockSpec((None, 8, LANE), index_map, memory_space=pltpu.SMEM)
    return pl.pallas_call(
        functools.partial(_combine_kernel, npt=npt, nt=nt),
        grid=(nt,),
        in_specs=[lists(this_tile), lists(this_tile), lists(next_tile), lists(next_tile),
                  pl.BlockSpec(memory_space=pl.ANY),
                  pl.BlockSpec((tm, LANE), tok), pl.BlockSpec((tm, LANE), tok), pl.BlockSpec((tm, D), tok),
                  pl.BlockSpec((1, D), lambda i: (0, 0))],
        out_specs=[pl.BlockSpec((tm, D), lambda i: (jnp.minimum(i, npt - 1), 0)),
                   pl.BlockSpec((tm, D), lambda i: (jnp.maximum(i - npt, 0), 0))],
        out_shape=[jax.ShapeDtypeStruct((tp, D), F32), jax.ShapeDtypeStruct((t - tp, D), F32)],
        scratch_shapes=[pltpu.VMEM((2, SORT_ROWS, HALF), U32), pltpu.SemaphoreType.DMA((2,))],
        compiler_params=_cp(("arbitrary",), 56),
        name="combine",
    )(src_lists, dst_lists, src_lists, dst_lists, out_sorted, tpos, gates, x1, fg)


def _rope_tables(length):
    half = DK // 2
    inv = ROPE_BASE ** (-jnp.arange(half, dtype=F32) / half)
    ang = jnp.arange(length, dtype=F32)[:, None] * inv[None, :]
    cos, sin = jnp.cos(ang), jnp.sin(ang)
    return jnp.concatenate([cos, cos], axis=-1), jnp.concatenate([-sin, sin], axis=-1)


def _layer(xp, xs, n_prompt, prompt_len, sample_len, mix_norm_g, w_in, ret_norm_g, w_ret_o, conv_w, conv_b, lru_w_a,
           lru_b_a, lru_w_x, lru_b_x, lru_lambda, w_lru_o, w_out, moe_norm_g, w_router, b_router, w_e_gate,
           b_e_gate, w_e_up, b_e_up, w_e_down, b_e_down, final_norm_g):
    tp = xp.shape[0]
    t = tp + xs.shape[0]
    seqs = _Seqs(n_prompt, prompt_len, sample_len, SEQ_CHUNK)
    row = lambda v: v.reshape(1, -1)

    z = _inproj(xp, xs, row(mix_norm_g), w_in.astype(BF16))

    cosf, sinf = _rope_tables(max(prompt_len, sample_len))
    y_ret = _retention(z, cosf, sinf, row(ret_norm_g), seqs, SEQ_CHUNK)

    wa, wx = lru_w_a.astype(BF16), lru_w_x.astype(BF16)
    ba, bx = lru_b_a[:, None, :], lru_b_x[:, None, :]
    c8 = (-LRU_C * math.log2(math.e) * jax.nn.softplus(-lru_lambda))[:, None, :]
    lru_args = (z, conv_w, row(conv_b), wa, ba, wx, bx, c8)
    lru_seqs = _Seqs(n_prompt, prompt_len, sample_len, LRU_CHUNK)
    fwd_scan = _lru_call(*lru_args, None, 0, lru_seqs, LRU_CHUNK, LRU_CB)
    y_lru = _lru_call(*lru_args, fwd_scan, 1, lru_seqs, LRU_CHUNK, LRU_CB)

    merged = _proj_merge(y_ret, y_lru, w_ret_o.astype(BF16), w_lru_o.astype(BF16), z)
    x1 = _proj_out(merged, w_out.astype(BF16), xp, xs)

    w_pad = jnp.zeros((D, LANE), F32).at[:, :N_EXP].set(w_router)
    b_pad = jnp.full((1, LANE), -1e30, F32).at[0, :N_EXP].set(b_router)
    w_hi = w_pad.astype(BF16)
    w_lo = (w_pad - w_hi.astype(F32)).astype(BF16)
    xn2, tpos, tpos_t, gates, meta = _router(x1, row(moe_norm_g), w_hi, w_lo, b_pad, TOK_TILE)

    nt = t // TOK_TILE
    c8 = meta[:, 0, :N_EXP].astype(I32)
    toff = meta[:, 1, :N_EXP].astype(I32)
    run = meta[:, 2, :N_EXP].astype(I32)
    total = run[-1] + c8[-1]
    padded = (total + MOE_ROWS - 1) // MOE_ROWS * MOE_ROWS
    pad_ends = jnp.cumsum(padded)
    pad_starts = pad_ends - padded
    src_lists, dst_lists = _copy_lists(toff, c8, pad_starts[None, :] + run)
    n_blocks = (t * TOP_K + (SEG_ALIGN - 1) * nt * N_EXP) // MOE_ROWS + 1 + N_EXP
    n_used = (pad_ends[-1] // MOE_ROWS).astype(I32)
    zblk = jnp.concatenate([jnp.where(padded > 0, pad_ends - MOE_ROWS, -1), n_used.reshape(1)]).astype(I32)
    blk_ids = jnp.minimum(jnp.arange(n_blocks, dtype=I32), n_used - 1)
    block_expert = jnp.minimum(
        jnp.sum((pad_ends[None, :] <= (blk_ids * MOE_ROWS)[:, None]).astype(I32), axis=-1), N_EXP - 1)

    seg_end = pad_starts + total
    end_of = jnp.sum(jnp.where(block_expert[:, None] == jnp.arange(N_EXP, dtype=I32), seg_end[None, :], 0), axis=-1)
    in_use = jnp.clip(end_of - blk_ids * MOE_ROWS, 0, MOE_ROWS)
    half_full = (in_use <= MOE_ROWS // 2).astype(I32)
    used_and_half = jnp.concatenate([n_used.reshape(1), half_full])

    xg = _dispatch(zblk, src_lists, dst_lists, tpos_t, xn2, n_blocks * MOE_ROWS)
    out_sorted = _experts(block_expert, used_and_half, xg,
                          w_e_gate, b_e_gate[:, None, :], w_e_up, b_e_up[:, None, :],
                          w_e_down, b_e_down[:, None, :], MOE_ROWS, MOE_TF, MOE_TN)
    return _combine(src_lists, dst_lists, out_sorted, tpos, gates, x1, row(final_norm_g), tp, TOK_TILE)


def kernel(x_prompt, x_sample, mix_norm_g, w_in, ret_norm_g, w_ret_o, conv_w, conv_b, lru_w_a, lru_b_a, lru_w_x,
           lru_b_x, lru_lambda, w_lru_o, w_out, moe_norm_g, w_router, b_router, w_e_gate, b_e_gate, w_e_up, b_e_up,
           w_e_down, b_e_down, final_norm_g):
    assert mix_norm_g.shape[0] == 1, "one layer"
    n_prompt, prompt_len, _ = x_prompt.shape
    n_sample, sample_len, _ = x_sample.shape
    assert n_sample == 1
    tp = n_prompt * prompt_len
    yp, ys = _layer(x_prompt.reshape(tp, D), x_sample.reshape(sample_len, D), n_prompt, prompt_len, sample_len,
                    mix_norm_g[0], w_in[0], ret_norm_g[0], w_ret_o[0], conv_w[0],
                    conv_b[0], lru_w_a[0], lru_b_a[0], lru_w_x[0], lru_b_x[0], lru_lambda[0], w_lru_o[0], w_out[0],
                    moe_norm_g[0], w_router[0], b_router[0], w_e_gate[0], b_e_gate[0], w_e_up[0], b_e_up[0],
                    w_e_down[0], b_e_down[0], final_norm_g)
    return yp.reshape(x_prompt.shape), ys.reshape(x_sample.shape)
```

```python
import functools
import math

import jax
import jax.numpy as jnp
from jax import lax
from jax.experimental import pallas as pl
from jax.experimental.pallas import tpu as pltpu

F32 = jnp.float32
BF16 = jnp.bfloat16
U32 = jnp.uint32
I32 = jnp.int32

D = 2048
HEADS = 8
DK = 128
DV = 256
QK_W = HEADS * DK
V_W = HEADS * DV
LRU_BLOCK = 128
N_EXP = 32
TOP_K = 4
D_FF = 2048
SWIGLU_LIMIT = 7.0
SWIGLU_ALPHA = 1.702
ROPE_BASE = 10000.0
LRU_C = 8.0
EPS = 1e-6
IN_W = 2 * QK_W + 2 * V_W + 2 * D + 2 * D

HALF = D // 2
LANE = 128
HALO = 16
HI_MASK = 0xFFFF0000

SEQ_CHUNK = 256
LRU_CHUNK = 512
LRU_CB = 1024
LRU_SEG = 4
LRU_SUPER = 8 * LRU_SEG
MOE_ROWS = 512
MOE_TF = 1024
MOE_TN = 1024
TOK_TILE = 512
SEG_ALIGN = 8
PERM_CHUNK = 256
SORT_ROWS = TOK_TILE * 4 + 32 * SEG_ALIGN
SEG_MAXBIT = (TOK_TILE // SEG_ALIGN).bit_length() - 1


def _cp(sem, vmem_mb):
    return pltpu.CompilerParams(dimension_semantics=sem, vmem_limit_bytes=vmem_mb << 20)


def _bits(x):
    return lax.bitcast_convert_type(x, U32)


def _pack_pair(lo, hi, rounded=False):
    if not rounded:
        lo = lo.astype(BF16).astype(F32)
        hi = hi.astype(BF16).astype(F32)
    return (_bits(lo) >> 16) | (_bits(hi) & jnp.uint32(HI_MASK))


def _unpack_pair(u):
    lo = lax.bitcast_convert_type(u << 16, F32)
    hi = lax.bitcast_convert_type(u & jnp.uint32(HI_MASK), F32)
    return lo, hi


class _Seqs:
    def __init__(self, n_prompt, prompt_len, sample_len, chunk):
        assert prompt_len % chunk == 0 and sample_len % chunk == 0
        self.cps = prompt_len // chunk
        self.npc = n_prompt * self.cps
        self.nch = self.npc + sample_len // chunk

    def is_first(self, ci):
        return ((ci < self.npc) & (ci % self.cps == 0)) | (ci == self.npc)

    def is_last(self, ci):
        return ((ci < self.npc) & (ci % self.cps == self.cps - 1)) | (ci == self.nch - 1)

    def pos_chunk(self, ci):
        return jnp.where(ci < self.npc, ci % self.cps, ci - self.npc)


def _two_group_specs(tm, npt, width=D):
    return [pl.BlockSpec((tm, width), lambda i, j: (jnp.minimum(i, npt - 1), 0)),
            pl.BlockSpec((tm, width), lambda i, j: (jnp.maximum(i - npt, 0), 0))]


def _inproj_kernel(xp_ref, xs_ref, g_ref, w_ref, o_ref, xn_ref, *, npt):
    i = pl.program_id(0)

    def norm(x_ref):
        x = x_ref[...]
        y = x * lax.rsqrt(jnp.mean(x * x, axis=-1, keepdims=True) + EPS)
        xn_ref[...] = (y * g_ref[...]).astype(BF16)

    @pl.when(pl.program_id(1) == 0)
    def _():
        pl.when(i < npt)(lambda: norm(xp_ref))
        pl.when(i >= npt)(lambda: norm(xs_ref))

    o_ref[...] = jnp.dot(xn_ref[...], w_ref[...], preferred_element_type=F32).astype(o_ref.dtype)


def _inproj(xp, xs, g, w, tm=1024, tn=1024):
    t, n = xp.shape[0] + xs.shape[0], w.shape[1]
    npt = xp.shape[0] // tm
    return pl.pallas_call(
        functools.partial(_inproj_kernel, npt=npt),
        grid=(t // tm, n // tn),
        in_specs=_two_group_specs(tm, npt) + [pl.BlockSpec((1, D), lambda i, j: (0, 0)),
                                              pl.BlockSpec((D, tn), lambda i, j: (0, j))],
        out_specs=pl.BlockSpec((tm, tn), lambda i, j: (i, j)),
        out_shape=jax.ShapeDtypeStruct((t, n), BF16),
        scratch_shapes=[pltpu.VMEM((tm, D), BF16)],
        compiler_params=_cp(("arbitrary", "arbitrary"), 56),
        name="inproj",
    )(xp, xs, g, w)


def _rot(x, cosf, sinf):
    return x * cosf + pltpu.roll(x, DK // 2, 1) * sinf


def _log_gamma(h):
    return math.log1p(-(2.0 ** (-5 - h)))


def _decay_tables(dec_ref, c, q_exponent, k_exponent):
    p = lax.broadcasted_iota(I32, (c, DK), 0).astype(F32)
    for h in range(HEADS):
        lg = _log_gamma(h)
        dec_ref[h, 0] = jnp.exp(lg * q_exponent(p))
        dec_ref[h, 1] = jnp.exp(lg * k_exponent(p))


def _ret_bwd_kernel(q_ref, k_ref, v_ref, cos_ref, sin_ref, o_ref, qr_ref, kr_ref, s_ref, dec_ref, *, seqs, c):
    ci = seqs.nch - 1 - pl.program_id(0)

    @pl.when(pl.program_id(0) == 0)
    def _():
        _decay_tables(dec_ref, c, lambda p: c - p, lambda p: p)

    @pl.when(seqs.is_last(ci))
    def _():
        s_ref[...] = jnp.zeros_like(s_ref)

    cosf = cos_ref[...]
    sinf = sin_ref[...]
    for h in range(HEADS):
        lg = _log_gamma(h)
        q = _rot(q_ref[:, h * DK:(h + 1) * DK].astype(F32), cosf, sinf)
        k = _rot(k_ref[:, h * DK:(h + 1) * DK].astype(F32), cosf, sinf) * (DK ** -0.5)
        v = v_ref[:, h * DV:(h + 1) * DV]
        qr_ref[:, h * DK:(h + 1) * DK] = q.astype(BF16)
        kr_ref[:, h * DK:(h + 1) * DK] = k.astype(BF16)
        qd = (q * dec_ref[h, 0]).astype(BF16)
        kd = (k * dec_ref[h, 1]).astype(BF16)
        s = s_ref[h]
        o_ref[:, h * DV:(h + 1) * DV] = jnp.dot(qd, s.astype(BF16), preferred_element_type=F32)
        s_ref[h] = math.exp(lg * c) * s + lax.dot_general(
            kd, v, (((0,), (0,)), ((), ())), preferred_element_type=F32)


def _ret_fwd_kernel(q_ref, k_ref, v_ref, bwd_ref, gate_ref, gn_ref, o_ref, s_ref, dm_ref, dec_ref, *, seqs, c):
    ci = pl.program_id(0)

    @pl.when(ci == 0)
    def _():
        _decay_tables(dec_ref, c, lambda p: p + 1.0, lambda p: c - 1.0 - p)
        r = lax.broadcasted_iota(I32, (c, c), 0)
        col = lax.broadcasted_iota(I32, (c, c), 1)
        dist = jnp.abs(r - col).astype(F32)
        for h in range(HEADS):
            dm_ref[h] = jnp.exp(_log_gamma(h) * dist)

    @pl.when(seqs.is_first(ci))
    def _():
        s_ref[...] = jnp.zeros_like(s_ref)

    for h in range(HEADS):
        lg = _log_gamma(h)
        qb = q_ref[:, h * DK:(h + 1) * DK]
        kb = k_ref[:, h * DK:(h + 1) * DK]
        q = qb.astype(F32)
        k = kb.astype(F32)
        v = v_ref[:, h * DV:(h + 1) * DV]
        scores = lax.dot_general(qb, kb, (((1,), (1,)), ((), ())), preferred_element_type=F32) * dm_ref[h]
        tot = jnp.dot(scores.astype(BF16), v, preferred_element_type=F32)
        qd = (q * dec_ref[h, 0]).astype(BF16)
        kd = (k * dec_ref[h, 1]).astype(BF16)
        s = s_ref[h]
        tot = tot + jnp.dot(qd, s.astype(BF16), preferred_element_type=F32)
        s_ref[h] = math.exp(lg * c) * s + lax.dot_general(
            kd, v, (((0,), (0,)), ((), ())), preferred_element_type=F32)
        tot = tot + bwd_ref[:, h * DV:(h + 1) * DV]
        mu = jnp.mean(tot, axis=-1, keepdims=True)
        cen = tot - mu
        var = jnp.mean(cen * cen, axis=-1, keepdims=True)
        yn = cen * lax.rsqrt(var + EPS) * gn_ref[:, h * DV:(h + 1) * DV]
        g = gate_ref[:, h * DV:(h + 1) * DV].astype(F32)
        o_ref[:, h * DV:(h + 1) * DV] = (yn * (g * jax.nn.sigmoid(g))).astype(o_ref.dtype)


def _retention(z, cosf, sinf, gn, seqs, c):
    t = z.shape[0]
    nch = seqs.nch
    rev = lambda i: nch - 1 - i
    qkv_specs = lambda f: [pl.BlockSpec((c, QK_W), lambda i: (f(i), 0)),
                           pl.BlockSpec((c, QK_W), lambda i: (f(i), 1)),
                           pl.BlockSpec((c, V_W), lambda i: (f(i), 1)),
                           pl.BlockSpec((c, DK), lambda i: (seqs.pos_chunk(f(i)), 0)),
                           pl.BlockSpec((c, DK), lambda i: (seqs.pos_chunk(f(i)), 0))]
    bwd, q_rot, k_rot = pl.pallas_call(
        functools.partial(_ret_bwd_kernel, seqs=seqs, c=c),
        grid=(nch,),
        in_specs=qkv_specs(rev),
        out_specs=[pl.BlockSpec((c, V_W), lambda i: (rev(i), 0)),
                   pl.BlockSpec((c, QK_W), lambda i: (rev(i), 0)),
                   pl.BlockSpec((c, QK_W), lambda i: (rev(i), 0))],
        out_shape=[jax.ShapeDtypeStruct((t, V_W), F32), jax.ShapeDtypeStruct((t, QK_W), BF16),
                   jax.ShapeDtypeStruct((t, QK_W), BF16)],
        scratch_shapes=[pltpu.VMEM((HEADS, DK, DV), F32), pltpu.VMEM((HEADS, 2, c, DK), F32)],
        compiler_params=_cp(("arbitrary",), 32),
        name="ret_bwd",
    )(z, z, z, cosf, sinf)
    return pl.pallas_call(
        functools.partial(_ret_fwd_kernel, seqs=seqs, c=c),
        grid=(nch,),
        in_specs=[pl.BlockSpec((c, QK_W), lambda i: (i, 0)),
                  pl.BlockSpec((c, QK_W), lambda i: (i, 0)),
                  pl.BlockSpec((c, V_W), lambda i: (i, 1)),
                  pl.BlockSpec((c, V_W), lambda i: (i, 0)),
                  pl.BlockSpec((c, V_W), lambda i: (i, 2)),
                  pl.BlockSpec((1, V_W), lambda i: (0, 0))],
        out_specs=pl.BlockSpec((c, V_W), lambda i: (i, 0)),
        out_shape=jax.ShapeDtypeStruct((t, V_W), BF16),
        scratch_shapes=[pltpu.VMEM((HEADS, DK, DV), F32), pltpu.VMEM((HEADS, c, c), F32),
                        pltpu.VMEM((HEADS, 2, c, DK), F32)],
        compiler_params=_cp(("arbitrary",), 40),
        name="ret_fwd",
    )(q_rot, k_rot, z, bwd, z, gn)


def _lru_kernel(*refs, reverse, seqs, c, cb):
    if reverse:
        (xc_ref, wa_ref, ba_ref, wx_ref, bx_ref, c8_ref, hf_ref, g_ref, o_ref, a_s, u_s, h_s, carry_s) = refs
    else:
        (x_ref, xp_ref, xn_ref, cw_ref, cbias_ref, wa_ref, ba_ref, wx_ref, bx_ref, c8_ref,
         o_ref, xc_out_ref, a_s, u_s, h_s, carry_s) = refs
    t = pl.program_id(1)
    ci = seqs.nch - 1 - t if reverse else t
    first = seqs.is_first(ci)
    last = seqs.is_last(ci)

    if reverse:
        xc = xc_ref[...]
    else:
        x = x_ref[...].astype(F32)
        prev = jnp.where(first, 0.0, xp_ref[...].astype(F32))
        nxt = jnp.where(last, 0.0, xn_ref[...].astype(F32))
        row8 = lax.broadcasted_iota(I32, (8, 1), 0)

        def patch(arr, at, rows8):
            parts = ([arr[:at]] if at > 0 else []) + [rows8] + ([arr[at + 8:]] if at + 8 < c else [])
            return jnp.concatenate(parts, axis=0)

        xm1 = pltpu.roll(x, 1, 0)
        xm1 = patch(xm1, 0, jnp.where(row8 == 0, prev[HALO - 1:HALO], xm1[0:8]))
        xm2 = pltpu.roll(x, 2, 0)
        xm2 = patch(xm2, 0, jnp.where(row8 == 0, prev[HALO - 2:HALO - 1],
                                      jnp.where(row8 == 1, prev[HALO - 1:HALO], xm2[0:8])))
        xp1 = pltpu.roll(x, c - 1, 0)
        xp1 = patch(xp1, c - 8, jnp.where(row8 == 7, nxt[0:1], xp1[c - 8:c]))
        cw = cw_ref[...]
        xc = cw[0:1] * xm2 + cw[1:2] * xm1 + cw[2:3] * x + cw[3:4] * xp1 + cbias_ref[...]
        xc_out_ref[...] = xc

    nslab = cb // LRU_BLOCK
    for gi in range(nslab):
        sl = slice(gi * LRU_BLOCK, (gi + 1) * LRU_BLOCK)
        xs = xc[:, sl]
        xb = xs.astype(BF16)
        r = jax.nn.sigmoid(jnp.dot(xb, wa_ref[gi], preferred_element_type=F32) + ba_ref[:, sl])
        ig = jax.nn.sigmoid(jnp.dot(xb, wx_ref[gi], preferred_element_type=F32) + bx_ref[:, sl])
        a = jnp.exp2(c8_ref[:, sl] * r)
        a_s[gi] = a
        v = 1.0 - a * a
        u_s[gi] = jnp.where(v > 0.0, v * lax.rsqrt(v), 0.0) * (ig * xs)

    @pl.when(last if reverse else first)
    def _():
        carry_s[...] = jnp.zeros_like(carry_s)

    sub = lax.broadcasted_iota(I32, (8, LRU_BLOCK), 0)
    nsuper = c // LRU_SUPER
    steps = list(range(LRU_SEG))
    if reverse:
        steps = steps[::-1]

    def super_group(q, carries):
        qi = nsuper - 1 - q if reverse else q
        base = qi * LRU_SUPER
        out = []
        for gi in range(nslab):
            hs, ps = {}, {}
            h = p = None
            for j in steps:
                rows = pl.ds(base + j, 8, stride=LRU_SEG)
                a = a_s[gi, rows, :]
                u = u_s[gi, rows, :]
                h = u if h is None else a * h + u
                p = a if p is None else a * p
                hs[j], ps[j] = h, p
            eh, ep = h, p
            for s in (1, 2, 4):
                shift = 8 - s if reverse else s
                m = (sub < 8 - s) if reverse else (sub >= s)
                eh_sh = pltpu.roll(eh, shift, 0)
                ep_sh = pltpu.roll(ep, shift, 0)
                eh = eh + ep * jnp.where(m, eh_sh, 0.0)
                ep = ep * jnp.where(m, ep_sh, 1.0)
            end = eh + ep * carries[gi]
            if reverse:
                enter = jnp.where(sub == 7, carries[gi], pltpu.roll(end, 7, 0))
                out.append(jnp.broadcast_to(end[0:1], (8, LRU_BLOCK)))
            else:
                enter = jnp.where(sub == 0, carries[gi], pltpu.roll(end, 1, 0))
                out.append(jnp.broadcast_to(end[7:8], (8, LRU_BLOCK)))
            for j in steps:
                h_s[gi, pl.ds(base + j, 8, stride=LRU_SEG), :] = hs[j] + ps[j] * enter
        return tuple(out)

    carries = lax.fori_loop(0, nsuper, super_group, tuple(carry_s[gi] for gi in range(nslab)))
    for gi in range(nslab):
        carry_s[gi] = carries[gi]

    for gi in range(nslab):
        sl = slice(gi * LRU_BLOCK, (gi + 1) * LRU_BLOCK)
        if reverse:
            g = g_ref[:, sl].astype(F32)
            o_ref[:, sl] = ((hf_ref[:, sl] + h_s[gi]) * jax.nn.gelu(g, approximate=True)).astype(o_ref.dtype)
        else:
            o_ref[:, sl] = h_s[gi]


def _lru_call(z, conv_w, conv_b, wa, ba, wx, bx, c8, fwd, direction, seqs, c, cb):
    t = z.shape[0]
    nch = seqs.nch
    reverse = direction == 1
    tmap = (lambda ti: nch - 1 - ti) if reverse else (lambda ti: ti)
    xcol = (2 * QK_W + 2 * V_W) // cb
    gcol = (2 * QK_W + 2 * V_W + D) // cb
    hb = c // HALO
    nhb = t // HALO
    nb = cb // LRU_BLOCK
    tile = pl.BlockSpec((c, cb), lambda ch, ti: (tmap(ti), ch))
    gate_specs = [
        pl.BlockSpec((None, nb, LRU_BLOCK, LRU_BLOCK), lambda ch, ti: (direction, ch, 0, 0)),
        pl.BlockSpec((None, 1, cb), lambda ch, ti: (direction, 0, ch)),
        pl.BlockSpec((None, nb, LRU_BLOCK, LRU_BLOCK), lambda ch, ti: (direction, ch, 0, 0)),
        pl.BlockSpec((None, 1, cb), lambda ch, ti: (direction, 0, ch)),
        pl.BlockSpec((None, 1, cb), lambda ch, ti: (direction, 0, ch)),
    ]
    if reverse:
        h_fwd, xc = fwd
        in_specs = [tile] + gate_specs + [tile, pl.BlockSpec((c, cb), lambda ch, ti: (tmap(ti), gcol + ch))]
        args = [xc, wa, ba, wx, bx, c8, h_fwd, z]
        out_specs = tile
        out_shape = jax.ShapeDtypeStruct((t, D), BF16)
    else:
        in_specs = [
            pl.BlockSpec((c, cb), lambda ch, ti: (tmap(ti), xcol + ch)),
            pl.BlockSpec((HALO, cb), lambda ch, ti: (jnp.maximum(tmap(ti) * hb - 1, 0), xcol + ch)),
            pl.BlockSpec((HALO, cb), lambda ch, ti: (jnp.minimum((tmap(ti) + 1) * hb, nhb - 1), xcol + ch)),
            pl.BlockSpec((4, cb), lambda ch, ti: (0, ch)),
            pl.BlockSpec((1, cb), lambda ch, ti: (0, ch)),
        ] + gate_specs
        args = [z, z, z, conv_w, conv_b, wa, ba, wx, bx, c8]
        out_specs = [tile, tile]
        out_shape = [jax.ShapeDtypeStruct((t, D), F32), jax.ShapeDtypeStruct((t, D), F32)]
    return pl.pallas_call(
        functools.partial(_lru_kernel, reverse=reverse, seqs=seqs, c=c, cb=cb),
        grid=(D // cb, nch),
        in_specs=in_specs,
        out_specs=out_specs,
        out_shape=out_shape,
        scratch_shapes=[pltpu.VMEM((nb, c, LRU_BLOCK), F32), pltpu.VMEM((nb, c, LRU_BLOCK), F32),
                        pltpu.VMEM((nb, c, LRU_BLOCK), F32), pltpu.VMEM((nb, 8, LRU_BLOCK), F32)],
        compiler_params=_cp(("arbitrary", "arbitrary"),32),
        name="lru_rev" if reverse else "lru_fwd",
    )(*args)


def _proj_merge_kernel(yr_ref, yl_ref, wr_ref, wl_ref, mr_ref, ml_ref, o_ref):
    r = jnp.dot(yr_ref[...], wr_ref[...], preferred_element_type=F32)
    l = jnp.dot(yl_ref[...], wl_ref[...], preferred_element_type=F32)
    o_ref[...] = (jax.nn.sigmoid(mr_ref[...].astype(F32)) * r
                  + jax.nn.sigmoid(ml_ref[...].astype(F32)) * l).astype(o_ref.dtype)


def _proj_merge(y_ret, y_lru, w_ret_o, w_lru_o, z, tm=1024, tn=1024):
    t = y_ret.shape[0]
    mcol = (2 * QK_W + 2 * V_W + 2 * D) // tn
    return pl.pallas_call(
        _proj_merge_kernel,
        grid=(D // tn, t // tm),
        in_specs=[pl.BlockSpec((tm, V_W), lambda j, i: (i, 0)),
                  pl.BlockSpec((tm, D), lambda j, i: (i, 0)),
                  pl.BlockSpec((V_W, tn), lambda j, i: (0, j)),
                  pl.BlockSpec((D, tn), lambda j, i: (0, j)),
                  pl.BlockSpec((tm, tn), lambda j, i: (i, mcol + j)),
                  pl.BlockSpec((tm, tn), lambda j, i: (i, mcol + D // tn + j))],
        out_specs=pl.BlockSpec((tm, tn), lambda j, i: (i, j)),
        out_shape=jax.ShapeDtypeStruct((t, D), BF16),
        compiler_params=_cp(("arbitrary", "arbitrary"), 56),
        name="proj_merge",
    )(y_ret, y_lru, w_ret_o, w_lru_o, z, z)


def _proj_out_kernel(m_ref, w_ref, xp_ref, xs_ref, o_ref, *, npt):
    i = pl.program_id(1)
    y = jnp.dot(m_ref[...], w_ref[...], preferred_element_type=F32)

    @pl.when(i < npt)
    def _():
        o_ref[...] = xp_ref[...] + y

    @pl.when(i >= npt)
    def _():
        o_ref[...] = xs_ref[...] + y


def _proj_out(merged, w_out, xp, xs, tm=1024, tn=1024):
    t = merged.shape[0]
    npt = xp.shape[0] // tm
    return pl.pallas_call(
        functools.partial(_proj_out_kernel, npt=npt),
        grid=(D // tn, t // tm),
        in_specs=[pl.BlockSpec((tm, D), lambda j, i: (i, 0)),
                  pl.BlockSpec((D, tn), lambda j, i: (0, j)),
                  pl.BlockSpec((tm, tn), lambda j, i: (jnp.minimum(i, npt - 1), j)),
                  pl.BlockSpec((tm, tn), lambda j, i: (jnp.maximum(i - npt, 0), j))],
        out_specs=pl.BlockSpec((tm, tn), lambda j, i: (i, j)),
        out_shape=jax.ShapeDtypeStruct((t, D), F32),
        compiler_params=_cp(("arbitrary", "arbitrary"), 48),
        name="proj_out",
    )(merged, w_out, xp, xs)


def _router_kernel(x_ref, g_ref, whi_ref, wlo_ref, b_ref, xn_ref, tpos_ref, tpos_t_ref, gate_ref, meta_ref, tri_ref,
                   run_ref, *, tm):
    i = pl.program_id(0)

    @pl.when(i == 0)
    def _():
        r = lax.broadcasted_iota(I32, (tm, tm), 0)
        col = lax.broadcasted_iota(I32, (tm, tm), 1)
        tri_ref[...] = (col < r).astype(BF16)
        run_ref[...] = jnp.zeros_like(run_ref)

    x = x_ref[...]
    xn = x * lax.rsqrt(jnp.mean(x * x, axis=-1, keepdims=True) + EPS) * g_ref[...]
    xh = xn.astype(BF16)
    xn_ref[...] = xh

    xl = (xn - xh.astype(F32)).astype(BF16)
    logits = (jnp.dot(xh, whi_ref[...], preferred_element_type=F32)
              + (jnp.dot(xl, whi_ref[...], preferred_element_type=F32)
                 + jnp.dot(xh, wlo_ref[...], preferred_element_type=F32))) + b_ref[...]
    lane = lax.broadcasted_iota(I32, (tm, LANE), 1)
    lane_f = lane.astype(F32)
    vals, idxs = [], []
    cur = logits
    for _ in range(TOP_K):
        m = jnp.max(cur, axis=-1, keepdims=True)
        idx = jnp.min(jnp.where(cur == m, lane_f, float(LANE)), axis=-1, keepdims=True).astype(I32)
        vals.append(m)
        idxs.append(idx)
        cur = jnp.where(lane == idx, -jnp.inf, cur)
    exps = [jnp.exp(v - vals[0]) for v in vals]
    denom = exps[0] + exps[1] + exps[2] + exps[3]

    onehot = jnp.zeros((tm, LANE), F32)
    for idx in idxs:
        onehot = onehot + (lane == idx).astype(F32)
    before = jnp.dot(tri_ref[...], onehot.astype(BF16), preferred_element_type=F32)

    cnt = jnp.broadcast_to(jnp.sum(onehot, axis=0, keepdims=True), (8, LANE))
    cnt8 = jnp.floor((cnt + (SEG_ALIGN - 1)) * (1.0 / SEG_ALIGN)) * SEG_ALIGN
    lane8 = lax.broadcasted_iota(I32, (8, LANE), 1)
    incl = cnt8
    s = 1
    while s < LANE:
        incl = incl + jnp.where(lane8 >= s, pltpu.roll(incl, s, 1), 0.0)
        s *= 2
    toff = incl - cnt8

    pos = before + toff[0:1]
    tpos_out = jnp.zeros((tm, LANE), F32)
    gate_out = jnp.zeros((tm, LANE), F32)
    for k in range(TOP_K):
        tpos_k = jnp.sum(jnp.where(lane == idxs[k], pos, 0.0), axis=-1, keepdims=True)
        tpos_out = jnp.where(lane == k, tpos_k, tpos_out)
        gate_out = jnp.where(lane == k, exps[k] / denom, gate_out)
    tpos_ref[...] = tpos_out.astype(I32)
    tpos_t_ref[...] = tpos_out.T[0:8].astype(I32)
    gate_ref[...] = gate_out

    sub8 = lax.broadcasted_iota(I32, (8, LANE), 0)
    run = jnp.broadcast_to(run_ref[...], (8, LANE))
    meta_ref[...] = jnp.where(sub8 == 0, cnt8, jnp.where(sub8 == 1, toff, jnp.where(sub8 == 2, run, 0.0)))
    run_ref[...] = run_ref[...] + cnt8[0:1]


def _router(x1, g, w_hi, w_lo, b_pad, tm):
    t = x1.shape[0]
    nt = t // tm
    tok = lambda i: (i, 0)
    fixed = lambda i: (0, 0)
    per_tile = lambda i: (i, 0, 0)
    return pl.pallas_call(
        functools.partial(_router_kernel, tm=tm),
        grid=(nt,),
        in_specs=[pl.BlockSpec((tm, D), tok), pl.BlockSpec((1, D), fixed),
                  pl.BlockSpec((D, LANE), fixed), pl.BlockSpec((D, LANE), fixed), pl.BlockSpec((1, LANE), fixed)],
        out_specs=[pl.BlockSpec((tm, D), tok), pl.BlockSpec((tm, LANE), tok),
                   pl.BlockSpec((None, 8, tm), per_tile), pl.BlockSpec((tm, LANE), tok),
                   pl.BlockSpec((None, 8, LANE), per_tile)],
        out_shape=[jax.ShapeDtypeStruct((t, D), BF16), jax.ShapeDtypeStruct((t, LANE), I32),
                   jax.ShapeDtypeStruct((nt, 8, tm), I32), jax.ShapeDtypeStruct((t, LANE), F32),
                   jax.ShapeDtypeStruct((nt, 8, LANE), F32)],
        scratch_shapes=[pltpu.VMEM((tm, tm), BF16), pltpu.VMEM((1, LANE), F32)],
        compiler_params=_cp(("arbitrary",), 32),
        name="router",
    )(x1, g, w_hi, w_lo, b_pad)


N_CLASS = SEG_MAXBIT + 1
USED_LANE = N_CLASS + 1


def _copy_lists(toff, c8, dst):
    nt = c8.shape[0]
    n = (c8 // SEG_ALIGN)[:, None, :]
    b = jnp.arange(N_CLASS, dtype=I32)[None, :, None]
    bits = (n >> b) & 1
    off = ((n >> (b + 1)) << (b + 1)) * SEG_ALIGN
    order = jnp.argsort(1 - bits, axis=-1, stable=True)
    src = jnp.take_along_axis(toff[:, None, :] + off, order, axis=-1)
    dstl = jnp.take_along_axis(dst[:, None, :] + off, order, axis=-1)
    src_t = jnp.zeros((nt, 8, LANE), I32).at[:, :N_CLASS, :N_EXP].set(src)
    src_t = src_t.at[:, N_CLASS, :N_CLASS].set(jnp.sum(bits, axis=-1))
    src_t = src_t.at[:, N_CLASS, USED_LANE].set(toff[:, -1] + c8[:, -1])
    dst_t = jnp.zeros((nt, 8, LANE), I32).at[:, :N_CLASS, :N_EXP].set(dstl)
    return src_t, dst_t


def _class_copies(src_ref, dst_ref, make_copy, act):
    for b in range(N_CLASS):
        rows = SEG_ALIGN << b

        def body(s, carry, b=b, rows=rows):
            act(make_copy(pl.multiple_of(src_ref[b, s], SEG_ALIGN), pl.multiple_of(dst_ref[b, s], SEG_ALIGN), rows))
            return carry

        lax.fori_loop(0, src_ref[N_CLASS, b], body, 0)


def _dispatch_kernel(zblk_ref, src_ref, dst_ref, src_prev_ref, dst_prev_ref, tpos_t_ref, x_ref, xg_ref, sorted_ref,
                     zero_ref, sem, *, nt):
    i = pl.program_id(0)
    slot = i % 2

    def zero_copy(e):
        start = pl.multiple_of(jnp.maximum(zblk_ref[e], 0), MOE_ROWS)
        return pltpu.make_async_copy(zero_ref, xg_ref.at[pl.ds(start, MOE_ROWS), :], sem.at[0])

    def for_nonempty(act):
        def body(e, carry):
            pl.when(zblk_ref[e] >= 0)(lambda: act(zero_copy(e)))
            return carry
        lax.fori_loop(0, N_EXP, body, 0)

    def for_unused(act):
        def body(b, carry):
            start = pl.multiple_of(b * MOE_ROWS, MOE_ROWS)
            act(pltpu.make_async_copy(zero_ref, xg_ref.at[pl.ds(start, MOE_ROWS), :], sem.at[0]))
            return carry
        lax.fori_loop(zblk_ref[N_EXP], xg_ref.shape[0] // MOE_ROWS, body, 0)

    @pl.when(i == 0)
    def _():
        zero_ref[...] = jnp.zeros_like(zero_ref)
        for_nonempty(lambda cp: cp.start())
        for_unused(lambda cp: cp.start())
        for_nonempty(lambda cp: cp.wait())
        for_unused(lambda cp: cp.wait())

    x = x_ref[...]
    for c in range(SORT_ROWS // PERM_CHUNK):
        p = c * PERM_CHUNK + lax.broadcasted_iota(I32, (PERM_CHUNK, 1), 0)
        perm = jnp.zeros((PERM_CHUNK, x.shape[0]), F32)
        for k in range(TOP_K):
            perm = perm + jnp.where(p == tpos_t_ref[k:k + 1, :], 1.0, 0.0)
        rows = jnp.dot(perm.astype(BF16), x, preferred_element_type=F32)
        sorted_ref[slot, c * PERM_CHUNK:(c + 1) * PERM_CHUNK, :] = _pack_pair(rows[:, :HALF], rows[:, HALF:],
                                                                              rounded=True)

    def copies(lists, buf, act):
        def make_copy(t_off, d, rows):
            return pltpu.make_async_copy(sorted_ref.at[buf, pl.ds(t_off, rows), :], xg_ref.at[pl.ds(d, rows), :],
                                         sem.at[buf])
        _class_copies(*lists, make_copy, act)

    this_tile = (src_ref, dst_ref)
    pl.when(i > 0)(lambda: copies((src_prev_ref, dst_prev_ref), 1 - slot, lambda cp: cp.wait()))
    copies(this_tile, slot, lambda cp: cp.start())
    pl.when(i == nt - 1)(lambda: copies(this_tile, slot, lambda cp: cp.wait()))


def _dispatch(zblk, src_lists, dst_lists, tpos_t, xn, rows):
    nt, _, tm = tpos_t.shape
    this_tile = lambda i, *_: (i, 0, 0)
    prev_tile = lambda i, *_: (jnp.maximum(i - 1, 0), 0, 0)
    lists = lambda index_map: pl.BlockSpec((None, 8, LANE), index_map, memory_space=pltpu.SMEM)
    grid_spec = pltpu.PrefetchScalarGridSpec(
        num_scalar_prefetch=1,
        grid=(nt,),
        in_specs=[lists(this_tile), lists(this_tile), lists(prev_tile), lists(prev_tile),
                  pl.BlockSpec((None, 8, tm), this_tile),
                  pl.BlockSpec((tm, D), lambda i, *_: (i, 0))],
        out_specs=pl.BlockSpec(memory_space=pl.ANY),
        scratch_shapes=[pltpu.VMEM((2, SORT_ROWS, HALF), U32), pltpu.VMEM((MOE_ROWS, HALF), U32),
                        pltpu.SemaphoreType.DMA((2,))],
    )
    return pl.pallas_call(
        functools.partial(_dispatch_kernel, nt=nt),
        grid_spec=grid_spec,
        out_shape=jax.ShapeDtypeStruct((rows, HALF), U32),
        compiler_params=_cp(("arbitrary",), 40),
        name="dispatch",
    )(zblk, src_lists, dst_lists, src_lists, dst_lists, tpos_t, xn)


def _expert_changed(be_ref, i):
    return (i == 0) | (be_ref[i] != be_ref[jnp.maximum(i - 1, 0)])


def _expert_up_kernel(be_ref, nu_ref, xg_ref, wg_ref, bg_ref, wu_ref, bu_ref, h_ref, wgb_ref, wub_ref):
    i = pl.program_id(1)

    @pl.when(i < nu_ref[0])
    def _():
        @pl.when(_expert_changed(be_ref, i))
        def _():
            wgb_ref[...] = wg_ref[...].astype(BF16)
            wub_ref[...] = wu_ref[...].astype(BF16)

        lo, hi = _unpack_pair(xg_ref[...])
        xb = jnp.concatenate([lo.astype(BF16), hi.astype(BF16)], axis=-1)
        gate = jnp.minimum(jnp.dot(xb, wgb_ref[...], preferred_element_type=F32) + bg_ref[...], SWIGLU_LIMIT)
        up = jnp.clip(jnp.dot(xb, wub_ref[...], preferred_element_type=F32) + bu_ref[...],
                      -SWIGLU_LIMIT, SWIGLU_LIMIT)
        h_ref[...] = ((up + 1.0) * gate * jax.nn.sigmoid(SWIGLU_ALPHA * gate)).astype(h_ref.dtype)

    @pl.when(i >= nu_ref[0])
    def _():
        h_ref[...] = jnp.zeros_like(h_ref)


def _expert_down_kernel(be_ref, nu_ref, h_ref, wlo_ref, whi_ref, blo_ref, bhi_ref, o_ref, wlob_ref, whib_ref):
    i = pl.program_id(1)

    @pl.when(i < nu_ref[0])
    def _():
        @pl.when(_expert_changed(be_ref, i))
        def _():
            wlob_ref[...] = wlo_ref[...].astype(BF16)
            whib_ref[...] = whi_ref[...].astype(BF16)

        h = h_ref[...]
        lo = jnp.dot(h, wlob_ref[...], preferred_element_type=F32) + blo_ref[...]
        hi = jnp.dot(h, whib_ref[...], preferred_element_type=F32) + bhi_ref[...]
        o_ref[...] = _pack_pair(lo, hi)

    @pl.when(i >= nu_ref[0])
    def _():
        o_ref[...] = jnp.zeros_like(o_ref)


def _experts(block_expert, n_used, xg, wg, bg, wu, bu, wd, bd, rows_blk, tf, tn):
    rows = xg.shape[0]
    nb = rows // rows_blk
    nj = D_FF // tf
    nn = HALF // tn

    def blk(i, nu):
        return jnp.minimum(i, nu[0] - 1)

    up_spec = pltpu.PrefetchScalarGridSpec(
        num_scalar_prefetch=2,
        grid=(nj, nb),
        in_specs=[pl.BlockSpec((rows_blk, HALF), lambda j, i, be, nu: (blk(i, nu), 0)),
                  pl.BlockSpec((None, D, tf), lambda j, i, be, nu: (be[i], 0, j)),
                  pl.BlockSpec((None, 1, tf), lambda j, i, be, nu: (be[i], 0, j)),
                  pl.BlockSpec((None, D, tf), lambda j, i, be, nu: (be[i], 0, j)),
                  pl.BlockSpec((None, 1, tf), lambda j, i, be, nu: (be[i], 0, j))],
        out_specs=pl.BlockSpec((rows_blk, tf), lambda j, i, be, nu: (i, j)),
        scratch_shapes=[pltpu.VMEM((D, tf), BF16), pltpu.VMEM((D, tf), BF16)],
    )
    hid = pl.pallas_call(
        _expert_up_kernel,
        grid_spec=up_spec,
        out_shape=jax.ShapeDtypeStruct((rows, D_FF), BF16),
        compiler_params=_cp(("arbitrary", "arbitrary"), 60),
        name="experts_up",
    )(block_expert, n_used, xg, wg, bg, wu, bu)

    down_spec = pltpu.PrefetchScalarGridSpec(
        num_scalar_prefetch=2,
        grid=(nn, nb),
        in_specs=[pl.BlockSpec((rows_blk, D_FF), lambda n, i, be, nu: (blk(i, nu), 0)),
                  pl.BlockSpec((None, D_FF, tn), lambda n, i, be, nu: (be[i], 0, n)),
                  pl.BlockSpec((None, D_FF, tn), lambda n, i, be, nu: (be[i], 0, nn + n)),
                  pl.BlockSpec((None, 1, tn), lambda n, i, be, nu: (be[i], 0, n)),
                  pl.BlockSpec((None, 1, tn), lambda n, i, be, nu: (be[i], 0, nn + n))],
        out_specs=pl.BlockSpec((rows_blk, tn), lambda n, i, be, nu: (i, n)),
        scratch_shapes=[pltpu.VMEM((D_FF, tn), BF16), pltpu.VMEM((D_FF, tn), BF16)],
    )
    return pl.pallas_call(
        _expert_down_kernel,
        grid_spec=down_spec,
        out_shape=jax.ShapeDtypeStruct((rows, HALF), U32),
        compiler_params=_cp(("arbitrary", "arbitrary"), 60),
        name="experts_down",
    )(block_expert, n_used, hid, wd, wd, bd, bd)


def _combine_kernel(src_ref, dst_ref, src_next_ref, dst_next_ref, os_ref, tpos_ref, gate_ref, x1_ref, fg_ref,
                    op_ref, osm_ref, sorted_ref, sem, *, npt, nt):
    i = pl.program_id(0)
    slot = i % 2

    def copies(lists, buf, act):
        def make_copy(t_off, d, rows):
            return pltpu.make_async_copy(os_ref.at[pl.ds(d, rows), :], sorted_ref.at[buf, pl.ds(t_off, rows), :],
                                         sem.at[buf])
        _class_copies(*lists, make_copy, act)

    this_tile = (src_ref, dst_ref)
    pl.when(i == 0)(lambda: copies(this_tile, 0, lambda cp: cp.start()))
    pl.when(i + 1 < nt)(lambda: copies((src_next_ref, dst_next_ref), 1 - slot, lambda cp: cp.start()))
    copies(this_tile, slot, lambda cp: cp.wait())

    used = src_ref[N_CLASS, USED_LANE]
    tpos = tpos_ref[...]
    gates = gate_ref[...]
    y_lo = x1_ref[:, :HALF]
    y_hi = x1_ref[:, HALF:]
    for c in range(SORT_ROWS // PERM_CHUNK):
        p_lane = c * PERM_CHUNK + lax.broadcasted_iota(I32, (1, PERM_CHUNK), 1)
        place = jnp.zeros((tpos.shape[0], PERM_CHUNK), F32)
        for k in range(TOP_K):
            place = jnp.where(tpos[:, k:k + 1] == p_lane, gates[:, k:k + 1], place)
        p_row = c * PERM_CHUNK + lax.broadcasted_iota(I32, (PERM_CHUNK, 1), 0)
        u = jnp.where(p_row < used, sorted_ref[slot, c * PERM_CHUNK:(c + 1) * PERM_CHUNK, :], jnp.uint32(0))
        lo, hi = _unpack_pair(u)
        place = place.astype(BF16)
        y_lo = y_lo + jnp.dot(place, lo.astype(BF16), preferred_element_type=F32)
        y_hi = y_hi + jnp.dot(place, hi.astype(BF16), preferred_element_type=F32)
    ms = (jnp.sum(y_lo * y_lo, axis=-1, keepdims=True) + jnp.sum(y_hi * y_hi, axis=-1, keepdims=True)) / D
    inv = lax.rsqrt(ms + EPS)
    out = jnp.concatenate([y_lo * inv * fg_ref[:, :HALF], y_hi * inv * fg_ref[:, HALF:]], axis=-1)

    @pl.when(i < npt)
    def _():
        op_ref[...] = out

    @pl.when(i >= npt)
    def _():
        osm_ref[...] = out


def _combine(src_lists, dst_lists, out_sorted, tpos, gates, x1, fg, tp, tm):
    t = x1.shape[0]
    npt = tp // tm
    nt = t // tm
    tok = lambda i: (i, 0)
    this_tile = lambda i: (i, 0, 0)
    next_tile = lambda i: (jnp.minimum(i + 1, nt - 1), 0, 0)
    lists = lambda index_map: pl.BlockSpec((None, 8, LANE), index_map, memory_space=pltpu.SMEM)
    return pl.pallas_call(
        functools.partial(_combine_kernel, npt=npt, nt=nt),
        grid=(nt,),
        in_specs=[lists(this_tile), lists(this_tile), lists(next_tile), lists(next_tile),
                  pl.BlockSpec(memory_space=pl.ANY),
                  pl.BlockSpec((tm, LANE), tok), pl.BlockSpec((tm, LANE), tok), pl.BlockSpec((tm, D), tok),
                  pl.BlockSpec((1, D), lambda i: (0, 0))],
        out_specs=[pl.BlockSpec((tm, D), lambda i: (jnp.minimum(i, npt - 1), 0)),
                   pl.BlockSpec((tm, D), lambda i: (jnp.maximum(i - npt, 0), 0))],
        out_shape=[jax.ShapeDtypeStruct((tp, D), F32), jax.ShapeDtypeStruct((t - tp, D), F32)],
        scratch_shapes=[pltpu.VMEM((2, SORT_ROWS, HALF), U32), pltpu.SemaphoreType.DMA((2,))],
        compiler_params=_cp(("arbitrary",), 56),
        name="combine",
    )(src_lists, dst_lists, src_lists, dst_lists, out_sorted, tpos, gates, x1, fg)


def _rope_tables(length):
    half = DK // 2
    inv = ROPE_BASE ** (-jnp.arange(half, dtype=F32) / half)
    ang = jnp.arange(length, dtype=F32)[:, None] * inv[None, :]
    cos, sin = jnp.cos(ang), jnp.sin(ang)
    return jnp.concatenate([cos, cos], axis=-1), jnp.concatenate([-sin, sin], axis=-1)


def _layer(xp, xs, n_prompt, prompt_len, sample_len, mix_norm_g, w_in, ret_norm_g, w_ret_o, conv_w, conv_b, lru_w_a,
           lru_b_a, lru_w_x, lru_b_x, lru_lambda, w_lru_o, w_out, moe_norm_g, w_router, b_router, w_e_gate,
           b_e_gate, w_e_up, b_e_up, w_e_down, b_e_down, final_norm_g):
    tp = xp.shape[0]
    t = tp + xs.shape[0]
    seqs = _Seqs(n_prompt, prompt_len, sample_len, SEQ_CHUNK)
    row = lambda v: v.reshape(1, -1)

    z = _inproj(xp, xs, row(mix_norm_g), w_in.astype(BF16))

    cosf, sinf = _rope_tables(max(prompt_len, sample_len))
    y_ret = _retention(z, cosf, sinf, row(ret_norm_g), seqs, SEQ_CHUNK)

    wa, wx = lru_w_a.astype(BF16), lru_w_x.astype(BF16)
    ba, bx = lru_b_a[:, None, :], lru_b_x[:, None, :]
    c8 = (-LRU_C * math.log2(math.e) * jax.nn.softplus(-lru_lambda))[:, None, :]
    lru_args = (z, conv_w, row(conv_b), wa, ba, wx, bx, c8)
    lru_seqs = _Seqs(n_prompt, prompt_len, sample_len, LRU_CHUNK)
    fwd_scan = _lru_call(*lru_args, None, 0, lru_seqs, LRU_CHUNK, LRU_CB)
    y_lru = _lru_call(*lru_args, fwd_scan, 1, lru_seqs, LRU_CHUNK, LRU_CB)

    merged = _proj_merge(y_ret, y_lru, w_ret_o.astype(BF16), w_lru_o.astype(BF16), z)
    x1 = _proj_out(merged, w_out.astype(BF16), xp, xs)

    w_pad = jnp.zeros((D, LANE), F32).at[:, :N_EXP].set(w_router)
    b_pad = jnp.full((1, LANE), -1e30, F32).at[0, :N_EXP].set(b_router)
    w_hi = w_pad.astype(BF16)
    w_lo = (w_pad - w_hi.astype(F32)).astype(BF16)
    xn2, tpos, tpos_t, gates, meta = _router(x1, row(moe_norm_g), w_hi, w_lo, b_pad, TOK_TILE)

    nt = t // TOK_TILE
    c8 = meta[:, 0, :N_EXP].astype(I32)
    toff = meta[:, 1, :N_EXP].astype(I32)
    run = meta[:, 2, :N_EXP].astype(I32)
    total = run[-1] + c8[-1]
    padded = (total + MOE_ROWS - 1) // MOE_ROWS * MOE_ROWS
    pad_ends = jnp.cumsum(padded)
    pad_starts = pad_ends - padded
    src_lists, dst_lists = _copy_lists(toff, c8, pad_starts[None, :] + run)
    n_blocks = (t * TOP_K + (SEG_ALIGN - 1) * nt * N_EXP) // MOE_ROWS + 1 + N_EXP
    n_used = (pad_ends[-1] // MOE_ROWS).astype(I32)
    zblk = jnp.concatenate([jnp.where(padded > 0, pad_ends - MOE_ROWS, -1), n_used.reshape(1)]).astype(I32)
    blk_ids = jnp.minimum(jnp.arange(n_blocks, dtype=I32), n_used - 1)
    block_expert = jnp.minimum(
        jnp.sum((pad_ends[None, :] <= (blk_ids * MOE_ROWS)[:, None]).astype(I32), axis=-1), N_EXP - 1)

    xg = _dispatch(zblk, src_lists, dst_lists, tpos_t, xn2, n_blocks * MOE_ROWS)
    out_sorted = _experts(block_expert, n_used.reshape(1), xg,
                          w_e_gate, b_e_gate[:, None, :], w_e_up, b_e_up[:, None, :],
                          w_e_down, b_e_down[:, None, :], MOE_ROWS, MOE_TF, MOE_TN)
    return _combine(src_lists, dst_lists, out_sorted, tpos, gates, x1, row(final_norm_g), tp, TOK_TILE)


def kernel(x_prompt, x_sample, mix_norm_g, w_in, ret_norm_g, w_ret_o, conv_w, conv_b, lru_w_a, lru_b_a, lru_w_x,
           lru_b_x, lru_lambda, w_lru_o, w_out, moe_norm_g, w_router, b_router, w_e_gate, b_e_gate, w_e_up, b_e_up,
           w_e_down, b_e_down, final_norm_g):
    assert mix_norm_g.shape[0] == 1, "one layer"
    n_prompt, prompt_len, _ = x_prompt.shape
    n_sample, sample_len, _ = x_sample.shape
    assert n_sample == 1
    tp = n_prompt * prompt_len
    yp, ys = _layer(x_prompt.reshape(tp, D), x_sample.reshape(sample_len, D), n_prompt, prompt_len, sample_len,
                    mix_norm_g[0], w_in[0], ret_norm_g[0], w_ret_o[0], conv_w[0],
                    conv_b[0], lru_w_a[0], lru_b_a[0], lru_w_x[0], lru_b_x[0], lru_lambda[0], w_lru_o[0], w_out[0],
                    moe_norm_g[0], w_router[0], b_router[0], w_e_gate[0], b_e_gate[0], w_e_up[0], b_e_up[0],
                    w_e_down[0], b_e_down[0], final_norm_g)
    return yp.reshape(x_prompt.shape), ys.reshape(x_sample.shape)
```
